```python
import math
import jax, jax.numpy as jnp
from jax import lax
import numpy as np

D_MODEL = 4096
BATCH = 4
SEQ = 2048
DEPTH = 4
DEC_BATCH = 32
DEC_SEQ = 1
PAST_LEN = 8192
PAGE_SIZE = 128

A_HEADS = 4
A_DK = 256
A_DV = 512
A_CHUNK = 64
B_HEADS = 32
B_KV_HEADS = 4
B_HEAD_DIM = 64
B_WINDOW = 128
C_HEADS = 32
C_KV_HEADS = 8
C_HEAD_DIM = 128
C_PATTERNS = ((128, 1), (512, 4), (2048, 16))
C_MAX_WINDOW = 2048
BAND_BLOCK = 128
FF_RAW = -(-8 * D_MODEL // 3)
D_FF = -(-FF_RAW // 256) * 256
ROPE_THETA = 10000.0
NORM_EPS = 1e-6
ADA_SCALE = 0.5
EVEN_SPLIT_SIZES = (A_HEADS * A_DK, A_HEADS * A_DK, A_HEADS * A_DV, A_HEADS * A_DV, A_HEADS, A_HEADS,
                    B_HEADS * B_HEAD_DIM, B_KV_HEADS * B_HEAD_DIM, B_KV_HEADS * B_HEAD_DIM)
N_IN_EVEN = sum(EVEN_SPLIT_SIZES)
ODD_SPLIT_SIZES = (C_HEADS * C_HEAD_DIM, C_KV_HEADS * C_HEAD_DIM, C_KV_HEADS * C_HEAD_DIM)
N_IN_ODD = sum(ODD_SPLIT_SIZES)

kernel_name = 'hybrid_mlstm_swa_dilated_decoder_step'

F32 = jnp.float32


def _split_sizes(a, sizes):
    return jnp.split(a, np.cumsum(sizes)[:-1].tolist(), axis=-1)


def _rms(x, g):
    x32 = x.astype(F32)
    return x32 * lax.rsqrt(jnp.mean(x32 * x32, axis=-1, keepdims=True) + NORM_EPS) * g.astype(F32)


def _rope(x, pos):
    d = x.shape[-1]
    half = d // 2
    inv = jnp.exp(jnp.arange(half, dtype=F32) * (-2.0 * math.log(ROPE_THETA) / d))
    ang = pos.astype(F32)[:, None] * inv[None, :]
    cos = jnp.cos(ang)[None, :, None, :]
    sin = jnp.sin(ang)[None, :, None, :]
    x32 = x.astype(F32)
    x1, x2 = x32[..., :half], x32[..., half:]
    return jnp.concatenate([x1 * cos - x2 * sin, x2 * cos + x1 * sin], axis=-1).astype(x.dtype)


def _softmax_sink(s, sink):
    m = jnp.max(s, axis=-1)
    if sink is not None:
        m = jnp.maximum(m, sink)
    e = jnp.exp(s - m[..., None])
    l = jnp.sum(e, axis=-1)
    if sink is not None:
        l = l + jnp.exp(sink - m)
    return e / l[..., None], m + jnp.log(l)


def _band_attention(q, k, v, window, dilation, sink):
    nb_, s_, h_, d_ = q.shape
    kvh = k.shape[2]
    g_ = h_ // kvh
    n = s_ // dilation
    nw = window // dilation
    bs = BAND_BLOCK
    nblk = -(-n // bs)
    npad = nblk * bs

    def split(a):
        a = a.reshape(nb_, n, dilation, *a.shape[2:]).swapaxes(1, 2)
        a = jnp.pad(a, [(0, 0), (0, 0), (0, npad - n)] + [(0, 0)] * (a.ndim - 3))
        return a.reshape(nb_, dilation, nblk, bs, *a.shape[3:])

    def band(a):
        prev = jnp.concatenate([jnp.zeros_like(a[:, :, :1]), a[:, :, :-1]], axis=2)
        return jnp.concatenate([prev, a], axis=3)

    qs = split(q.reshape(nb_, s_, kvh, g_, d_)).astype(F32)
    k2 = band(split(k).astype(F32))
    v2 = band(split(v).astype(F32))
    s = jnp.einsum('brnqhgd,brnkhd->brnhgqk', qs, k2) * (d_ ** -0.5)
    qi = jnp.arange(bs)[:, None]
    kj = jnp.arange(2 * bs)[None, :]
    rel = qi + bs - kj
    kidx = jnp.arange(nblk)[:, None, None] * bs - bs + kj[None]
    valid = (rel >= 0)[None] & (rel <= nw)[None] & (kidx >= 0)
    s = jnp.where(valid[:, None, None], s, -jnp.inf)
    sk = None if sink is None else sink.astype(F32).reshape(kvh, g_, 1)
    p, lse = _softmax_sink(s, sk)
    o = jnp.einsum('brnhgqk,brnkhd->brnqhgd', p, v2)

    def merge(a):
        a = a.reshape(nb_, dilation, npad, *a.shape[4:])[:, :, :n]
        return a.swapaxes(1, 2).reshape(nb_, s_, *a.shape[3:])

    out = merge(o).reshape(nb_, s_, h_, d_)
    lse = merge(jnp.moveaxis(lse, -1, 3)).reshape(nb_, s_, h_)
    return out, lse


def _window_decode(q, k_ext, v_ext, q_off, window, dilation, sink):
    nb_, t_, h_, d_ = q.shape
    kvh = k_ext.shape[2]
    g_ = h_ // kvh
    dist = jnp.arange(window // dilation + 1) * dilation
    idx = q_off + jnp.arange(t_)[:, None] - dist[None, :]
    valid = idx >= 0
    idx = jnp.maximum(idx, 0)
    kg = k_ext[:, idx].astype(F32)
    vg = v_ext[:, idx].astype(F32)
    s = jnp.einsum('btkgd,btjkd->btkgj', q.reshape(nb_, t_, kvh, g_, d_).astype(F32), kg) * (d_ ** -0.5)
    s = jnp.where(valid[None, :, None, None, :], s, -jnp.inf)
    sk = None if sink is None else sink.astype(F32).reshape(kvh, g_)
    p, lse = _softmax_sink(s, sk)
    o = jnp.einsum('btkgj,btjkd->btkgd', p, vg)
    return o.reshape(nb_, t_, h_, d_), lse.reshape(nb_, t_, h_)


def _mlstm(q, k, v, ig, lf, c0, n0, m0, chunk):
    nb_, t_, h_, _ = q.shape
    dv = v.shape[-1]
    L = chunk if t_ % chunk == 0 else t_
    nc = t_ // L
    tril = jnp.tril(jnp.ones((L, L), dtype=bool))

    def to_chunks(a):
        return jnp.moveaxis(a.reshape(nb_, nc, L, *a.shape[2:]), 1, 0)

    def step(carry, xs):
        cm, nv, m = carry
        qc, kc, vc, ic, fc = xs
        b = jnp.cumsum(fc, axis=1)
        log_d = b[:, :, None, :] - b[:, None, :, :] + ic[:, None, :, :]
        log_d = jnp.where(tril[None, :, :, None], log_d, -jnp.inf)
        inter = b + m[:, None, :]
        mt = jnp.maximum(inter, jnp.max(log_d, axis=2))
        dmat = jnp.exp(log_d - mt[:, :, None, :])
        wi = jnp.exp(inter - mt)
        sc = jnp.einsum('bthd,bshd->btsh', qc, kc) * dmat
        num = jnp.einsum('btsh,bshv->bthv', sc, vc) + wi[..., None] * jnp.einsum('bthd,bhdv->bthv', qc, cm)
        den = jnp.sum(sc, axis=2) + wi * jnp.einsum('bthd,bhd->bth', qc, nv)
        hc = num / jnp.maximum(jnp.abs(den), jnp.exp(-mt))[..., None]
        b_last = b[:, -1]
        tail = b_last[:, None, :] - b + ic
        m_new = jnp.maximum(b_last + m, jnp.max(tail, axis=1))
        w_c = jnp.exp(b_last + m - m_new)
        w_s = jnp.exp(tail - m_new[:, None, :])
        cm = w_c[..., None, None] * cm + jnp.einsum('bshd,bshv->bhdv', kc * w_s[..., None], vc)
        nv = w_c[..., None] * nv + jnp.einsum('bsh,bshd->bhd', w_s, kc)
        return (cm, nv, m_new), hc

    (c1, n1, m1), hs = lax.scan(step, (c0, n0, m0), tuple(to_chunks(a) for a in (q, k, v, ig, lf)))
    return jnp.moveaxis(hs, 0, 1).reshape(nb_, t_, h_, dv), (c1, n1, m1)


def _even_mixer(h, pos, w_in, gate_b, head_norm, sinks, w_out, state):
    nb_, t_, _ = h.shape
    dt = h.dtype
    aq, ak, av, ao, ai, af, bq, bk, bv = _split_sizes(h @ w_in, EVEN_SPLIT_SIZES)
    gate_b = gate_b.astype(F32)
    q = aq.reshape(nb_, t_, A_HEADS, A_DK).astype(F32) * (A_DK ** -0.5)
    k = ak.reshape(nb_, t_, A_HEADS, A_DK).astype(F32)
    v = av.reshape(nb_, t_, A_HEADS, A_DV).astype(F32)
    ig = ai.astype(F32) + gate_b[:A_HEADS]
    lf = jax.nn.log_sigmoid(af.astype(F32) + gate_b[A_HEADS:])
    if state is None:
        c0 = jnp.zeros((nb_, A_HEADS, A_DK, A_DV), F32)
        n0 = jnp.zeros((nb_, A_HEADS, A_DK), F32)
        m0 = jnp.zeros((nb_, A_HEADS), F32)
        chunk = A_CHUNK
    else:
        c0, n0, m0 = (s.astype(F32) for s in state[:3])
        chunk = t_
    ha, (c1, n1, m1) = _mlstm(q, k, v, ig, lf, c0, n0, m0, chunk)
    ha = _rms(ha, head_norm.reshape(A_HEADS, A_DV)) * jax.nn.sigmoid(ao.reshape(nb_, t_, A_HEADS, A_DV).astype(F32))
    qb = _rope(bq.reshape(nb_, t_, B_HEADS, B_HEAD_DIM), pos)
    kb = _rope(bk.reshape(nb_, t_, B_KV_HEADS, B_HEAD_DIM), pos)
    vb = bv.reshape(nb_, t_, B_KV_HEADS, B_HEAD_DIM)
    if state is None:
        ob, _ = _band_attention(qb, kb, vb, B_WINDOW, 1, sinks)
        kbuf, vbuf = kb[:, -B_WINDOW:], vb[:, -B_WINDOW:]
    else:
        nbuf = state[3].shape[1]
        k_ext = jnp.concatenate([state[3].astype(kb.dtype), kb], axis=1)
        v_ext = jnp.concatenate([state[4].astype(vb.dtype), vb], axis=1)
        ob, _ = _window_decode(qb, k_ext, v_ext, nbuf, B_WINDOW, 1, sinks)
        kbuf, vbuf = k_ext[:, -nbuf:], v_ext[:, -nbuf:]
    mixed = jnp.concatenate([ha.reshape(nb_, t_, -1), ob.reshape(nb_, t_, -1)], axis=-1).astype(dt)
    return mixed @ w_out, (c1.astype(dt), n1.astype(dt), m1.astype(dt), kbuf, vbuf)


def _odd_mixer(h, pos, w_in, w_out, state):
    nb_, t_, _ = h.shape
    cq, ck, cv = _split_sizes(h @ w_in, ODD_SPLIT_SIZES)
    q = _rope(cq.reshape(nb_, t_, C_HEADS, C_HEAD_DIM), pos)
    k = _rope(ck.reshape(nb_, t_, C_KV_HEADS, C_HEAD_DIM), pos)
    v = cv.reshape(nb_, t_, C_KV_HEADS, C_HEAD_DIM)
    if state is None:
        outs = [_band_attention(q, k, v, w, r, None) for (w, r) in C_PATTERNS]
        kbuf, vbuf = k[:, -C_MAX_WINDOW:], v[:, -C_MAX_WINDOW:]
    else:
        nbuf = state[0].shape[1]
        k_ext = jnp.concatenate([state[0].astype(k.dtype), k], axis=1)
        v_ext = jnp.concatenate([state[1].astype(v.dtype), v], axis=1)
        outs = [_window_decode(q, k_ext, v_ext, nbuf, w, r, None) for (w, r) in C_PATTERNS]
        kbuf, vbuf = k_ext[:, -nbuf:], v_ext[:, -nbuf:]
    o = jnp.stack([a for a, _ in outs])
    lse = jnp.stack([b for _, b in outs])
    wts = jax.nn.softmax(lse, axis=0)
    o = jnp.sum(wts[..., None] * o, axis=0)
    return o.reshape(nb_, t_, -1).astype(h.dtype) @ w_out, (kbuf, vbuf)


def _layer(x, c, mix_fn, ada_w, ada_b, g_mix_pre, g_mix_post, g_ffn_pre, g_ffn_post, w_gate, w_up, w_down):
    dt = x.dtype
    mod = (jax.nn.silu(c) @ ada_w + ada_b).astype(F32)[:, None, :]
    sh1, sc1, gt1, sh2, sc2, gt2 = jnp.split(mod, 6, axis=-1)
    h = (_rms(x, g_mix_pre) * (1.0 + sc1) + sh1).astype(dt)
    y, state = mix_fn(h)
    x = x + (gt1 * _rms(y, g_mix_post)).astype(dt)
    h = (_rms(x, g_ffn_pre) * (1.0 + sc2) + sh2).astype(dt)
    f = (jax.nn.silu(h @ w_gate) * (h @ w_up)) @ w_down
    x = x + (gt2 * _rms(f, g_ffn_post)).astype(dt)
    return x, state


def setup_inputs(seed: int = 0) -> dict:
    key = jax.random.key(seed)
    keys = iter(jax.random.split(key, 64))

    def nrm(shape, scale=1.0):
        return jax.random.normal(next(keys), shape, F32) * scale

    n_even = (DEPTH + 1) // 2
    n_odd = DEPTH // 2
    b_buf = min(B_WINDOW, PAST_LEN)
    c_buf = min(C_MAX_WINDOW, PAST_LEN)
    inp = {'x_prompt': nrm((BATCH, SEQ, D_MODEL)), 'x_sample': nrm((DEC_BATCH, DEC_SEQ, D_MODEL))}
    for li in range(DEPTH):
        if li % 2 == 0:
            inp[f'state_l{li}_mlstm_c'] = nrm((DEC_BATCH, A_HEADS, A_DK, A_DV))
            inp[f'state_l{li}_mlstm_n'] = nrm((DEC_BATCH, A_HEADS, A_DK))
            inp[f'state_l{li}_mlstm_m'] = nrm((DEC_BATCH, A_HEADS))
            inp[f'cache_l{li}_swa_k'] = nrm((DEC_BATCH, b_buf, B_KV_HEADS, B_HEAD_DIM))
            inp[f'cache_l{li}_swa_v'] = nrm((DEC_BATCH, b_buf, B_KV_HEADS, B_HEAD_DIM))
        else:
            inp[f'cache_l{li}_dil_k'] = nrm((DEC_BATCH, c_buf, C_KV_HEADS, C_HEAD_DIM))
            inp[f'cache_l{li}_dil_v'] = nrm((DEC_BATCH, c_buf, C_KV_HEADS, C_HEAD_DIM))
    inp['c_prompt'] = nrm((BATCH, D_MODEL))
    inp['c_sample'] = nrm((DEC_BATCH, D_MODEL))
    inp['ada_w'] = nrm((DEPTH, D_MODEL, 6 * D_MODEL), ADA_SCALE * D_MODEL ** -0.5)
    inp['ada_b'] = nrm((DEPTH, 6 * D_MODEL), 0.02)
    for name in ('norm_mix_pre', 'norm_mix_post', 'norm_ffn_pre', 'norm_ffn_post'):
        inp[name] = 1.0 + nrm((DEPTH, D_MODEL), 0.02)
    inp['even_w_in'] = nrm((n_even, D_MODEL, N_IN_EVEN), D_MODEL ** -0.5)
    f_bias = jnp.concatenate([jnp.zeros((A_HEADS,), F32), jnp.linspace(3.0, 6.0, A_HEADS, dtype=F32)])
    inp['even_gate_b'] = nrm((n_even, 2 * A_HEADS), 0.1) + f_bias
    inp['even_head_norm'] = 1.0 + nrm((n_even, A_HEADS * A_DV), 0.02)
    inp['even_sinks'] = nrm((n_even, B_HEADS), 0.5)
    mix_w = A_HEADS * A_DV + B_HEADS * B_HEAD_DIM
    inp['even_w_out'] = nrm((n_even, mix_w, D_MODEL), mix_w ** -0.5)
    inp['odd_w_in'] = nrm((n_odd, D_MODEL, N_IN_ODD), D_MODEL ** -0.5)
    inp['odd_w_out'] = nrm((n_odd, C_HEADS * C_HEAD_DIM, D_MODEL), (C_HEADS * C_HEAD_DIM) ** -0.5)
    inp['ffn_w_gate'] = nrm((DEPTH, D_MODEL, D_FF), D_MODEL ** -0.5)
    inp['ffn_w_up'] = nrm((DEPTH, D_MODEL, D_FF), D_MODEL ** -0.5)
    inp['ffn_w_down'] = nrm((DEPTH, D_FF, D_MODEL), D_FF ** -0.5)
    return inp


def reference(x_prompt, x_sample,
              state_l0_mlstm_c, state_l0_mlstm_n, state_l0_mlstm_m, cache_l0_swa_k, cache_l0_swa_v,
              cache_l1_dil_k, cache_l1_dil_v,
              state_l2_mlstm_c, state_l2_mlstm_n, state_l2_mlstm_m, cache_l2_swa_k, cache_l2_swa_v,
              cache_l3_dil_k, cache_l3_dil_v,
              c_prompt, c_sample,
              ada_w, ada_b, norm_mix_pre, norm_mix_post, norm_ffn_pre, norm_ffn_post,
              even_w_in, even_gate_b, even_head_norm, even_sinks, even_w_out,
              odd_w_in, odd_w_out, ffn_w_gate, ffn_w_up, ffn_w_down):
    past = {0: (state_l0_mlstm_c, state_l0_mlstm_n, state_l0_mlstm_m, cache_l0_swa_k, cache_l0_swa_v),
            1: (cache_l1_dil_k, cache_l1_dil_v),
            2: (state_l2_mlstm_c, state_l2_mlstm_n, state_l2_mlstm_m, cache_l2_swa_k, cache_l2_swa_v),
            3: (cache_l3_dil_k, cache_l3_dil_v)}
    pos_p = jnp.arange(x_prompt.shape[1])
    pos_s = PAST_LEN + jnp.arange(x_sample.shape[1])
    y_p, y_s = x_prompt, x_sample
    new_p, new_s = [], []
    for li in range(DEPTH):
        lw = (ada_w[li], ada_b[li], norm_mix_pre[li], norm_mix_post[li], norm_ffn_pre[li], norm_ffn_post[li],
              ffn_w_gate[li], ffn_w_up[li], ffn_w_down[li])
        j = li // 2
        if li % 2 == 0:
            mw = (even_w_in[j], even_gate_b[j], even_head_norm[j], even_sinks[j], even_w_out[j])
            mix_p = lambda h, mw=mw: _even_mixer(h, pos_p, *mw, None)
            mix_s = lambda h, mw=mw, st=past[li]: _even_mixer(h, pos_s, *mw, st)
        else:
            mw = (odd_w_in[j], odd_w_out[j])
            mix_p = lambda h, mw=mw: _odd_mixer(h, pos_p, *mw, None)
            mix_s = lambda h, mw=mw, st=past[li]: _odd_mixer(h, pos_s, *mw, st)
        y_p, st_p = _layer(y_p, c_prompt, mix_p, *lw)
        y_s, st_s = _layer(y_s, c_sample, mix_s, *lw)
        new_p.append(st_p)
        new_s.append(st_s)
    (p0c, p0n, p0m, p0k, p0v), (p1k, p1v), (p2c, p2n, p2m, p2k, p2v), (p3k, p3v) = new_p
    (s0c, s0n, s0m, s0k, s0v), (s1k, s1v), (s2c, s2n, s2m, s2k, s2v), (s3k, s3v) = new_s
    return (y_p, y_s,
            p0c, p0n, p0m, p0k, p0v, p1k, p1v, p2c, p2n, p2m, p2k, p2v, p3k, p3v,
            s0c, s0n, s0m, s0k, s0v, s1k, s1v, s2c, s2n, s2m, s2k, s2v, s3k, s3v)
```

```python
import functools
import math

import jax
import jax.numpy as jnp
import numpy as np
from jax import lax
from jax.experimental import pallas as pl
from jax.experimental.pallas import tpu as pltpu

F32 = jnp.float32
BF16 = jnp.bfloat16

D_MODEL = 4096
DEPTH = 4
PAST_LEN = 8192
A_HEADS, A_DK, A_DV, A_CHUNK = 4, 256, 512, 64
B_HEADS, B_KV_HEADS, B_HEAD_DIM, B_WINDOW = 32, 4, 64, 128
C_HEADS, C_KV_HEADS, C_HEAD_DIM = 32, 8, 128
C_PATTERNS = ((128, 1), (512, 4), (2048, 16))
BAND = 128
ROPE_THETA = 10000.0
NORM_EPS = 1e-6
LANES = 128
NEG = -1e30
VMEM_LIMIT = 56 * 1024 * 1024


def _cparams(sem):
    return pltpu.CompilerParams(dimension_semantics=sem, vmem_limit_bytes=VMEM_LIMIT)


def _mm_kernel(*refs, n_in):
    o_ref = refs[-1]
    acc = None
    for i in range(n_in):
        x = refs[2 * i][...].astype(BF16)
        w = refs[2 * i + 1][...].astype(BF16)
        d = jnp.dot(x, w, preferred_element_type=F32)
        acc = d if acc is None else acc + d
    o_ref[...] = acc.astype(o_ref.dtype)


def _matmul(pairs, out_dtype, tm, tn, name):
    m = pairs[0][0].shape[0]
    n = pairs[0][1].shape[1]
    tm = min(tm, m)
    tn = min(tn, n)
    assert m % tm == 0 and n % tn == 0, (m, n, tm, tn)
    in_specs, args = [], []
    for x, w in pairs:
        k = x.shape[1]
        in_specs += [pl.BlockSpec((tm, k), lambda i, j: (i, 0)), pl.BlockSpec((k, tn), lambda i, j: (0, j))]
        args += [x, w]
    return pl.pallas_call(
        functools.partial(_mm_kernel, n_in=len(pairs)),
        grid=(m // tm, n // tn),
        in_specs=in_specs,
        out_specs=pl.BlockSpec((tm, tn), lambda i, j: (i, j)),
        out_shape=jax.ShapeDtypeStruct((m, n), out_dtype),
        compiler_params=_cparams(("parallel", "arbitrary")),
        name=name,
    )(*args)


def _swiglu_kernel(x_ref, wg_ref, wu_ref, o_ref):
    x = x_ref[...]
    g = jnp.dot(x, wg_ref[...], preferred_element_type=F32)
    u = jnp.dot(x, wu_ref[...], preferred_element_type=F32)
    o_ref[...] = (g * jax.nn.sigmoid(g) * u).astype(o_ref.dtype)


def _swiglu(x, wg, wu, tm, tn):
    m, k = x.shape
    n = wg.shape[1]
    tm = min(tm, m)
    assert m % tm == 0 and n % tn == 0
    return pl.pallas_call(
        _swiglu_kernel,
        grid=(m // tm, n // tn),
        in_specs=[pl.BlockSpec((tm, k), lambda i, j: (i, 0)),
                  pl.BlockSpec((k, tn), lambda i, j: (0, j)),
                  pl.BlockSpec((k, tn), lambda i, j: (0, j))],
        out_specs=pl.BlockSpec((tm, tn), lambda i, j: (i, j)),
        out_shape=jax.ShapeDtypeStruct((m, n), BF16),
        compiler_params=_cparams(("parallel", "arbitrary")),
        name="swiglu",
    )(x, wg, wu)


def _ada_kernel(c_ref, w_ref, b_ref, o_ref):
    o_ref[...] = jnp.dot(c_ref[...], w_ref[...].astype(BF16), preferred_element_type=F32) + b_ref[...]


def _ada(sc_all, ada_w, ada_b, tn=512):
    nl, k, n = ada_w.shape
    r = sc_all.shape[0]
    return pl.pallas_call(
        _ada_kernel,
        grid=(nl, n // tn),
        in_specs=[pl.BlockSpec((r, k), lambda l, j: (0, 0)),
                  pl.BlockSpec((None, k, tn), lambda l, j: (l, 0, j)),
                  pl.BlockSpec((None, 1, tn), lambda l, j: (l, 0, j))],
        out_specs=pl.BlockSpec((None, r, tn), lambda l, j: (l, 0, j)),
        out_shape=jax.ShapeDtypeStruct((nl, r, n), F32),
        compiler_params=_cparams(("parallel", "arbitrary")),
        name="ada",
    )(sc_all, ada_w, ada_b.astype(F32).reshape(nl, 1, n))


def _rope_tables(pos, d):
    half = d // 2
    inv = jnp.exp(jnp.arange(half, dtype=F32) * (-2.0 * math.log(ROPE_THETA) / d))
    ang = pos.astype(F32)[:, None] * inv[None, :]
    cos, sin = jnp.cos(ang), jnp.sin(ang)
    reps = LANES // d
    return (jnp.tile(jnp.concatenate([cos, cos], axis=-1), (1, reps)),
            jnp.tile(jnp.concatenate([-sin, sin], axis=-1), (1, reps)))


def _rope_lanes(x, cos, sin, d):
    if d == LANES:
        partner = pltpu.roll(x, LANES // 2, axis=1)
    else:
        lane = lax.broadcasted_iota(jnp.int32, x.shape, 1)
        half = d // 2
        partner = jnp.where((lane % d) < half, pltpu.roll(x, LANES - half, axis=1), pltpu.roll(x, half, axis=1))
    return x * cos + partner * sin


def _band_attn_kernel(*refs, patterns, d, kv_per_block, n_units, seq, scale, has_sink):
    if has_sink:
        sink_ref, refs = refs[0], refs[1:]
    q_refs, refs = refs[:n_units], refs[n_units:]
    k_ref, v_ref, cos_ref, sin_ref, o_ref, krot_ref = refs[:6]
    scratch = refs[6:]
    n_pat = len(patterns)
    if n_pat > 1:
        acc_scr, m_scr, l_scr = scratch
    heads_per_unit = LANES // d
    units_per_kv = n_units // kv_per_block
    n_rs = units_per_kv * heads_per_unit
    rows_q = n_rs * BAND

    chunk = min(256, seq)

    def rope_k(i, _):
        rows = pl.ds(pl.multiple_of(i * chunk, chunk), chunk)
        krot_ref[rows, :] = _rope_lanes(k_ref[rows, :], cos_ref[rows, :], sin_ref[rows, :], d)
        return 0

    lax.fori_loop(0, seq // chunk, rope_k, 0)

    lane_q = lax.broadcasted_iota(jnp.int32, (BAND, LANES), 1)
    jq = lax.broadcasted_iota(jnp.int32, (BAND, 2 * BAND), 0)
    jk = lax.broadcasted_iota(jnp.int32, (BAND, 2 * BAND), 1)
    valid_rest = (jk >= jq) & (jk <= jq + BAND)
    valid_first = (lax.broadcasted_iota(jnp.int32, (BAND, BAND), 1)
                   <= lax.broadcasted_iota(jnp.int32, (BAND, BAND), 0))

    def dup(a, j):
        if d == LANES:
            return a
        lane = lax.broadcasted_iota(jnp.int32, a.shape, 1)
        rolled = pltpu.roll(a, LANES // 2, axis=1)
        return jnp.where(lane < d, a, rolled) if j == 0 else jnp.where(lane < d, rolled, a)

    def block(p_idx, r, qs, ks, nk, valid, kvj):
        first, last = p_idx == 0, p_idx == n_pat - 1
        rows = pl.ds(qs, BAND, stride=r) if r > 1 else pl.ds(qs, BAND)
        krows = pl.ds(ks, nk, stride=r) if r > 1 else pl.ds(ks, nk)
        cosq, sinq = cos_ref[rows, :], sin_ref[rows, :]
        parts = []
        for u in range(units_per_kv):
            xr = _rope_lanes(q_refs[kvj * units_per_kv + u][rows, :], cosq, sinq, d)
            if heads_per_unit == 1:
                parts.append(xr)
            else:
                parts += [jnp.where(lane_q < d, xr, 0.0), jnp.where(lane_q >= d, xr, 0.0)]
        qst = jnp.concatenate(parts, axis=0).astype(BF16)
        kk = dup(krot_ref[krows, :], kvj).astype(BF16)
        vv = dup(v_ref[krows, :], kvj).astype(BF16)
        vaug = jnp.concatenate([vv, jnp.ones((nk, LANES), BF16)], axis=1)
        s = lax.dot_general(qst, kk, (((1,), (1,)), ((), ())), preferred_element_type=F32) * scale
        s = jnp.where(valid[None], s.reshape(n_rs, BAND, nk), NEG).reshape(rows_q, nk)
        m_blk = jnp.max(s, axis=-1, keepdims=True)
        if first:
            if has_sink:
                m_old = jnp.concatenate(
                    [jnp.full((BAND, 1), sink_ref[(pl.program_id(1) * kv_per_block + kvj) * n_rs + h], F32)
                     for h in range(n_rs)], axis=0)
                l_old = 1.0
            else:
                m_old, l_old = jnp.full((rows_q, 1), NEG, F32), 0.0
        else:
            m_old = jnp.concatenate([m_scr[h, rows, :] for h in range(n_rs)], axis=0)[:, :1]
        m_new = jnp.maximum(m_old, m_blk)
        p = jnp.exp(s - m_new)
        pv = jnp.dot(p.astype(BF16), vaug, preferred_element_type=F32)
        alpha = jnp.exp(m_old - m_new)
        if first:
            acc = pv[:, :LANES]
            l_new = pv[:, LANES:] + alpha * l_old
        else:
            acc = alpha * jnp.concatenate([acc_scr[h, rows, :] for h in range(n_rs)], axis=0) + pv[:, :LANES]
            l_new = alpha * jnp.concatenate([l_scr[h, rows, :] for h in range(n_rs)], axis=0) + pv[:, LANES:]
        if last:
            out = acc / l_new
            for u in range(units_per_kv):
                lo = (kvj * units_per_kv + u) * LANES
                if heads_per_unit == 1:
                    piece = out[u * BAND:(u + 1) * BAND]
                else:
                    piece = jnp.where(lane_q < d, out[2 * u * BAND:(2 * u + 1) * BAND],
                                      out[(2 * u + 1) * BAND:(2 * u + 2) * BAND])
                o_ref[rows, lo:lo + LANES] = piece.astype(o_ref.dtype)
        else:
            m_b = jnp.broadcast_to(m_new, (rows_q, LANES))
            for h in range(n_rs):
                acc_scr[h, rows, :] = acc[h * BAND:(h + 1) * BAND]
                l_scr[h, rows, :] = l_new[h * BAND:(h + 1) * BAND]
                m_scr[h, rows, :] = m_b[h * BAND:(h + 1) * BAND]

    for kvj in range(kv_per_block):
        for p_idx, (window, r) in enumerate(patterns):
            assert window // r == BAND
            n = seq // r
            nblk = n // BAND
            for c in range(r):
                block(p_idx, r, c, c, BAND, valid_first, kvj)
                if nblk > 1:
                    def body(i, _, p_idx=p_idx, r=r, c=c, kvj=kvj):
                        qs = c + r * BAND * i
                        block(p_idx, r, qs, qs - r * BAND, 2 * BAND, valid_rest, kvj)
                        return 0
                    lax.fori_loop(1, nblk, body, 0)


def _band_attention(y3, cos, sin, sinks, *, patterns, d, n_q_heads, n_kv_heads, q_col, k_col, v_col, name):
    nb, seq, _ = y3.shape
    kv_per_block = LANES // d
    n_kvb = n_kv_heads // kv_per_block
    qw = n_q_heads * d // n_kvb
    n_rs = n_q_heads // n_kv_heads
    n_units = qw // LANES
    kern = functools.partial(_band_attn_kernel, patterns=patterns, d=d, kv_per_block=kv_per_block,
                             n_units=n_units, seq=seq, scale=d ** -0.5, has_sink=sinks is not None)
    q_spec = lambda u: pl.BlockSpec((None, seq, LANES), lambda b, j, *_: (b, 0, q_col // LANES + j * n_units + u))
    in_specs = [q_spec(u) for u in range(n_units)] + [
                pl.BlockSpec((None, seq, LANES), lambda b, j, *_: (b, 0, k_col // LANES + j)),
                pl.BlockSpec((None, seq, LANES), lambda b, j, *_: (b, 0, v_col // LANES + j)),
                pl.BlockSpec((seq, LANES), lambda b, j, *_: (0, 0)),
                pl.BlockSpec((seq, LANES), lambda b, j, *_: (0, 0))]
    out_specs = [pl.BlockSpec((None, seq, qw), lambda b, j, *_: (b, 0, j)),
                 pl.BlockSpec((None, seq, LANES), lambda b, j, *_: (b, 0, j))]
    out_shape = [jax.ShapeDtypeStruct((nb, seq, n_q_heads * d), BF16),
                 jax.ShapeDtypeStruct((nb, seq, n_kv_heads * d), F32)]
    scratch = []
    if len(patterns) > 1:
        scratch = [pltpu.VMEM((n_rs, seq, LANES), F32) for _ in range(3)]
    assert q_col % qw == 0 and k_col % LANES == 0 and v_col % LANES == 0
    if sinks is None:
        gs = pl.GridSpec(grid=(nb, n_kvb), in_specs=in_specs, out_specs=out_specs, scratch_shapes=scratch)
        args = (y3,) * (n_units + 2) + (cos, sin)
    else:
        gs = pltpu.PrefetchScalarGridSpec(num_scalar_prefetch=1, grid=(nb, n_kvb), in_specs=in_specs,
                                          out_specs=out_specs, scratch_shapes=scratch)
        args = (sinks.astype(F32),) + (y3,) * (n_units + 2) + (cos, sin)
    return pl.pallas_call(kern, grid_spec=gs, out_shape=out_shape,
                          compiler_params=_cparams(("parallel", "arbitrary")), name=name)(*args)


def _mlstm_kernel(q_ref, k_ref, v_ref, og_ref, igc_ref, lfc_ref, igr_ref, lfr_ref, hn_ref,
                  o_ref, c_ref, n_ref, m_ref, *, seq, chunk):
    L = chunk
    ri = lax.broadcasted_iota(jnp.int32, (L, L), 0)
    ci = lax.broadcasted_iota(jnp.int32, (L, L), 1)
    tril = ri >= ci
    tril_f = tril.astype(F32)
    triu_f = (ri <= ci).astype(F32)
    c_ref[...] = jnp.zeros_like(c_ref)
    n_ref[...] = jnp.zeros_like(n_ref)
    m_ref[...] = jnp.zeros_like(m_ref)
    hp = lax.Precision.HIGHEST

    def step(c, _):
        rows = pl.ds(pl.multiple_of(c * L, L), L)
        m = m_ref[...]
        qc = q_ref[rows, :] * (A_DK ** -0.5)
        kc = k_ref[rows, :]
        vc = v_ref[rows, :].astype(BF16)
        icol, fcol = igc_ref[rows, :], lfc_ref[rows, :]
        irow, frow = igr_ref[c], lfr_ref[c]
        bcol = jnp.dot(tril_f, jnp.broadcast_to(fcol, (L, LANES)), precision=hp,
                       preferred_element_type=F32)[:, :1]
        brow = jnp.dot(jnp.broadcast_to(frow, (8, L)), triu_f, precision=hp, preferred_element_type=F32)[:1, :]
        acol, arow = icol - bcol, irow - brow
        log_d = jnp.where(tril, bcol + arow, NEG)
        inter = bcol + m
        mt = jnp.maximum(inter, jnp.max(log_d, axis=-1, keepdims=True))
        dmat = jnp.exp(log_d - mt)
        wi = jnp.exp(inter - mt)
        qb = qc.astype(BF16)
        sc = lax.dot_general(qb, kc.astype(BF16), (((1,), (1,)), ((), ())), preferred_element_type=F32) * dmat
        cm = c_ref[...]
        nv = n_ref[...]
        num = (jnp.dot(sc.astype(BF16), vc, preferred_element_type=F32)
               + wi * jnp.dot(qb, cm.astype(BF16), preferred_element_type=F32))
        den = jnp.sum(sc, axis=-1, keepdims=True) + wi * jnp.sum(qc * nv, axis=-1, keepdims=True)
        hc = num / jnp.maximum(jnp.abs(den), jnp.exp(-mt))
        hn = hc * lax.rsqrt(jnp.mean(hc * hc, axis=-1, keepdims=True) + NORM_EPS) * hn_ref[...]
        o_ref[rows, :] = (hn * jax.nn.sigmoid(og_ref[rows, :])).astype(o_ref.dtype)
        blast = bcol[L - 1:L, :]
        m_new = jnp.maximum(blast + m, jnp.max(blast + arow, axis=-1, keepdims=True))
        w_c = jnp.exp(blast + m - m_new)
        kw = kc * jnp.exp(blast + acol - m_new)
        c_ref[...] = w_c * cm + lax.dot_general(kw.astype(BF16), vc, (((0,), (0,)), ((), ())),
                                                preferred_element_type=F32)
        n_ref[...] = w_c * nv + jnp.sum(kw, axis=0, keepdims=True)
        m_ref[...] = m_new
        return 0

    lax.fori_loop(0, seq // L, step, 0)


def _mlstm_prompt(y3, ig, lf, head_norm, *, q_col, k_col, v_col, o_col):
    nb, seq, _ = y3.shape
    L = A_CHUNK
    nc = seq // L
    igc = jnp.transpose(ig, (0, 2, 1))[..., None]
    lfc = jnp.transpose(lf, (0, 2, 1))[..., None]
    igr = igc.reshape(nb, A_HEADS, nc, 1, L)
    lfr = lfc.reshape(nb, A_HEADS, nc, 1, L)
    hn = head_norm.reshape(A_HEADS, 1, A_DV).astype(F32)
    col = lambda c0, w: (lambda b, h: (b, 0, c0 // w + h))
    gcol = pl.BlockSpec((None, None, seq, 1), lambda b, h: (b, h, 0, 0))
    grow = pl.BlockSpec((None, None, nc, 1, L), lambda b, h: (b, h, 0, 0, 0))
    return pl.pallas_call(
        functools.partial(_mlstm_kernel, seq=seq, chunk=L),
        grid=(nb, A_HEADS),
        in_specs=[pl.BlockSpec((None, seq, A_DK), col(q_col, A_DK)),
                  pl.BlockSpec((None, seq, A_DK), col(k_col, A_DK)),
                  pl.BlockSpec((None, seq, A_DV), col(v_col, A_DV)),
                  pl.BlockSpec((None, seq, A_DV), col(o_col, A_DV)),
                  gcol, gcol, grow, grow,
                  pl.BlockSpec((None, 1, A_DV), lambda b, h: (h, 0, 0))],
        out_specs=[pl.BlockSpec((None, seq, A_DV), lambda b, h: (b, 0, h)),
                   pl.BlockSpec((None, None, A_DK, A_DV), lambda b, h: (b, h, 0, 0)),
                   pl.BlockSpec((None, None, 1, A_DK), lambda b, h: (b, h, 0, 0)),
                   pl.BlockSpec((None, None, 1, 1), lambda b, h: (b, h, 0, 0))],
        out_shape=[jax.ShapeDtypeStruct((nb, seq, A_HEADS * A_DV), BF16),
                   jax.ShapeDtypeStruct((nb, A_HEADS, A_DK, A_DV), F32),
                   jax.ShapeDtypeStruct((nb, A_HEADS, 1, A_DK), F32),
                   jax.ShapeDtypeStruct((nb, A_HEADS, 1, 1), F32)],
        compiler_params=_cparams(("parallel", "arbitrary")),
        name="mlstm_prompt",
    )(y3, y3, y3, y3, igc, lfc, igr, lfr, hn)


def _rms(x, g):
    x32 = x.astype(F32)
    return x32 * lax.rsqrt(jnp.mean(x32 * x32, axis=-1, keepdims=True) + NORM_EPS) * g.astype(F32)


def _split_sizes(a, sizes):
    return jnp.split(a, np.cumsum(sizes)[:-1].tolist(), axis=-1)


def _rope_ref(x, pos):
    d = x.shape[-1]
    half = d // 2
    inv = jnp.exp(jnp.arange(half, dtype=F32) * (-2.0 * math.log(ROPE_THETA) / d))
    ang = pos.astype(F32)[:, None] * inv[None, :]
    cos = jnp.cos(ang)[None, :, None, :]
    sin = jnp.sin(ang)[None, :, None, :]
    x1, x2 = x[..., :half], x[..., half:]
    return jnp.concatenate([x1 * cos - x2 * sin, x2 * cos + x1 * sin], axis=-1)


def _softmax_sink(s, sink):
    m = jnp.max(s, axis=-1)
    if sink is not None:
        m = jnp.maximum(m, sink)
    e = jnp.exp(s - m[..., None])
    l = jnp.sum(e, axis=-1)
    if sink is not None:
        l = l + jnp.exp(sink - m)
    return e / l[..., None], m + jnp.log(l)


def _window_decode(q, k_ext, v_ext, q_off, window, dilation, sink):
    nb_, t_, h_, d_ = q.shape
    kvh = k_ext.shape[2]
    g_ = h_ // kvh
    dist = jnp.arange(window // dilation + 1) * dilation
    idx = q_off + jnp.arange(t_)[:, None] - dist[None, :]
    valid = idx >= 0
    idx = jnp.maximum(idx, 0)
    kg = k_ext[:, idx].astype(F32)
    vg = v_ext[:, idx].astype(F32)
    s = jnp.einsum('btkgd,btjkd->btkgj', q.reshape(nb_, t_, kvh, g_, d_).astype(F32), kg) * (d_ ** -0.5)
    s = jnp.where(valid[None, :, None, None, :], s, -jnp.inf)
    sk = None if sink is None else sink.astype(F32).reshape(kvh, g_)
    p, lse = _softmax_sink(s, sk)
    o = jnp.einsum('btkgj,btjkd->btkgd', p, vg)
    return o.reshape(nb_, t_, h_, d_), lse.reshape(nb_, t_, h_)


def _mlstm_step(q, k, v, ig, lf, c0, n0, m0):
    inter = lf + m0
    mt = jnp.maximum(inter, ig)
    dm = jnp.exp(ig - mt)
    wi = jnp.exp(inter - mt)
    sc = jnp.sum(q * k, axis=-1) * dm
    num = sc[..., None] * v + wi[..., None] * jnp.einsum('bhd,bhdv->bhv', q, c0)
    den = sc + wi * jnp.sum(q * n0, axis=-1)
    hc = num / jnp.maximum(jnp.abs(den), jnp.exp(-mt))[..., None]
    c1 = wi[..., None, None] * c0 + (k * dm[..., None])[..., :, None] * v[..., None, :]
    n1 = wi[..., None] * n0 + dm[..., None] * k
    return hc, (c1, n1, mt)


EVEN_MAIN = 2 * A_HEADS * A_DK + 2 * A_HEADS * A_DV + (B_HEADS + 2 * B_KV_HEADS) * B_HEAD_DIM
A_Q, A_K, A_V, A_O = 0, 1024, 2048, 4096
B_Q, B_K, B_V = 6144, 8192, 8448
C_Q, C_K, C_V = 0, 4096, 5120


def _even_weights(w_in, gate_b):
    aq, ak, av, ao, ai, af, bq, bk, bv = _split_sizes(
        w_in, (1024, 1024, 2048, 2048, A_HEADS, A_HEADS, 2048, 256, 256))
    main = jnp.concatenate([aq, ak, av, ao, bq, bk, bv], axis=-1).astype(BF16)
    gates = jnp.concatenate([ai, af, jnp.zeros((w_in.shape[0], LANES - 2 * A_HEADS), w_in.dtype)], axis=-1)
    return main, gates.astype(BF16)


def _gates(yg, gate_b):
    gb = gate_b.astype(F32)
    ig = yg[..., :A_HEADS] + gb[:A_HEADS]
    lf = jax.nn.log_sigmoid(yg[..., A_HEADS:2 * A_HEADS] + gb[A_HEADS:])
    return ig, lf


def _even_mixer_prompt(h2, nb, seq, w_main, w_gate, gate_b, head_norm, sinks, w_out_a, w_out_b):
    y = _matmul([(h2, w_main)], F32, 1024, 512, "even_in")
    yg = _matmul([(h2, w_gate)], F32, 1024, LANES, "even_gates")
    y3 = y.reshape(nb, seq, EVEN_MAIN)
    ig, lf = _gates(yg.reshape(nb, seq, LANES), gate_b)
    ha, c1, n1, m1 = _mlstm_prompt(y3, ig, lf, head_norm, q_col=A_Q, k_col=A_K, v_col=A_V, o_col=A_O)
    cos, sin = _rope_tables(jnp.arange(seq), B_HEAD_DIM)
    ob, kb = _band_attention(y3, cos, sin, sinks, patterns=((B_WINDOW, 1),), d=B_HEAD_DIM, n_q_heads=B_HEADS,
                             n_kv_heads=B_KV_HEADS, q_col=B_Q, k_col=B_K, v_col=B_V, name="swa_prompt")
    out = _matmul([(ha.reshape(nb * seq, -1), w_out_a), (ob.reshape(nb * seq, -1), w_out_b)],
                  F32, 1024, 1024, "even_out")
    kbuf = kb[:, -B_WINDOW:].reshape(nb, -1, B_KV_HEADS, B_HEAD_DIM)
    vbuf = y3[:, -B_WINDOW:, B_V:B_V + B_KV_HEADS * B_HEAD_DIM].reshape(nb, -1, B_KV_HEADS, B_HEAD_DIM)
    state = (c1, n1.reshape(nb, A_HEADS, A_DK), m1.reshape(nb, A_HEADS), kbuf, vbuf)
    return out, state


def _even_mixer_sample(h2, w_main, w_gate, gate_b, head_norm, sinks, w_out_a, w_out_b, state):
    nb = h2.shape[0]
    y = _matmul([(h2, w_main)], F32, nb, 512, "even_in_s")
    yg = _matmul([(h2, w_gate)], F32, nb, LANES, "even_gates_s")
    ig, lf = _gates(yg, gate_b)
    q = y[:, A_Q:A_Q + 1024].reshape(nb, A_HEADS, A_DK) * (A_DK ** -0.5)
    k = y[:, A_K:A_K + 1024].reshape(nb, A_HEADS, A_DK)
    v = y[:, A_V:A_V + 2048].reshape(nb, A_HEADS, A_DV)
    ao = y[:, A_O:A_O + 2048].reshape(nb, A_HEADS, A_DV)
    c0, n0, m0, k_old, v_old = state
    hc, (c1, n1, m1) = _mlstm_step(q, k, v, ig, lf, c0, n0, m0)
    ha = _rms(hc, head_norm.reshape(A_HEADS, A_DV)) * jax.nn.sigmoid(ao)
    pos = PAST_LEN + jnp.arange(1)
    qb = _rope_ref(y[:, B_Q:B_Q + 2048].reshape(nb, 1, B_HEADS, B_HEAD_DIM), pos)
    kb = _rope_ref(y[:, B_K:B_K + 256].reshape(nb, 1, B_KV_HEADS, B_HEAD_DIM), pos)
    vb = y[:, B_V:B_V + 256].reshape(nb, 1, B_KV_HEADS, B_HEAD_DIM)
    nbuf = k_old.shape[1]
    k_ext = jnp.concatenate([k_old, kb], axis=1)
    v_ext = jnp.concatenate([v_old, vb], axis=1)
    ob, _ = _window_decode(qb, k_ext, v_ext, nbuf, B_WINDOW, 1, sinks)
    out = _matmul([(ha.reshape(nb, -1).astype(BF16), w_out_a), (ob.reshape(nb, -1).astype(BF16), w_out_b)],
                  F32, nb, 1024, "even_out_s")
    return out, (c1, n1, m1, k_ext[:, -nbuf:], v_ext[:, -nbuf:])


def _odd_mixer_prompt(h2, nb, seq, w_in, w_out):
    y3 = _matmul([(h2, w_in)], F32, 1024, 1024, "odd_in").reshape(nb, seq, -1)
    cos, sin = _rope_tables(jnp.arange(seq), C_HEAD_DIM)
    o, kr = _band_attention(y3, cos, sin, None, patterns=C_PATTERNS[::-1], d=C_HEAD_DIM, n_q_heads=C_HEADS,
                            n_kv_heads=C_KV_HEADS, q_col=C_Q, k_col=C_K, v_col=C_V, name="dil_prompt")
    out = _matmul([(o.reshape(nb * seq, -1), w_out)], F32, 1024, 1024, "odd_out")
    kbuf = kr.reshape(nb, seq, C_KV_HEADS, C_HEAD_DIM)
    vbuf = y3[:, :, C_V:].reshape(nb, seq, C_KV_HEADS, C_HEAD_DIM)
    return out, (kbuf, vbuf)


def _odd_mixer_sample(h2, w_in, w_out, state):
    nb = h2.shape[0]
    y = _matmul([(h2, w_in)], F32, nb, 1024, "odd_in_s")
    pos = PAST_LEN + jnp.arange(1)
    q = _rope_ref(y[:, C_Q:C_K].reshape(nb, 1, C_HEADS, C_HEAD_DIM), pos)
    k = _rope_ref(y[:, C_K:C_V].reshape(nb, 1, C_KV_HEADS, C_HEAD_DIM), pos)
    v = y[:, C_V:].reshape(nb, 1, C_KV_HEADS, C_HEAD_DIM)
    nbuf = state[0].shape[1]
    k_ext = jnp.concatenate([state[0], k], axis=1)
    v_ext = jnp.concatenate([state[1], v], axis=1)
    outs = [_window_decode(q, k_ext, v_ext, nbuf, w, r, None) for (w, r) in C_PATTERNS]
    o = jnp.stack([a for a, _ in outs])
    lse = jnp.stack([b for _, b in outs])
    wts = jax.nn.softmax(lse, axis=0)
    o = jnp.sum(wts[..., None] * o, axis=0)
    out = _matmul([(o.reshape(nb, -1).astype(BF16), w_out)], F32, nb, 1024, "odd_out_s")
    return out, (k_ext[:, -nbuf:], v_ext[:, -nbuf:])


def _layer(x, mod, mix_fn, g1, g2, g3, g4, wg, wu, wd):
    nb, t, dm = x.shape
    sh1, sc1, gt1, sh2, sc2, gt2 = jnp.split(mod[:, None, :], 6, axis=-1)
    h = (_rms(x, g1) * (1.0 + sc1) + sh1).astype(BF16)
    y, state = mix_fn(h.reshape(nb * t, dm))
    x = x + gt1 * _rms(y.reshape(nb, t, dm), g2)
    h = (_rms(x, g3) * (1.0 + sc2) + sh2).astype(BF16).reshape(nb * t, dm)
    tm = min(nb * t, 512)
    f = _matmul([(_swiglu(h, wg, wu, 1024, 256), wd)], F32, tm, 512, "ffn_down")
    x = x + gt2 * _rms(f.reshape(nb, t, dm), g4)
    return x, state


def kernel(x_prompt, x_sample, state_l0_mlstm_c, state_l0_mlstm_n, state_l0_mlstm_m, cache_l0_swa_k, cache_l0_swa_v, cache_l1_dil_k, cache_l1_dil_v, state_l2_mlstm_c, state_l2_mlstm_n, state_l2_mlstm_m, cache_l2_swa_k, cache_l2_swa_v, cache_l3_dil_k, cache_l3_dil_v, c_prompt, c_sample, ada_w, ada_b, norm_mix_pre, norm_mix_post, norm_ffn_pre, norm_ffn_post, even_w_in, even_gate_b, even_head_norm, even_sinks, even_w_out, odd_w_in, odd_w_out, ffn_w_gate, ffn_w_up, ffn_w_down):
    past = {0: (state_l0_mlstm_c, state_l0_mlstm_n, state_l0_mlstm_m, cache_l0_swa_k, cache_l0_swa_v),
            1: (cache_l1_dil_k, cache_l1_dil_v),
            2: (state_l2_mlstm_c, state_l2_mlstm_n, state_l2_mlstm_m, cache_l2_swa_k, cache_l2_swa_v),
            3: (cache_l3_dil_k, cache_l3_dil_v)}
    nbp, seq, dm = x_prompt.shape
    nbs = x_sample.shape[0]
    c_all = jnp.concatenate([c_sample, c_prompt], axis=0)
    pad = (-c_all.shape[0]) % 16
    sc_all = jnp.pad(jax.nn.silu(c_all), ((0, pad), (0, 0))).astype(BF16)
    mod_all = _ada(sc_all, ada_w, ada_b)
    y_p, y_s = x_prompt, x_sample
    new_p, new_s = [], []
    for li in range(DEPTH):
        mod_s, mod_p = mod_all[li, :nbs], mod_all[li, nbs:nbs + nbp]
        lw = (norm_mix_pre[li], norm_mix_post[li], norm_ffn_pre[li], norm_ffn_post[li],
              ffn_w_gate[li].astype(BF16), ffn_w_up[li].astype(BF16), ffn_w_down[li].astype(BF16))
        j = li // 2
        if li % 2 == 0:
            w_main, w_gate = _even_weights(even_w_in[j], even_gate_b[j])
            wo = even_w_out[j].astype(BF16)
            wo_a, wo_b = wo[:A_HEADS * A_DV], wo[A_HEADS * A_DV:]
            mw = (w_main, w_gate, even_gate_b[j], even_head_norm[j], even_sinks[j], wo_a, wo_b)
            mix_p = lambda h, mw=mw: _even_mixer_prompt(h, nbp, seq, *mw)
            mix_s = lambda h, mw=mw, st=past[li]: _even_mixer_sample(h, *mw, st)
        else:
            mw = (odd_w_in[j].astype(BF16), odd_w_out[j].astype(BF16))
            mix_p = lambda h, mw=mw: _odd_mixer_prompt(h, nbp, seq, *mw)
            mix_s = lambda h, mw=mw, st=past[li]: _odd_mixer_sample(h, *mw, st)
        y_p, st_p = _layer(y_p, mod_p, mix_p, *lw)
        y_s, st_s = _layer(y_s, mod_s, mix_s, *lw)
        new_p.append(st_p)
        new_s.append(st_s)
    (p0c, p0n, p0m, p0k, p0v), (p1k, p1v), (p2c, p2n, p2m, p2k, p2v), (p3k, p3v) = new_p
    (s0c, s0n, s0m, s0k, s0v), (s1k, s1v), (s2c, s2n, s2m, s2k, s2v), (s3k, s3v) = new_s
    return (y_p, y_s,
            p0c, p0n, p0m, p0k, p0v, p1k, p1v, p2c, p2n, p2m, p2k, p2v, p3k, p3v,
            s0c, s0n, s0m, s0k, s0v, s1k, s1v, s2c, s2n, s2m, s2k, s2v, s3k, s3v)
```

```python
import functools
import math

import jax
import jax.numpy as jnp
import numpy as np
from jax import lax
from jax.experimental import pallas as pl
from jax.experimental.pallas import tpu as pltpu

F32 = jnp.float32
BF16 = jnp.bfloat16

D_MODEL = 4096
DEPTH = 4
PAST_LEN = 8192
A_HEADS, A_DK, A_DV, A_CHUNK = 4, 256, 512, 64
B_HEADS, B_KV_HEADS, B_HEAD_DIM, B_WINDOW = 32, 4, 64, 128
C_HEADS, C_KV_HEADS, C_HEAD_DIM = 32, 8, 128
C_PATTERNS = ((128, 1), (512, 4), (2048, 16))
BAND = 128
ROPE_THETA = 10000.0
NORM_EPS = 1e-6
LANES = 128
NEG = -1e30
VMEM_LIMIT = 56 * 1024 * 1024


def _cparams(sem):
    return pltpu.CompilerParams(dimension_semantics=sem, vmem_limit_bytes=VMEM_LIMIT)


def _mm_kernel(*refs, n_in):
    o_ref = refs[-1]
    acc = None
    for i in range(n_in):
        x = refs[2 * i][...].astype(BF16)
        w = refs[2 * i + 1][...].astype(BF16)
        d = jnp.dot(x, w, preferred_element_type=F32)
        acc = d if acc is None else acc + d
    o_ref[...] = acc.astype(o_ref.dtype)


def _w_spec(w, layer, kblk, k, tn):
    if w.ndim == 3:
        return pl.BlockSpec((None, k, tn), lambda i, j: (layer, kblk, j))
    return pl.BlockSpec((k, tn), lambda i, j: (kblk, j))


def _matmul(pairs, n, out_dtype, tm, tn, name):
    m = pairs[0][0].shape[0]
    tm = min(tm, m)
    tn = min(tn, n)
    assert m % tm == 0 and n % tn == 0, (m, n, tm, tn)
    in_specs, args = [], []
    for x, w, layer, kblk in pairs:
        k = x.shape[1]
        in_specs += [pl.BlockSpec((tm, k), lambda i, j: (i, 0)), _w_spec(w, layer, kblk, k, tn)]
        args += [x, w]
    return pl.pallas_call(
        functools.partial(_mm_kernel, n_in=len(pairs)),
        grid=(m // tm, n // tn),
        in_specs=in_specs,
        out_specs=pl.BlockSpec((tm, tn), lambda i, j: (i, j)),
        out_shape=jax.ShapeDtypeStruct((m, n), out_dtype),
        compiler_params=_cparams(("parallel", "arbitrary")),
        name=name,
    )(*args)


def _swiglu_kernel(x_ref, wg_ref, wu_ref, o_ref):
    x = x_ref[...]
    g = jnp.dot(x, wg_ref[...].astype(BF16), preferred_element_type=F32)
    u = jnp.dot(x, wu_ref[...].astype(BF16), preferred_element_type=F32)
    o_ref[...] = (g * jax.nn.sigmoid(g) * u).astype(o_ref.dtype)


def _swiglu(x, wg, wu, layer, tm, tn):
    m, k = x.shape
    n = wg.shape[2]
    tm = min(tm, m)
    assert m % tm == 0 and n % tn == 0
    return pl.pallas_call(
        _swiglu_kernel,
        grid=(m // tm, n // tn),
        in_specs=[pl.BlockSpec((tm, k), lambda i, j: (i, 0)),
                  _w_spec(wg, layer, 0, k, tn), _w_spec(wu, layer, 0, k, tn)],
        out_specs=pl.BlockSpec((tm, tn), lambda i, j: (i, j)),
        out_shape=jax.ShapeDtypeStruct((m, n), BF16),
        compiler_params=_cparams(("parallel", "arbitrary")),
        name="swiglu",
    )(x, wg, wu)


MOD_SH1, MOD_SC1, MOD_GT1, MOD_SH2, MOD_SC2, MOD_GT2 = range(6)


def _rms_rows(x, g):
    return x * lax.rsqrt(jnp.mean(x * x, axis=-1, keepdims=True) + NORM_EPS) * g


def _prenorm_kernel(x_ref, g_ref, sc_ref, sh_ref, h_ref):
    h_ref[...] = (_rms_rows(x_ref[...], g_ref[...]) * (1.0 + sc_ref[...]) + sh_ref[...]).astype(h_ref.dtype)


def _postnorm_kernel(*refs, has_next):
    x_ref, y_ref, gt_ref, gpost_ref = refs[:4]
    xn = x_ref[...] + gt_ref[...] * _rms_rows(y_ref[...], gpost_ref[...])
    if has_next:
        gpre_ref, sc_ref, sh_ref, xo_ref, h_ref = refs[4:]
        h_ref[...] = (_rms_rows(xn, gpre_ref[...]) * (1.0 + sc_ref[...]) + sh_ref[...]).astype(h_ref.dtype)
    else:
        xo_ref = refs[4]
    xo_ref[...] = xn


class _Rows:
    def __init__(self, mod_all, n_rows, rows_per_batch, mod_row0, tm):
        self.dm = mod_all.shape[-1] // 6
        self.tm = min(tm, n_rows)
        self.n_rows = n_rows
        self.per_token = rows_per_batch == 1
        if self.per_token:
            self.mod = mod_all
            assert mod_row0 % self.tm == 0 and n_rows == self.tm
        else:
            self.mod = mod_all.reshape(mod_all.shape[0], mod_all.shape[1], 1, mod_all.shape[2])
            assert rows_per_batch % self.tm == 0
        self.tiles_per_batch = max(rows_per_batch // self.tm, 1)
        self.mod_row0 = mod_row0

    def rows(self):
        return pl.BlockSpec((self.tm, self.dm), lambda i: (i, 0))

    def vec(self, layer, chunk):
        if self.per_token:
            return pl.BlockSpec((None, self.tm, self.dm), lambda i: (layer, self.mod_row0 // self.tm, chunk))
        return pl.BlockSpec((None, None, 1, self.dm),
                            lambda i: (layer, self.mod_row0 + i // self.tiles_per_batch, 0, chunk))


def _gain_spec(dm):
    return pl.BlockSpec((None, 1, dm), lambda i: (0, 0, 0))


def _prenorm(rows, x2, gain, layer, sc_chunk, sh_chunk):
    dm = rows.dm
    return pl.pallas_call(
        _prenorm_kernel,
        grid=(rows.n_rows // rows.tm,),
        in_specs=[rows.rows(), _gain_spec(dm), rows.vec(layer, sc_chunk), rows.vec(layer, sh_chunk)],
        out_specs=rows.rows(),
        out_shape=jax.ShapeDtypeStruct((rows.n_rows, dm), BF16),
        compiler_params=_cparams(("parallel",)),
        name="prenorm",
    )(x2, gain.astype(F32).reshape(1, 1, dm), rows.mod, rows.mod)


def _postnorm(rows, x2, y2, gain_post, layer, gt_chunk, nxt):
    dm = rows.dm
    in_specs = [rows.rows(), rows.rows(), rows.vec(layer, gt_chunk), _gain_spec(dm)]
    args = [x2, y2, rows.mod, gain_post.astype(F32).reshape(1, 1, dm)]
    out_specs = [rows.rows()]
    out_shape = [jax.ShapeDtypeStruct((rows.n_rows, dm), F32)]
    if nxt is not None:
        gain_pre, nl, sc_chunk, sh_chunk = nxt
        in_specs += [_gain_spec(dm), rows.vec(nl, sc_chunk), rows.vec(nl, sh_chunk)]
        args += [gain_pre.astype(F32).reshape(1, 1, dm), rows.mod, rows.mod]
        out_specs.append(rows.rows())
        out_shape.append(jax.ShapeDtypeStruct((rows.n_rows, dm), BF16))
    res = pl.pallas_call(
        functools.partial(_postnorm_kernel, has_next=nxt is not None),
        grid=(rows.n_rows // rows.tm,),
        in_specs=in_specs, out_specs=out_specs, out_shape=out_shape,
        compiler_params=_cparams(("parallel",)),
        name="postnorm",
    )(*args)
    return (res[0], res[1]) if nxt is not None else (res[0], None)


def _ada_kernel(c_ref, w_ref, b_ref, o_ref):
    o_ref[...] = jnp.dot(c_ref[...], w_ref[...].astype(BF16), preferred_element_type=F32) + b_ref[...]


def _ada(sc_all, ada_w, ada_b, tn=512):
    nl, k, n = ada_w.shape
    r = sc_all.shape[0]
    return pl.pallas_call(
        _ada_kernel,
        grid=(nl, n // tn),
        in_specs=[pl.BlockSpec((r, k), lambda l, j: (0, 0)),
                  pl.BlockSpec((None, k, tn), lambda l, j: (l, 0, j)),
                  pl.BlockSpec((None, 1, tn), lambda l, j: (l, 0, j))],
        out_specs=pl.BlockSpec((None, r, tn), lambda l, j: (l, 0, j)),
        out_shape=jax.ShapeDtypeStruct((nl, r, n), F32),
        compiler_params=_cparams(("parallel", "arbitrary")),
        name="ada",
    )(sc_all, ada_w, ada_b.astype(F32).reshape(nl, 1, n))


def _rope_tables(pos, d):
    half = d // 2
    inv = jnp.exp(jnp.arange(half, dtype=F32) * (-2.0 * math.log(ROPE_THETA) / d))
    ang = pos.astype(F32)[:, None] * inv[None, :]
    cos, sin = jnp.cos(ang), jnp.sin(ang)
    reps = LANES // d
    return (jnp.tile(jnp.concatenate([cos, cos], axis=-1), (1, reps)),
            jnp.tile(jnp.concatenate([-sin, sin], axis=-1), (1, reps)))


def _rope_lanes(x, cos, sin, d):
    if d == LANES:
        partner = pltpu.roll(x, LANES // 2, axis=1)
    else:
        lane = lax.broadcasted_iota(jnp.int32, x.shape, 1)
        half = d // 2
        partner = jnp.where((lane % d) < half, pltpu.roll(x, LANES - half, axis=1), pltpu.roll(x, half, axis=1))
    return x * cos + partner * sin


def _band_attn_kernel(*refs, patterns, d, kv_per_block, n_units, seq, scale, has_sink):
    if has_sink:
        sink_ref, refs = refs[0], refs[1:]
    q_refs, refs = refs[:n_units], refs[n_units:]
    k_ref, v_ref, cos_ref, sin_ref, o_ref, krot_ref = refs[:6]
    scratch = refs[6:]
    n_pat = len(patterns)
    if n_pat > 1:
        acc_scr, m_scr, l_scr = scratch
    heads_per_unit = LANES // d
    units_per_kv = n_units // kv_per_block
    n_rs = units_per_kv * heads_per_unit
    rows_q = n_rs * BAND

    chunk = min(256, seq)

    def rope_k(i, _):
        rows = pl.ds(pl.multiple_of(i * chunk, chunk), chunk)
        krot_ref[rows, :] = _rope_lanes(k_ref[rows, :], cos_ref[rows, :], sin_ref[rows, :], d)
        return 0

    lax.fori_loop(0, seq // chunk, rope_k, 0)

    lane_q = lax.broadcasted_iota(jnp.int32, (BAND, LANES), 1)
    jq = lax.broadcasted_iota(jnp.int32, (BAND, 2 * BAND), 0)
    jk = lax.broadcasted_iota(jnp.int32, (BAND, 2 * BAND), 1)
    valid_rest = (jk >= jq) & (jk <= jq + BAND)
    valid_first = (lax.broadcasted_iota(jnp.int32, (BAND, BAND), 1)
                   <= lax.broadcasted_iota(jnp.int32, (BAND, BAND), 0))

    def dup(a, j):
        if d == LANES:
            return a
        lane = lax.broadcasted_iota(jnp.int32, a.shape, 1)
        rolled = pltpu.roll(a, LANES // 2, axis=1)
        return jnp.where(lane < d, a, rolled) if j == 0 else jnp.where(lane < d, rolled, a)

    def block(p_idx, r, qs, ks, nk, valid, kvj):
        first, last = p_idx == 0, p_idx == n_pat - 1
        rows = pl.ds(qs, BAND, stride=r) if r > 1 else pl.ds(qs, BAND)
        krows = pl.ds(ks, nk, stride=r) if r > 1 else pl.ds(ks, nk)
        cosq, sinq = cos_ref[rows, :], sin_ref[rows, :]
        parts = []
        for u in range(units_per_kv):
            xr = _rope_lanes(q_refs[kvj * units_per_kv + u][rows, :], cosq, sinq, d)
            if heads_per_unit == 1:
                parts.append(xr)
            else:
                parts += [jnp.where(lane_q < d, xr, 0.0), jnp.where(lane_q >= d, xr, 0.0)]
        qst = jnp.concatenate(parts, axis=0).astype(BF16)
        kk = dup(krot_ref[krows, :], kvj).astype(BF16)
        vv = dup(v_ref[krows, :], kvj).astype(BF16)
        vaug = jnp.concatenate([vv, jnp.ones((nk, LANES), BF16)], axis=1)
        s = lax.dot_general(qst, kk, (((1,), (1,)), ((), ())), preferred_element_type=F32) * scale
        s = jnp.where(valid[None], s.reshape(n_rs, BAND, nk), NEG).reshape(rows_q, nk)
        m_blk = jnp.max(s, axis=-1, keepdims=True)
        if first:
            if has_sink:
                m_old = jnp.concatenate(
                    [jnp.full((BAND, 1), sink_ref[(pl.program_id(1) * kv_per_block + kvj) * n_rs + h], F32)
                     for h in range(n_rs)], axis=0)
                l_old = 1.0
            else:
                m_old, l_old = jnp.full((rows_q, 1), NEG, F32), 0.0
        else:
            m_old = jnp.concatenate([m_scr[h, rows, :] for h in range(n_rs)], axis=0)[:, :1]
        m_new = jnp.maximum(m_old, m_blk)
        p = jnp.exp(s - m_new)
        pv = jnp.dot(p.astype(BF16), vaug, preferred_element_type=F32)
        alpha = jnp.exp(m_old - m_new)
        if first:
            acc = pv[:, :LANES]
            l_new = pv[:, LANES:] + alpha * l_old
        else:
            acc = alpha * jnp.concatenate([acc_scr[h, rows, :] for h in range(n_rs)], axis=0) + pv[:, :LANES]
            l_new = alpha * jnp.concatenate([l_scr[h, rows, :] for h in range(n_rs)], axis=0) + pv[:, LANES:]
        if last:
            out = acc / l_new
            for u in range(units_per_kv):
                lo = (kvj * units_per_kv + u) * LANES
                if heads_per_unit == 1:
                    piece = out[u * BAND:(u + 1) * BAND]
                else:
                    piece = jnp.where(lane_q < d, out[2 * u * BAND:(2 * u + 1) * BAND],
                                      out[(2 * u + 1) * BAND:(2 * u + 2) * BAND])
                o_ref[rows, lo:lo + LANES] = piece.astype(o_ref.dtype)
        else:
            m_b = jnp.broadcast_to(m_new, (rows_q, LANES))
            for h in range(n_rs):
                acc_scr[h, rows, :] = acc[h * BAND:(h + 1) * BAND]
                l_scr[h, rows, :] = l_new[h * BAND:(h + 1) * BAND]
                m_scr[h, rows, :] = m_b[h * BAND:(h + 1) * BAND]

    for kvj in range(kv_per_block):
        for p_idx, (window, r) in enumerate(patterns):
            assert window // r == BAND
            n = seq // r
            nblk = n // BAND
            for c in range(r):
                block(p_idx, r, c, c, BAND, valid_first, kvj)
                if nblk > 1:
                    def body(i, _, p_idx=p_idx, r=r, c=c, kvj=kvj):
                        qs = c + r * BAND * i
                        block(p_idx, r, qs, qs - r * BAND, 2 * BAND, valid_rest, kvj)
                        return 0
                    lax.fori_loop(1, nblk, body, 0)


def _band_attention(y3, cos, sin, sinks, *, patterns, d, n_q_heads, n_kv_heads, q_col, k_col, v_col, name):
    nb, seq, _ = y3.shape
    kv_per_block = LANES // d
    n_kvb = n_kv_heads // kv_per_block
    qw = n_q_heads * d // n_kvb
    n_rs = n_q_heads // n_kv_heads
    n_units = qw // LANES
    kern = functools.partial(_band_attn_kernel, patterns=patterns, d=d, kv_per_block=kv_per_block,
                             n_units=n_units, seq=seq, scale=d ** -0.5, has_sink=sinks is not None)
    q_spec = lambda u: pl.BlockSpec((None, seq, LANES), lambda b, j, *_: (b, 0, q_col // LANES + j * n_units + u))
    in_specs = [q_spec(u) for u in range(n_units)] + [
                pl.BlockSpec((None, seq, LANES), lambda b, j, *_: (b, 0, k_col // LANES + j)),
                pl.BlockSpec((None, seq, LANES), lambda b, j, *_: (b, 0, v_col // LANES + j)),
                pl.BlockSpec((seq, LANES), lambda b, j, *_: (0, 0)),
                pl.BlockSpec((seq, LANES), lambda b, j, *_: (0, 0))]
    out_specs = [pl.BlockSpec((None, seq, qw), lambda b, j, *_: (b, 0, j)),
                 pl.BlockSpec((None, seq, LANES), lambda b, j, *_: (b, 0, j))]
    out_shape = [jax.ShapeDtypeStruct((nb, seq, n_q_heads * d), BF16),
                 jax.ShapeDtypeStruct((nb, seq, n_kv_heads * d), F32)]
    scratch = []
    if len(patterns) > 1:
        scratch = [pltpu.VMEM((n_rs, seq, LANES), F32) for _ in range(3)]
    assert q_col % qw == 0 and k_col % LANES == 0 and v_col % LANES == 0
    if sinks is None:
        gs = pl.GridSpec(grid=(nb, n_kvb), in_specs=in_specs, out_specs=out_specs, scratch_shapes=scratch)
        args = (y3,) * (n_units + 2) + (cos, sin)
    else:
        gs = pltpu.PrefetchScalarGridSpec(num_scalar_prefetch=1, grid=(nb, n_kvb), in_specs=in_specs,
                                          out_specs=out_specs, scratch_shapes=scratch)
        args = (sinks.astype(F32),) + (y3,) * (n_units + 2) + (cos, sin)
    return pl.pallas_call(kern, grid_spec=gs, out_shape=out_shape,
                          compiler_params=_cparams(("parallel", "arbitrary")), name=name)(*args)


def _mlstm_kernel(q_ref, k_ref, v_ref, og_ref, igc_ref, lfc_ref, igr_ref, lfr_ref, hn_ref,
                  o_ref, c_ref, n_ref, m_ref, *, seq, chunk):
    L = chunk
    ri = lax.broadcasted_iota(jnp.int32, (L, L), 0)
    ci = lax.broadcasted_iota(jnp.int32, (L, L), 1)
    tril = ri >= ci
    tril_f = tril.astype(F32)
    triu_f = (ri <= ci).astype(F32)
    c_ref[...] = jnp.zeros_like(c_ref)
    n_ref[...] = jnp.zeros_like(n_ref)
    m_ref[...] = jnp.zeros_like(m_ref)
    hp = lax.Precision.HIGHEST

    def step(c, _):
        rows = pl.ds(pl.multiple_of(c * L, L), L)
        m = m_ref[...]
        qc = q_ref[rows, :] * (A_DK ** -0.5)
        kc = k_ref[rows, :]
        vc = v_ref[rows, :].astype(BF16)
        icol, fcol = igc_ref[rows, :], lfc_ref[rows, :]
        irow, frow = igr_ref[c], lfr_ref[c]
        bcol = jnp.dot(tril_f, jnp.broadcast_to(fcol, (L, LANES)), precision=hp,
                       preferred_element_type=F32)[:, :1]
        brow = jnp.dot(jnp.broadcast_to(frow, (8, L)), triu_f, precision=hp, preferred_element_type=F32)[:1, :]
        acol, arow = icol - bcol, irow - brow
        log_d = jnp.where(tril, bcol + arow, NEG)
        inter = bcol + m
        mt = jnp.maximum(inter, jnp.max(log_d, axis=-1, keepdims=True))
        dmat = jnp.exp(log_d - mt)
        wi = jnp.exp(inter - mt)
        qb = qc.astype(BF16)
        sc = lax.dot_general(qb, kc.astype(BF16), (((1,), (1,)), ((), ())), preferred_element_type=F32) * dmat
        cm = c_ref[...]
        nv = n_ref[...]
        num = (jnp.dot(sc.astype(BF16), vc, preferred_element_type=F32)
               + wi * jnp.dot(qb, cm.astype(BF16), preferred_element_type=F32))
        den = jnp.sum(sc, axis=-1, keepdims=True) + wi * jnp.sum(qc * nv, axis=-1, keepdims=True)
        hc = num / jnp.maximum(jnp.abs(den), jnp.exp(-mt))
        hn = hc * lax.rsqrt(jnp.mean(hc * hc, axis=-1, keepdims=True) + NORM_EPS) * hn_ref[...]
        o_ref[rows, :] = (hn * jax.nn.sigmoid(og_ref[rows, :])).astype(o_ref.dtype)
        blast = bcol[L - 1:L, :]
        m_new = jnp.maximum(blast + m, jnp.max(blast + arow, axis=-1, keepdims=True))
        w_c = jnp.exp(blast + m - m_new)
        kw = kc * jnp.exp(blast + acol - m_new)
        c_ref[...] = w_c * cm + lax.dot_general(kw.astype(BF16), vc, (((0,), (0,)), ((), ())),
                                                preferred_element_type=F32)
        n_ref[...] = w_c * nv + jnp.sum(kw, axis=0, keepdims=True)
        m_ref[...] = m_new
        return 0

    lax.fori_loop(0, seq // L, step, 0)


def _mlstm_prompt(y3, ig, lf, head_norm, *, q_col, k_col, v_col, o_col):
    nb, seq, _ = y3.shape
    L = A_CHUNK
    nc = seq // L
    igc = jnp.transpose(ig, (0, 2, 1))[..., None]
    lfc = jnp.transpose(lf, (0, 2, 1))[..., None]
    igr = igc.reshape(nb, A_HEADS, nc, 1, L)
    lfr = lfc.reshape(nb, A_HEADS, nc, 1, L)
    hn = head_norm.reshape(A_HEADS, 1, A_DV).astype(F32)
    col = lambda c0, w: (lambda b, h: (b, 0, c0 // w + h))
    gcol = pl.BlockSpec((None, None, seq, 1), lambda b, h: (b, h, 0, 0))
    grow = pl.BlockSpec((None, None, nc, 1, L), lambda b, h: (b, h, 0, 0, 0))
    return pl.pallas_call(
        functools.partial(_mlstm_kernel, seq=seq, chunk=L),
        grid=(nb, A_HEADS),
        in_specs=[pl.BlockSpec((None, seq, A_DK), col(q_col, A_DK)),
                  pl.BlockSpec((None, seq, A_DK), col(k_col, A_DK)),
                  pl.BlockSpec((None, seq, A_DV), col(v_col, A_DV)),
                  pl.BlockSpec((None, seq, A_DV), col(o_col, A_DV)),
                  gcol, gcol, grow, grow,
                  pl.BlockSpec((None, 1, A_DV), lambda b, h: (h, 0, 0))],
        out_specs=[pl.BlockSpec((None, seq, A_DV), lambda b, h: (b, 0, h)),
                   pl.BlockSpec((None, None, A_DK, A_DV), lambda b, h: (b, h, 0, 0)),
                   pl.BlockSpec((None, None, 1, A_DK), lambda b, h: (b, h, 0, 0)),
                   pl.BlockSpec((None, None, 1, 1), lambda b, h: (b, h, 0, 0))],
        out_shape=[jax.ShapeDtypeStruct((nb, seq, A_HEADS * A_DV), BF16),
                   jax.ShapeDtypeStruct((nb, A_HEADS, A_DK, A_DV), F32),
                   jax.ShapeDtypeStruct((nb, A_HEADS, 1, A_DK), F32),
                   jax.ShapeDtypeStruct((nb, A_HEADS, 1, 1), F32)],
        compiler_params=_cparams(("parallel", "arbitrary")),
        name="mlstm_prompt",
    )(y3, y3, y3, y3, igc, lfc, igr, lfr, hn)


def _mlstm_step_kernel(q_ref, k_ref, v_ref, og_ref, ig_ref, lf_ref, c0_ref, n0_ref, m0_ref, hn_ref,
                       o_ref, c1_ref, n1_ref, m1_ref):
    for h in range(A_HEADS):
        q = q_ref[h] * (A_DK ** -0.5)
        k = k_ref[h]
        v = v_ref[h]
        c0 = c0_ref[h]
        n0 = n0_ref[h]
        ig, lf, m0 = ig_ref[h], lf_ref[h], m0_ref[h]
        inter = lf + m0
        mt = jnp.maximum(inter, ig)
        dm = jnp.exp(ig - mt)
        wi = jnp.exp(inter - mt)
        sc = jnp.sum(q * k, axis=0, keepdims=True) * dm
        qc = jnp.sum(c0 * q, axis=0, keepdims=True)
        num = sc * v + wi * qc
        den = sc + wi * jnp.sum(q * n0, axis=0, keepdims=True)
        hc = num / jnp.maximum(jnp.abs(den), jnp.exp(-mt))
        hn = hc * lax.rsqrt(jnp.mean(hc * hc, axis=-1, keepdims=True) + NORM_EPS) * hn_ref[h]
        o_ref[h] = (hn * jax.nn.sigmoid(og_ref[h])).astype(o_ref.dtype)
        c1_ref[h] = wi * c0 + (k * dm) * v
        n1_ref[h] = wi * n0 + dm * k
        m1_ref[h] = mt


def _mlstm_sample(y, ig, lf, head_norm, c0, n0, m0):
    nb = y.shape[0]
    col = lambda a: a.reshape(nb, A_HEADS, A_DK, 1)
    row = lambda a: a.reshape(nb, A_HEADS, 1, A_DV)
    sca = lambda a: a.astype(F32).reshape(nb, A_HEADS, 1, 1)
    cspec = pl.BlockSpec((None, A_HEADS, A_DK, 1), lambda b: (b, 0, 0, 0))
    rspec = pl.BlockSpec((None, A_HEADS, 1, A_DV), lambda b: (b, 0, 0, 0))
    sspec = pl.BlockSpec((None, A_HEADS, 1, 1), lambda b: (b, 0, 0, 0))
    mspec = pl.BlockSpec((None, A_HEADS, A_DK, A_DV), lambda b: (b, 0, 0, 0))
    o, c1, n1, m1 = pl.pallas_call(
        _mlstm_step_kernel,
        grid=(nb,),
        in_specs=[cspec, cspec, rspec, rspec, sspec, sspec, mspec, cspec, sspec,
                  pl.BlockSpec((A_HEADS, 1, A_DV), lambda b: (0, 0, 0))],
        out_specs=[rspec, mspec, cspec, sspec],
        out_shape=[jax.ShapeDtypeStruct((nb, A_HEADS, 1, A_DV), F32),
                   jax.ShapeDtypeStruct((nb, A_HEADS, A_DK, A_DV), F32),
                   jax.ShapeDtypeStruct((nb, A_HEADS, A_DK, 1), F32),
                   jax.ShapeDtypeStruct((nb, A_HEADS, 1, 1), F32)],
        compiler_params=_cparams(("parallel",)),
        name="mlstm_step",
    )(col(y[:, A_Q:A_Q + A_HEADS * A_DK]), col(y[:, A_K:A_K + A_HEADS * A_DK]),
      row(y[:, A_V:A_V + A_HEADS * A_DV]), row(y[:, A_O:A_O + A_HEADS * A_DV]),
      sca(ig), sca(lf), c0.astype(F32), col(n0.astype(F32)), sca(m0),
      head_norm.astype(F32).reshape(A_HEADS, 1, A_DV))
    return (o.reshape(nb, A_HEADS * A_DV), c1, n1.reshape(nb, A_HEADS, A_DK), m1.reshape(nb, A_HEADS))


def _decode_attn_kernel(*refs, n_pat, d, group, scale, has_sink):
    q_ref, kn_ref, vn_ref, cos_ref, sin_ref = refs[:5]
    refs = refs[5:]
    if has_sink:
        sink_ref, refs = refs[0], refs[1:]
    k_refs, v_refs = refs[:n_pat], refs[n_pat:2 * n_pat]
    o_ref, knew_ref = refs[2 * n_pat:]
    n_seg = LANES // d
    cos, sin = cos_ref[...], sin_ref[...]
    kn = _rope_lanes(kn_ref[...], cos, sin, d)
    knew_ref[...] = kn
    vn = vn_ref[...]
    lane = lax.broadcasted_iota(jnp.int32, kn.shape, 1)

    def seg_sum(p):
        if n_seg == 1:
            return jnp.sum(p, axis=-1, keepdims=True)
        lane_p = lax.broadcasted_iota(jnp.int32, p.shape, p.ndim - 1)
        lo = jnp.sum(jnp.where(lane_p < d, p, 0.0), axis=-1, keepdims=True)
        hi = jnp.sum(jnp.where(lane_p >= d, p, 0.0), axis=-1, keepdims=True)
        return jnp.where(lane_p < d, lo, hi)

    for g in range(group):
        qg = _rope_lanes(q_ref[g], cos, sin, d)
        s_new = seg_sum(kn * qg) * scale
        s = [seg_sum(k_refs[p][...] * qg[None]) * scale for p in range(n_pat)]
        m = s_new
        for p in range(n_pat):
            m = jnp.maximum(m, jnp.max(s[p], axis=0))
        if has_sink:
            m = jnp.maximum(m, sink_ref[g])
        e_new = jnp.exp(s_new - m) * float(n_pat)
        l = e_new
        acc = e_new * vn
        for p in range(n_pat):
            e = jnp.exp(s[p] - m[None])
            l = l + jnp.sum(e, axis=0)
            acc = acc + jnp.sum(e * v_refs[p][...], axis=0)
        if has_sink:
            l = l + jnp.exp(sink_ref[g] - m)
        o_ref[g] = (acc / l).astype(o_ref.dtype)


def _decode_attention(q, k_new, v_new, k_cache, v_cache, sinks, pos, *, patterns, d, name):
    nb, n_heads, _ = q.shape
    n_kv, cache_len = k_cache.shape[2], k_cache.shape[1]
    group = n_heads // n_kv
    n_seg = LANES // d
    r_rows = n_kv // n_seg
    to_tiles = lambda a, inner: a.reshape(nb, r_rows, n_seg, inner, d).transpose(0, 3, 1, 2, 4).reshape(
        nb, inner, r_rows, LANES)
    qt = to_tiles(q, group)
    knt = k_new.reshape(nb, r_rows, LANES)
    vnt = v_new.reshape(nb, r_rows, LANES)
    cos, sin = _rope_tables(pos, d)
    kc = k_cache.reshape(nb, cache_len, r_rows, LANES)
    vc = v_cache.reshape(nb, cache_len, r_rows, LANES)
    tile = pl.BlockSpec((None, r_rows, LANES), lambda b: (b, 0, 0))
    one = pl.BlockSpec((1, LANES), lambda b: (0, 0))
    in_specs = [pl.BlockSpec((None, group, r_rows, LANES), lambda b: (b, 0, 0, 0)), tile, tile, one, one]
    args = [qt, knt, vnt, cos, sin]
    if sinks is not None:
        sk = jnp.broadcast_to(sinks.astype(F32).reshape(r_rows, n_seg, group, 1), (r_rows, n_seg, group, d))
        args.append(sk.transpose(2, 0, 1, 3).reshape(group, r_rows, LANES))
        in_specs.append(pl.BlockSpec((group, r_rows, LANES), lambda b: (0, 0, 0)))
    views = []
    for window, r in patterns:
        assert window // r == BAND and cache_len % (BAND * r) == 0
        blk = cache_len // r // BAND - 1
        views.append(pl.BlockSpec((None, BAND, None, r_rows, LANES), lambda b, blk=blk: (b, blk, 0, 0, 0)))
    in_specs += views + views
    args += [kc.reshape(nb, cache_len // r, r, r_rows, LANES) for _, r in patterns]
    args += [vc.reshape(nb, cache_len // r, r, r_rows, LANES) for _, r in patterns]
    o, kn = pl.pallas_call(
        functools.partial(_decode_attn_kernel, n_pat=len(patterns), d=d, group=group, scale=d ** -0.5,
                          has_sink=sinks is not None),
        grid=(nb,),
        in_specs=in_specs,
        out_specs=[pl.BlockSpec((None, group, r_rows, LANES), lambda b: (b, 0, 0, 0)), tile],
        out_shape=[jax.ShapeDtypeStruct((nb, group, r_rows, LANES), F32),
                   jax.ShapeDtypeStruct((nb, r_rows, LANES), F32)],
        compiler_params=_cparams(("parallel",)),
        name=name,
    )(*args)
    o = o.reshape(nb, group, r_rows, n_seg, d).transpose(0, 2, 3, 1, 4).reshape(nb, n_heads * d)
    return o, kn.reshape(nb, n_kv, d)


def _shift_kernel(c_ref, new_ref, o_ref, *, length, chunk):
    n_full = (length - 1) // chunk

    def body(i, _):
        o_ref[pl.ds(i * chunk, chunk)] = c_ref[pl.ds(i * chunk + 1, chunk)]
        return 0

    lax.fori_loop(0, n_full, body, 0)
    done = n_full * chunk
    if done < length - 1:
        o_ref[done:length - 1] = c_ref[done + 1:length]
    o_ref[length - 1] = new_ref[...]


def _shift_cache(cache, new, name):
    nb, length, n_kv, d = cache.shape
    return pl.pallas_call(
        functools.partial(_shift_kernel, length=length, chunk=min(64, length - 1)),
        grid=(nb,),
        in_specs=[pl.BlockSpec((None, length, n_kv, d), lambda b: (b, 0, 0, 0)),
                  pl.BlockSpec((None, n_kv, d), lambda b: (b, 0, 0))],
        out_specs=pl.BlockSpec((None, length, n_kv, d), lambda b: (b, 0, 0, 0)),
        out_shape=jax.ShapeDtypeStruct(cache.shape, cache.dtype),
        compiler_params=_cparams(("parallel",)),
        name=name,
    )(cache, new.astype(cache.dtype))


N_IN_EVEN = 2 * A_HEADS * A_DK + 2 * A_HEADS * A_DV + 2 * A_HEADS + (B_HEADS + 2 * B_KV_HEADS) * B_HEAD_DIM
EVEN_IN = N_IN_EVEN // 512 * 512
A_Q, A_K, A_V, A_O = 0, 1024, 2048, 4096
GATE_COL = 6144
B_COL = GATE_COL + 2 * A_HEADS
B_Q, B_K, B_V = 0, 2048, 2304
C_Q, C_K, C_V = 0, 4096, 5120
SWA_PATTERN = ((B_WINDOW, 1),)


def _even_projection(h2, w_in, j, tm):
    y = _matmul([(h2, w_in, j, 0)], EVEN_IN, F32, tm, 512, "even_in")
    n_tail = N_IN_EVEN - EVEN_IN
    w_tail = jnp.pad(w_in[j, :, EVEN_IN:], ((0, 0), (0, LANES - n_tail)))
    y_tail = _matmul([(h2, w_tail, None, 0)], LANES, F32, tm, LANES, "even_in_tail")
    yb = jnp.concatenate([y[:, B_COL:], y_tail[:, :n_tail]], axis=-1)
    return y, y[:, GATE_COL:B_COL], yb


def _gates(yg, gate_b):
    gb = gate_b.astype(F32)
    ig = yg[..., :A_HEADS] + gb[:A_HEADS]
    lf = jax.nn.log_sigmoid(yg[..., A_HEADS:] + gb[A_HEADS:])
    return ig, lf


def _even_mixer_prompt(h2, nb, seq, j, w_in, gate_b, head_norm, sinks, w_out):
    y, yg, yb = _even_projection(h2, w_in, j, 1024)
    y3, yb3 = y.reshape(nb, seq, -1), yb.reshape(nb, seq, -1)
    ig, lf = _gates(yg.reshape(nb, seq, -1), gate_b)
    ha, c1, n1, m1 = _mlstm_prompt(y3, ig, lf, head_norm, q_col=A_Q, k_col=A_K, v_col=A_V, o_col=A_O)
    cos, sin = _rope_tables(jnp.arange(seq), B_HEAD_DIM)
    ob, kb = _band_attention(yb3, cos, sin, sinks, patterns=SWA_PATTERN, d=B_HEAD_DIM, n_q_heads=B_HEADS,
                             n_kv_heads=B_KV_HEADS, q_col=B_Q, k_col=B_K, v_col=B_V, name="swa_prompt")
    out = _matmul([(ha.reshape(nb * seq, -1), w_out, j, 0), (ob.reshape(nb * seq, -1), w_out, j, 1)],
                  D_MODEL, F32, 1024, 512, "even_out")
    kbuf = kb[:, -B_WINDOW:].reshape(nb, -1, B_KV_HEADS, B_HEAD_DIM)
    vbuf = yb3[:, -B_WINDOW:, B_V:].reshape(nb, -1, B_KV_HEADS, B_HEAD_DIM)
    return out, (c1, n1.reshape(nb, A_HEADS, A_DK), m1.reshape(nb, A_HEADS), kbuf, vbuf)


def _even_mixer_sample(h2, j, w_in, gate_b, head_norm, sinks, w_out, state):
    nb = h2.shape[0]
    y, yg, yb = _even_projection(h2, w_in, j, nb)
    ig, lf = _gates(yg, gate_b)
    c0, n0, m0, k_old, v_old = state
    ha, c1, n1, m1 = _mlstm_sample(y, ig, lf, head_norm, c0, n0, m0)
    q = yb[:, B_Q:B_K].reshape(nb, B_HEADS, B_HEAD_DIM)
    k_new = yb[:, B_K:B_V].reshape(nb, B_KV_HEADS, B_HEAD_DIM)
    v_new = yb[:, B_V:].reshape(nb, B_KV_HEADS, B_HEAD_DIM)
    ob, k_rot = _decode_attention(q, k_new, v_new, k_old, v_old, sinks, PAST_LEN + jnp.arange(1),
                                  patterns=SWA_PATTERN, d=B_HEAD_DIM, name="swa_step")
    out = _matmul([(ha.astype(BF16), w_out, j, 0), (ob.astype(BF16), w_out, j, 1)],
                  D_MODEL, F32, nb, 512, "even_out_s")
    return out, (c1, n1, m1, _shift_cache(k_old, k_rot, "swa_k_shift"), _shift_cache(v_old, v_new, "swa_v_shift"))


def _odd_mixer_prompt(h2, nb, seq, j, w_in, w_out):
    y3 = _matmul([(h2, w_in, j, 0)], w_in.shape[2], F32, 1024, 512, "odd_in").reshape(nb, seq, -1)
    cos, sin = _rope_tables(jnp.arange(seq), C_HEAD_DIM)
    o, kr = _band_attention(y3, cos, sin, None, patterns=C_PATTERNS[::-1], d=C_HEAD_DIM, n_q_heads=C_HEADS,
                            n_kv_heads=C_KV_HEADS, q_col=C_Q, k_col=C_K, v_col=C_V, name="dil_prompt")
    out = _matmul([(o.reshape(nb * seq, -1), w_out, j, 0)], D_MODEL, F32, 1024, 512, "odd_out")
    kbuf = kr.reshape(nb, seq, C_KV_HEADS, C_HEAD_DIM)
    vbuf = y3[:, :, C_V:].reshape(nb, seq, C_KV_HEADS, C_HEAD_DIM)
    return out, (kbuf, vbuf)


def _odd_mixer_sample(h2, j, w_in, w_out, state):
    nb = h2.shape[0]
    y = _matmul([(h2, w_in, j, 0)], w_in.shape[2], F32, nb, 512, "odd_in_s")
    q = y[:, C_Q:C_K].reshape(nb, C_HEADS, C_HEAD_DIM)
    k_new = y[:, C_K:C_V].reshape(nb, C_KV_HEADS, C_HEAD_DIM)
    v_new = y[:, C_V:].reshape(nb, C_KV_HEADS, C_HEAD_DIM)
    k_old, v_old = state
    o, k_rot = _decode_attention(q, k_new, v_new, k_old, v_old, None, PAST_LEN + jnp.arange(1),
                                 patterns=C_PATTERNS, d=C_HEAD_DIM, name="dil_step")
    out = _matmul([(o.astype(BF16), w_out, j, 0)], D_MODEL, F32, nb, 512, "odd_out_s")
    return out, (_shift_cache(k_old, k_rot, "dil_k_shift"), _shift_cache(v_old, v_new, "dil_v_shift"))


def _trunk(x2, rows, mixers, norms, ffn):
    g_mix_pre, g_mix_post, g_ffn_pre, g_ffn_post = norms
    wg, wu, wd = ffn
    h = _prenorm(rows, x2, g_mix_pre[0], 0, MOD_SC1, MOD_SH1)
    states = []
    for li in range(DEPTH):
        y, state = mixers[li](h)
        states.append(state)
        x2, h = _postnorm(rows, x2, y, g_mix_post[li], li, MOD_GT1, (g_ffn_pre[li], li, MOD_SC2, MOD_SH2))
        a = _swiglu(h, wg, wu, li, 1024, 256)
        f = _matmul([(a, wd, li, 0)], D_MODEL, F32, 512, 512, "ffn_down")
        nxt = (g_mix_pre[li + 1], li + 1, MOD_SC1, MOD_SH1) if li + 1 < DEPTH else None
        x2, h = _postnorm(rows, x2, f, g_ffn_post[li], li, MOD_GT2, nxt)
    return x2, states


def kernel(x_prompt, x_sample, state_l0_mlstm_c, state_l0_mlstm_n, state_l0_mlstm_m, cache_l0_swa_k, cache_l0_swa_v, cache_l1_dil_k, cache_l1_dil_v, state_l2_mlstm_c, state_l2_mlstm_n, state_l2_mlstm_m, cache_l2_swa_k, cache_l2_swa_v, cache_l3_dil_k, cache_l3_dil_v, c_prompt, c_sample, ada_w, ada_b, norm_mix_pre, norm_mix_post, norm_ffn_pre, norm_ffn_post, even_w_in, even_gate_b, even_head_norm, even_sinks, even_w_out, odd_w_in, odd_w_out, ffn_w_gate, ffn_w_up, ffn_w_down):
    past = {0: (state_l0_mlstm_c, state_l0_mlstm_n, state_l0_mlstm_m, cache_l0_swa_k, cache_l0_swa_v),
            1: (cache_l1_dil_k, cache_l1_dil_v),
            2: (state_l2_mlstm_c, state_l2_mlstm_n, state_l2_mlstm_m, cache_l2_swa_k, cache_l2_swa_v),
            3: (cache_l3_dil_k, cache_l3_dil_v)}
    nbp, seq, dm = x_prompt.shape
    nbs = x_sample.shape[0]
    c_all = jnp.concatenate([c_sample, c_prompt], axis=0)
    pad = (-c_all.shape[0]) % 16
    sc_all = jnp.pad(jax.nn.silu(c_all), ((0, pad), (0, 0))).astype(BF16)
    mod_all = _ada(sc_all, ada_w, ada_b)
    mix_p, mix_s = [], []
    for li in range(DEPTH):
        j = li // 2
        if li % 2 == 0:
            mw = (j, even_w_in, even_gate_b[j], even_head_norm[j], even_sinks[j], even_w_out)
            mix_p.append(lambda h, mw=mw: _even_mixer_prompt(h, nbp, seq, *mw))
            mix_s.append(lambda h, mw=mw, st=past[li]: _even_mixer_sample(h, *mw, st))
        else:
            mw = (j, odd_w_in, odd_w_out)
            mix_p.append(lambda h, mw=mw: _odd_mixer_prompt(h, nbp, seq, *mw))
            mix_s.append(lambda h, mw=mw, st=past[li]: _odd_mixer_sample(h, *mw, st))
    norms = (norm_mix_pre, norm_mix_post, norm_ffn_pre, norm_ffn_post)
    ffn = (ffn_w_gate, ffn_w_up, ffn_w_down.astype(BF16))
    rows_p = _Rows(mod_all, nbp * seq, seq, nbs, 256)
    rows_s = _Rows(mod_all, nbs, 1, 0, nbs)
    y_p, new_p = _trunk(x_prompt.reshape(nbp * seq, dm), rows_p, mix_p, norms, ffn)
    y_s, new_s = _trunk(x_sample.reshape(nbs, dm), rows_s, mix_s, norms, ffn)
    y_p, y_s = y_p.reshape(x_prompt.shape), y_s.reshape(x_sample.shape)
    (p0c, p0n, p0m, p0k, p0v), (p1k, p1v), (p2c, p2n, p2m, p2k, p2v), (p3k, p3v) = new_p
    (s0c, s0n, s0m, s0k, s0v), (s1k, s1v), (s2c, s2n, s2m, s2k, s2v), (s3k, s3v) = new_s
    return (y_p, y_s,
            p0c, p0n, p0m, p0k, p0v, p1k, p1v, p2c, p2n, p2m, p2k, p2v, p3k, p3v,
            s0c, s0n, s0m, s0k, s0v, s1k, s1v, s2c, s2n, s2m, s2k, s2v, s3k, s3v)
```

```python
import functools
import math

import jax
import jax.numpy as jnp
import numpy as np
from jax import lax
from jax.experimental import pallas as pl
from jax.experimental.pallas import tpu as pltpu

F32 = jnp.float32
BF16 = jnp.bfloat16

D_MODEL = 4096
DEPTH = 4
PAST_LEN = 8192
A_HEADS, A_DK, A_DV, A_CHUNK = 4, 256, 512, 64
B_HEADS, B_KV_HEADS, B_HEAD_DIM, B_WINDOW = 32, 4, 64, 128
C_HEADS, C_KV_HEADS, C_HEAD_DIM = 32, 8, 128
C_PATTERNS = ((128, 1), (512, 4), (2048, 16))
BAND = 128
ROPE_THETA = 10000.0
NORM_EPS = 1e-6
LANES = 128
NEG = -1e30
VMEM_LIMIT = 56 * 1024 * 1024


def _cparams(sem):
    return pltpu.CompilerParams(dimension_semantics=sem, vmem_limit_bytes=VMEM_LIMIT)


def _mm_kernel(*refs, n_in):
    o_ref = refs[-1]
    acc = None
    for i in range(n_in):
        x = refs[2 * i][...].astype(BF16)
        w = refs[2 * i + 1][...].astype(BF16)
        d = jnp.dot(x, w, preferred_element_type=F32)
        acc = d if acc is None else acc + d
    o_ref[...] = acc.astype(o_ref.dtype)


def _w_spec(w, layer, kblk, k, tn):
    if w.ndim == 3:
        return pl.BlockSpec((None, k, tn), lambda i, j: (layer, kblk, j))
    return pl.BlockSpec((k, tn), lambda i, j: (kblk, j))


def _matmul(pairs, n, out_dtype, tm, tn, name):
    m = pairs[0][0].shape[0]
    tm = min(tm, m)
    tn = min(tn, n)
    assert m % tm == 0 and n % tn == 0, (m, n, tm, tn)
    in_specs, args = [], []
    for x, w, layer, kblk in pairs:
        k = x.shape[1]
        in_specs += [pl.BlockSpec((tm, k), lambda i, j: (i, 0)), _w_spec(w, layer, kblk, k, tn)]
        args += [x, w]
    return pl.pallas_call(
        functools.partial(_mm_kernel, n_in=len(pairs)),
        grid=(m // tm, n // tn),
        in_specs=in_specs,
        out_specs=pl.BlockSpec((tm, tn), lambda i, j: (i, j)),
        out_shape=jax.ShapeDtypeStruct((m, n), out_dtype),
        compiler_params=_cparams(("parallel", "arbitrary")),
        name=name,
    )(*args)


def _swiglu_kernel(x_ref, wg_ref, wu_ref, o_ref):
    x = x_ref[...]
    g = jnp.dot(x, wg_ref[...].astype(BF16), preferred_element_type=F32)
    u = jnp.dot(x, wu_ref[...].astype(BF16), preferred_element_type=F32)
    o_ref[...] = (g * jax.nn.sigmoid(g) * u).astype(o_ref.dtype)


def _swiglu(x, wg, wu, layer, tm, tn):
    m, k = x.shape
    n = wg.shape[2]
    tm = min(tm, m)
    assert m % tm == 0 and n % tn == 0
    return pl.pallas_call(
        _swiglu_kernel,
        grid=(m // tm, n // tn),
        in_specs=[pl.BlockSpec((tm, k), lambda i, j: (i, 0)),
                  _w_spec(wg, layer, 0, k, tn), _w_spec(wu, layer, 0, k, tn)],
        out_specs=pl.BlockSpec((tm, tn), lambda i, j: (i, j)),
        out_shape=jax.ShapeDtypeStruct((m, n), BF16),
        compiler_params=_cparams(("parallel", "arbitrary")),
        name="swiglu",
    )(x, wg, wu)


MOD_SH1, MOD_SC1, MOD_GT1, MOD_SH2, MOD_SC2, MOD_GT2 = range(6)


def _rms_rows(x, g):
    return x * lax.rsqrt(jnp.mean(x * x, axis=-1, keepdims=True) + NORM_EPS) * g


def _prenorm_kernel(x_ref, g_ref, sc_ref, sh_ref, h_ref):
    h_ref[...] = (_rms_rows(x_ref[...], g_ref[...]) * (1.0 + sc_ref[...]) + sh_ref[...]).astype(h_ref.dtype)


def _postnorm_kernel(*refs, has_next):
    x_ref, y_ref, gt_ref, gpost_ref = refs[:4]
    xn = x_ref[...] + gt_ref[...] * _rms_rows(y_ref[...], gpost_ref[...])
    if has_next:
        gpre_ref, sc_ref, sh_ref, xo_ref, h_ref = refs[4:]
        h_ref[...] = (_rms_rows(xn, gpre_ref[...]) * (1.0 + sc_ref[...]) + sh_ref[...]).astype(h_ref.dtype)
    else:
        xo_ref = refs[4]
    xo_ref[...] = xn


class _Rows:
    def __init__(self, mod_all, n_rows, rows_per_batch, mod_row0, tm, n_classes=1):
        self.dm = mod_all.shape[-1] // 6
        self.n_classes = n_classes
        self.npc = rows_per_batch // n_classes
        self.tm = min(tm, n_rows)
        self.n_rows = n_rows
        self.per_token = rows_per_batch == 1
        if self.per_token:
            self.mod = mod_all
            assert mod_row0 % self.tm == 0 and n_rows == self.tm
        else:
            self.mod = mod_all.reshape(mod_all.shape[0], mod_all.shape[1], 1, mod_all.shape[2])
            assert rows_per_batch % self.tm == 0
        self.tiles_per_batch = max(rows_per_batch // self.tm, 1)
        self.mod_row0 = mod_row0

    def rows(self):
        return pl.BlockSpec((self.tm, self.dm), lambda i: (i, 0))

    def vec(self, layer, chunk):
        if self.per_token:
            return pl.BlockSpec((None, self.tm, self.dm), lambda i: (layer, self.mod_row0 // self.tm, chunk))
        return pl.BlockSpec((None, None, 1, self.dm),
                            lambda i: (layer, self.mod_row0 + i // self.tiles_per_batch, 0, chunk))

    def class_view(self, a):
        return a.reshape(self.n_rows // self.n_classes, self.n_classes * self.dm)

    def class_tile(self, natural):
        if natural:
            return pl.BlockSpec((self.npc, self.dm), lambda g: (g // self.n_classes, g % self.n_classes))
        return pl.BlockSpec((self.npc, self.dm), lambda g: (g, 0))

    def class_vec(self, layer, chunk):
        return pl.BlockSpec((None, None, 1, self.dm),
                            lambda g: (layer, self.mod_row0 + g // self.n_classes, 0, chunk))


def _gain_spec(dm):
    return pl.BlockSpec((None, 1, dm), lambda i: (0, 0, 0))


def _prenorm(rows, x2, gain, layer, sc_chunk, sh_chunk):
    dm = rows.dm
    return pl.pallas_call(
        _prenorm_kernel,
        grid=(rows.n_rows // rows.tm,),
        in_specs=[rows.rows(), _gain_spec(dm), rows.vec(layer, sc_chunk), rows.vec(layer, sh_chunk)],
        out_specs=rows.rows(),
        out_shape=jax.ShapeDtypeStruct((rows.n_rows, dm), BF16),
        compiler_params=_cparams(("parallel",)),
        name="prenorm",
    )(x2, gain.astype(F32).reshape(1, 1, dm), rows.mod, rows.mod)


def _postnorm(rows, x2, y2, gain_post, layer, gt_chunk, nxt, y_cm=False, h_cm=False):
    dm = rows.dm
    tiled = y_cm or h_cm
    if tiled:
        nat, vec, grid = rows.class_tile(True), rows.class_vec, rows.n_rows // rows.npc
        y_spec, h_spec = rows.class_tile(not y_cm), rows.class_tile(not h_cm)
        view = rows.class_view
        nat_shape = (rows.n_rows // rows.n_classes, rows.n_classes * dm)
    else:
        nat = y_spec = h_spec = rows.rows()
        vec, grid = rows.vec, rows.n_rows // rows.tm
        view = lambda a: a
        nat_shape = (rows.n_rows, dm)
    in_specs = [nat, y_spec, vec(layer, gt_chunk), _gain_spec(dm)]
    args = [view(x2), y2 if y_cm else view(y2), rows.mod, gain_post.astype(F32).reshape(1, 1, dm)]
    out_specs = [nat]
    out_shape = [jax.ShapeDtypeStruct(nat_shape, F32)]
    if nxt is not None:
        gain_pre, nl, sc_chunk, sh_chunk = nxt
        in_specs += [_gain_spec(dm), vec(nl, sc_chunk), vec(nl, sh_chunk)]
        args += [gain_pre.astype(F32).reshape(1, 1, dm), rows.mod, rows.mod]
        out_specs.append(h_spec)
        out_shape.append(jax.ShapeDtypeStruct((rows.n_rows, dm) if h_cm else nat_shape, BF16))
    res = pl.pallas_call(
        functools.partial(_postnorm_kernel, has_next=nxt is not None),
        grid=(grid,),
        in_specs=in_specs, out_specs=out_specs, out_shape=out_shape,
        compiler_params=_cparams(("parallel",)),
        name="postnorm",
    )(*args)
    x_new = res[0].reshape(rows.n_rows, dm)
    return (x_new, res[1].reshape(rows.n_rows, dm)) if nxt is not None else (x_new, None)


def _ada_kernel(c_ref, w_ref, b_ref, o_ref):
    o_ref[...] = jnp.dot(c_ref[...], w_ref[...].astype(BF16), preferred_element_type=F32) + b_ref[...]


def _ada(sc_all, ada_w, ada_b, tn=512):
    nl, k, n = ada_w.shape
    r = sc_all.shape[0]
    return pl.pallas_call(
        _ada_kernel,
        grid=(nl, n // tn),
        in_specs=[pl.BlockSpec((r, k), lambda l, j: (0, 0)),
                  pl.BlockSpec((None, k, tn), lambda l, j: (l, 0, j)),
                  pl.BlockSpec((None, 1, tn), lambda l, j: (l, 0, j))],
        out_specs=pl.BlockSpec((None, r, tn), lambda l, j: (l, 0, j)),
        out_shape=jax.ShapeDtypeStruct((nl, r, n), F32),
        compiler_params=_cparams(("parallel", "arbitrary")),
        name="ada",
    )(sc_all, ada_w, ada_b.astype(F32).reshape(nl, 1, n))


def _rope_tables(pos, d):
    half = d // 2
    inv = jnp.exp(jnp.arange(half, dtype=F32) * (-2.0 * math.log(ROPE_THETA) / d))
    ang = pos.astype(F32)[:, None] * inv[None, :]
    cos, sin = jnp.cos(ang), jnp.sin(ang)
    reps = LANES // d
    return (jnp.tile(jnp.concatenate([cos, cos], axis=-1), (1, reps)),
            jnp.tile(jnp.concatenate([-sin, sin], axis=-1), (1, reps)))


def _rope_lanes(x, cos, sin, d):
    if d == LANES:
        partner = pltpu.roll(x, LANES // 2, axis=1)
    else:
        lane = lax.broadcasted_iota(jnp.int32, x.shape, 1)
        half = d // 2
        partner = jnp.where((lane % d) < half, pltpu.roll(x, LANES - half, axis=1), pltpu.roll(x, half, axis=1))
    return x * cos + partner * sin


def _band_attn_kernel(*refs, patterns, d, kv_per_block, head_split, n_classes, seq, scale, has_sink):
    if has_sink:
        sink_ref, refs = refs[0], refs[1:]
    q_ref, k_ref, v_ref, cos_ref, sin_ref, o_ref, krot_ref = refs[:7]
    scratch = refs[7:]
    n_pat = len(patterns)
    if n_pat > 1:
        acc_scr, m_scr, l_scr, qrot_scr = scratch
    heads_per_unit = LANES // d
    n_units = q_ref.shape[1] // LANES
    units_per_kv = n_units // kv_per_block
    heads_per_kv = units_per_kv * heads_per_unit
    units_per_blk = units_per_kv // head_split
    n_rs = units_per_blk * heads_per_unit
    rows_q = n_rs * BAND
    npc = seq // n_classes
    log2e = math.log2(math.e)
    q_scale = scale * log2e

    chunk = min(256, seq)

    def rope_all(i, _):
        rows = pl.ds(pl.multiple_of(i * chunk, chunk), chunk)
        cos_c, sin_c = cos_ref[rows, :], sin_ref[rows, :]
        krot_ref[rows, :] = _rope_lanes(k_ref[rows, :], cos_c, sin_c, d)
        if n_pat > 1:
            for u in range(n_units):
                lanes = slice(u * LANES, (u + 1) * LANES)
                qrot_scr[rows, lanes] = _rope_lanes(q_ref[rows, lanes], cos_c, sin_c, d) * q_scale
        return 0

    lax.fori_loop(0, seq // chunk, rope_all, 0)

    lane_q = lax.broadcasted_iota(jnp.int32, (BAND, LANES), 1)

    def band_mask(s, first):
        qlen = BAND // s
        klen = qlen if first else 2 * qlen
        nk = s * klen
        rq = lax.broadcasted_iota(jnp.int32, (BAND, nk), 0)
        rk = lax.broadcasted_iota(jnp.int32, (BAND, nk), 1)
        lq, lk = qlen.bit_length() - 1, klen.bit_length() - 1
        rel = s * ((rq & (qlen - 1)) - (rk & (klen - 1))) + ((rq >> lq) - (rk >> lk)) + (0 if first else BAND)
        return (rel >= 0) & (rel <= BAND)

    def gather(ref, runs, lanes=None):
        rd = lambda st, ln: ref[pl.ds(st, ln), :] if lanes is None else ref[pl.ds(st, ln), lanes]
        parts = [rd(st, ln) for st, ln in runs]
        return parts[0] if len(parts) == 1 else jnp.concatenate(parts, axis=0)

    def scatter(store, runs, value):
        off = 0
        for st, ln in runs:
            store(pl.ds(st, ln), value[off:off + ln])
            off += ln

    def dup(a, j):
        if d == LANES:
            return a
        lane = lax.broadcasted_iota(jnp.int32, a.shape, 1)
        rolled = pltpu.roll(a, LANES // 2, axis=1)
        return jnp.where(lane < d, a, rolled) if j == 0 else jnp.where(lane < d, rolled, a)

    def block(p_idx, kvj, ug, q_runs, k_runs, valid):
        first, last = p_idx == 0, p_idx == n_pat - 1
        nk = sum(ln for _, ln in k_runs)
        if n_pat == 1:
            cosq, sinq = gather(cos_ref, q_runs), gather(sin_ref, q_runs)
        parts = []
        unit0 = kvj * units_per_kv + ug * units_per_blk
        hs0 = ug * n_rs
        for u in range(units_per_blk):
            lo = (unit0 + u) * LANES
            if n_pat > 1:
                xr = gather(qrot_scr, q_runs, slice(lo, lo + LANES))
            else:
                xr = _rope_lanes(gather(q_ref, q_runs, slice(lo, lo + LANES)), cosq, sinq, d) * q_scale
            if heads_per_unit == 1:
                parts.append(xr)
            else:
                parts += [jnp.where(lane_q < d, xr, 0.0), jnp.where(lane_q >= d, xr, 0.0)]
        qst = jnp.concatenate(parts, axis=0).astype(BF16)
        kk = dup(gather(krot_ref, k_runs), kvj).astype(BF16)
        vv = dup(gather(v_ref, k_runs), kvj).astype(BF16)
        vaug = jnp.concatenate([vv, jnp.ones((nk, LANES), BF16)], axis=1)
        s = lax.dot_general(qst, kk, (((1,), (1,)), ((), ())), preferred_element_type=F32)
        s = jnp.where(valid[None], s.reshape(n_rs, BAND, nk), NEG).reshape(rows_q, nk)
        m_blk = jnp.max(s, axis=-1, keepdims=True)
        if first:
            if has_sink:
                head0 = (pl.program_id(1) * kv_per_block + kvj) * heads_per_kv + hs0
                m_old = jnp.concatenate(
                    [jnp.full((BAND, LANES), sink_ref[head0 + h] * log2e, F32) for h in range(n_rs)], axis=0)
                l_old = 1.0
            else:
                m_old, l_old = jnp.full((rows_q, LANES), NEG, F32), 0.0
        else:
            m_old = jnp.concatenate([gather(m_scr.at[hs0 + h], q_runs) for h in range(n_rs)], axis=0)
        m_new = jnp.maximum(m_old, m_blk)
        p = jnp.exp2(s - jnp.concatenate([m_new] * (nk // LANES), axis=1))
        pv = jnp.dot(p.astype(BF16), vaug, preferred_element_type=F32)
        alpha = jnp.exp2(m_old - m_new)
        if first:
            acc = pv[:, :LANES]
            l_new = pv[:, LANES:] + alpha * l_old
        else:
            acc = alpha * jnp.concatenate([gather(acc_scr.at[hs0 + h], q_runs) for h in range(n_rs)], axis=0)
            acc = acc + pv[:, :LANES]
            l_new = alpha * jnp.concatenate([gather(l_scr.at[hs0 + h], q_runs) for h in range(n_rs)], axis=0)
            l_new = l_new + pv[:, LANES:]
        if last:
            out = acc / l_new
            for u in range(units_per_blk):
                lo = (unit0 + u) * LANES
                if heads_per_unit == 1:
                    piece = out[u * BAND:(u + 1) * BAND]
                else:
                    piece = jnp.where(lane_q < d, out[2 * u * BAND:(2 * u + 1) * BAND],
                                      out[(2 * u + 1) * BAND:(2 * u + 2) * BAND])

                def store_o(rows, val, lo=lo):
                    o_ref[rows, lo:lo + LANES] = val

                scatter(store_o, q_runs, piece.astype(o_ref.dtype))
        else:
            for h in range(n_rs):
                for scr, val in ((acc_scr, acc), (l_scr, l_new), (m_scr, m_new)):
                    def store_s(rows, v, scr=scr, h=hs0 + h):
                        scr[h, rows, :] = v

                    scatter(store_s, q_runs, val[h * BAND:(h + 1) * BAND])

    for p_idx, (window, r) in enumerate(patterns):
        assert window // r == BAND and n_classes % r == 0
        s = n_classes // r
        qlen = BAND // s
        nblk = npc // qlen
        valid_first, valid_rest = band_mask(s, True), band_mask(s, False)
        streams = [(kvj, ug, cr) for kvj in range(kv_per_block) for ug in range(head_split) for cr in range(r)]

        def one_block(kvj, ug, cr, m, first_blk, p_idx=p_idx, r=r, s=s, qlen=qlen,
                      valid_first=valid_first, valid_rest=valid_rest):
            base = [npc * (cr + r * a) for a in range(s)]
            if first_blk:
                q_runs = [(b0, qlen) for b0 in base]
                block(p_idx, kvj, ug, q_runs, q_runs, valid_first)
            else:
                off = pl.multiple_of(qlen * m, qlen)
                q_runs = [(b0 + off, qlen) for b0 in base]
                k_runs = [(b0 + off - qlen, 2 * qlen) for b0 in base]
                block(p_idx, kvj, ug, q_runs, k_runs, valid_rest)

        for kvj, ug, cr in streams:
            one_block(kvj, ug, cr, 0, True)
        if nblk > 1:
            unroll = 1
            if len(streams) < 3:
                unroll = next((u for u in (2, 3, 4, 5) if (nblk - 1) % u == 0 and len(streams) * u >= 3), 1)

            def body(it, _, streams=streams, unroll=unroll, one_block=one_block):
                for u in range(unroll):
                    for kvj, ug, cr in streams:
                        one_block(kvj, ug, cr, 1 + it * unroll + u, False)
                return 0

            lax.fori_loop(0, (nblk - 1) // unroll, body, 0)


def _class_major_positions(seq, n_classes):
    return jnp.arange(seq).reshape(seq // n_classes, n_classes).T.reshape(seq)


def _band_attention(y3, sinks, *, patterns, n_classes, d, n_q_heads, n_kv_heads, q_col, k_col, v_col, name,
                    head_split=1):
    nb, seq, _ = y3.shape
    kv_per_block = LANES // d
    n_kvb = n_kv_heads // kv_per_block
    qw = n_q_heads * d // n_kvb
    n_rs = n_q_heads // n_kv_heads
    cos, sin = _rope_tables(_class_major_positions(seq, n_classes), d)
    kern = functools.partial(_band_attn_kernel, patterns=patterns, d=d, kv_per_block=kv_per_block,
                             head_split=head_split, n_classes=n_classes, seq=seq, scale=d ** -0.5,
                             has_sink=sinks is not None)
    in_specs = [pl.BlockSpec((None, seq, qw), lambda b, j, *_: (b, 0, q_col // qw + j)),
                pl.BlockSpec((None, seq, LANES), lambda b, j, *_: (b, 0, k_col // LANES + j)),
                pl.BlockSpec((None, seq, LANES), lambda b, j, *_: (b, 0, v_col // LANES + j)),
                pl.BlockSpec((seq, LANES), lambda b, j, *_: (0, 0)),
                pl.BlockSpec((seq, LANES), lambda b, j, *_: (0, 0))]
    out_specs = [pl.BlockSpec((None, seq, qw), lambda b, j, *_: (b, 0, j)),
                 pl.BlockSpec((None, seq, LANES), lambda b, j, *_: (b, 0, j))]
    out_shape = [jax.ShapeDtypeStruct((nb, seq, n_q_heads * d), BF16),
                 jax.ShapeDtypeStruct((nb, seq, n_kv_heads * d), F32)]
    scratch = []
    if len(patterns) > 1:
        scratch = [pltpu.VMEM((n_rs, seq, LANES), F32) for _ in range(3)] + [pltpu.VMEM((seq, qw), F32)]
    assert q_col % qw == 0 and k_col % LANES == 0 and v_col % LANES == 0
    if sinks is None:
        gs = pl.GridSpec(grid=(nb, n_kvb), in_specs=in_specs, out_specs=out_specs, scratch_shapes=scratch)
        args = (y3, y3, y3, cos, sin)
    else:
        gs = pltpu.PrefetchScalarGridSpec(num_scalar_prefetch=1, grid=(nb, n_kvb), in_specs=in_specs,
                                          out_specs=out_specs, scratch_shapes=scratch)
        args = (sinks.astype(F32), y3, y3, y3, cos, sin)
    return pl.pallas_call(kern, grid_spec=gs, out_shape=out_shape,
                          compiler_params=_cparams(("parallel", "arbitrary")), name=name)(*args)


def _mlstm_kernel(q_ref, k_ref, v_ref, og_ref, igc_ref, lfc_ref, igr_ref, lfr_ref, hn_ref,
                  o_ref, c_ref, n_ref, m_ref, *, seq, chunk):
    L = chunk
    ri = lax.broadcasted_iota(jnp.int32, (L, L), 0)
    ci = lax.broadcasted_iota(jnp.int32, (L, L), 1)
    tril = ri >= ci
    tril_f = tril.astype(F32)
    triu_f = (ri <= ci).astype(F32)
    c_ref[...] = jnp.zeros_like(c_ref)
    n_ref[...] = jnp.zeros_like(n_ref)
    m_ref[...] = jnp.zeros_like(m_ref)
    hp = lax.Precision.HIGHEST

    def step(c, _):
        rows = pl.ds(pl.multiple_of(c * L, L), L)
        m = m_ref[...]
        qc = q_ref[rows, :] * (A_DK ** -0.5)
        kc = k_ref[rows, :]
        vc = v_ref[rows, :].astype(BF16)
        icol, fcol = igc_ref[rows, :], lfc_ref[rows, :]
        irow, frow = igr_ref[c], lfr_ref[c]
        bcol = jnp.dot(tril_f, jnp.broadcast_to(fcol, (L, LANES)), precision=hp,
                       preferred_element_type=F32)[:, :1]
        brow = jnp.dot(jnp.broadcast_to(frow, (8, L)), triu_f, precision=hp, preferred_element_type=F32)[:1, :]
        acol, arow = icol - bcol, irow - brow
        log_d = jnp.where(tril, bcol + arow, NEG)
        inter = bcol + m
        mt = jnp.maximum(inter, jnp.max(log_d, axis=-1, keepdims=True))
        dmat = jnp.exp(log_d - mt)
        wi = jnp.exp(inter - mt)
        qb = qc.astype(BF16)
        sc = lax.dot_general(qb, kc.astype(BF16), (((1,), (1,)), ((), ())), preferred_element_type=F32) * dmat
        cm = c_ref[...]
        nv = n_ref[...]
        num = (jnp.dot(sc.astype(BF16), vc, preferred_element_type=F32)
               + wi * jnp.dot(qb, cm.astype(BF16), preferred_element_type=F32))
        den = jnp.sum(sc, axis=-1, keepdims=True) + wi * jnp.sum(qc * nv, axis=-1, keepdims=True)
        hc = num / jnp.maximum(jnp.abs(den), jnp.exp(-mt))
        hn = hc * lax.rsqrt(jnp.mean(hc * hc, axis=-1, keepdims=True) + NORM_EPS) * hn_ref[...]
        o_ref[rows, :] = (hn * jax.nn.sigmoid(og_ref[rows, :])).astype(o_ref.dtype)
        blast = bcol[L - 1:L, :]
        m_new = jnp.maximum(blast + m, jnp.max(blast + arow, axis=-1, keepdims=True))
        w_c = jnp.exp(blast + m - m_new)
        kw = kc * jnp.exp(blast + acol - m_new)
        c_ref[...] = w_c * cm + lax.dot_general(kw.astype(BF16), vc, (((0,), (0,)), ((), ())),
                                                preferred_element_type=F32)
        n_ref[...] = w_c * nv + jnp.sum(kw, axis=0, keepdims=True)
        m_ref[...] = m_new
        return 0

    lax.fori_loop(0, seq // L, step, 0)


def _mlstm_prompt(y3, ig, lf, head_norm, *, q_col, k_col, v_col, o_col):
    nb, seq, _ = y3.shape
    L = A_CHUNK
    nc = seq // L
    igc = jnp.transpose(ig, (0, 2, 1))[..., None]
    lfc = jnp.transpose(lf, (0, 2, 1))[..., None]
    igr = igc.reshape(nb, A_HEADS, nc, 1, L)
    lfr = lfc.reshape(nb, A_HEADS, nc, 1, L)
    hn = head_norm.reshape(A_HEADS, 1, A_DV).astype(F32)
    col = lambda c0, w: (lambda b, h: (b, 0, c0 // w + h))
    gcol = pl.BlockSpec((None, None, seq, 1), lambda b, h: (b, h, 0, 0))
    grow = pl.BlockSpec((None, None, nc, 1, L), lambda b, h: (b, h, 0, 0, 0))
    return pl.pallas_call(
        functools.partial(_mlstm_kernel, seq=seq, chunk=L),
        grid=(nb, A_HEADS),
        in_specs=[pl.BlockSpec((None, seq, A_DK), col(q_col, A_DK)),
                  pl.BlockSpec((None, seq, A_DK), col(k_col, A_DK)),
                  pl.BlockSpec((None, seq, A_DV), col(v_col, A_DV)),
                  pl.BlockSpec((None, seq, A_DV), col(o_col, A_DV)),
                  gcol, gcol, grow, grow,
                  pl.BlockSpec((None, 1, A_DV), lambda b, h: (h, 0, 0))],
        out_specs=[pl.BlockSpec((None, seq, A_DV), lambda b, h: (b, 0, h)),
                   pl.BlockSpec((None, None, A_DK, A_DV), lambda b, h: (b, h, 0, 0)),
                   pl.BlockSpec((None, None, 1, A_DK), lambda b, h: (b, h, 0, 0)),
                   pl.BlockSpec((None, None, 1, 1), lambda b, h: (b, h, 0, 0))],
        out_shape=[jax.ShapeDtypeStruct((nb, seq, A_HEADS * A_DV), BF16),
                   jax.ShapeDtypeStruct((nb, A_HEADS, A_DK, A_DV), F32),
                   jax.ShapeDtypeStruct((nb, A_HEADS, 1, A_DK), F32),
                   jax.ShapeDtypeStruct((nb, A_HEADS, 1, 1), F32)],
        compiler_params=_cparams(("parallel", "arbitrary")),
        name="mlstm_prompt",
    )(y3, y3, y3, y3, igc, lfc, igr, lfr, hn)


def _mlstm_step_kernel(q_ref, k_ref, v_ref, og_ref, ig_ref, lf_ref, c0_ref, n0_ref, m0_ref, hn_ref,
                       o_ref, c1_ref, n1_ref, m1_ref):
    for h in range(A_HEADS):
        q = q_ref[h] * (A_DK ** -0.5)
        k = k_ref[h]
        v = v_ref[h]
        c0 = c0_ref[h]
        n0 = n0_ref[h]
        ig, lf, m0 = ig_ref[h], lf_ref[h], m0_ref[h]
        inter = lf + m0
        mt = jnp.maximum(inter, ig)
        dm = jnp.exp(ig - mt)
        wi = jnp.exp(inter - mt)
        sc = jnp.sum(q * k, axis=0, keepdims=True) * dm
        qc = jnp.sum(c0 * q, axis=0, keepdims=True)
        num = sc * v + wi * qc
        den = sc + wi * jnp.sum(q * n0, axis=0, keepdims=True)
        hc = num / jnp.maximum(jnp.abs(den), jnp.exp(-mt))
        hn = hc * lax.rsqrt(jnp.mean(hc * hc, axis=-1, keepdims=True) + NORM_EPS) * hn_ref[h]
        o_ref[h] = (hn * jax.nn.sigmoid(og_ref[h])).astype(o_ref.dtype)
        c1_ref[h] = wi * c0 + (k * dm) * v
        n1_ref[h] = wi * n0 + dm * k
        m1_ref[h] = mt


def _mlstm_sample(y, ig, lf, head_norm, c0, n0, m0):
    nb = y.shape[0]
    col = lambda a: a.reshape(nb, A_HEADS, A_DK, 1)
    row = lambda a: a.reshape(nb, A_HEADS, 1, A_DV)
    sca = lambda a: a.astype(F32).reshape(nb, A_HEADS, 1, 1)
    cspec = pl.BlockSpec((None, A_HEADS, A_DK, 1), lambda b: (b, 0, 0, 0))
    rspec = pl.BlockSpec((None, A_HEADS, 1, A_DV), lambda b: (b, 0, 0, 0))
    sspec = pl.BlockSpec((None, A_HEADS, 1, 1), lambda b: (b, 0, 0, 0))
    mspec = pl.BlockSpec((None, A_HEADS, A_DK, A_DV), lambda b: (b, 0, 0, 0))
    o, c1, n1, m1 = pl.pallas_call(
        _mlstm_step_kernel,
        grid=(nb,),
        in_specs=[cspec, cspec, rspec, rspec, sspec, sspec, mspec, cspec, sspec,
                  pl.BlockSpec((A_HEADS, 1, A_DV), lambda b: (0, 0, 0))],
        out_specs=[rspec, mspec, cspec, sspec],
        out_shape=[jax.ShapeDtypeStruct((nb, A_HEADS, 1, A_DV), F32),
                   jax.ShapeDtypeStruct((nb, A_HEADS, A_DK, A_DV), F32),
                   jax.ShapeDtypeStruct((nb, A_HEADS, A_DK, 1), F32),
                   jax.ShapeDtypeStruct((nb, A_HEADS, 1, 1), F32)],
        compiler_params=_cparams(("parallel",)),
        name="mlstm_step",
    )(col(y[:, A_Q:A_Q + A_HEADS * A_DK]), col(y[:, A_K:A_K + A_HEADS * A_DK]),
      row(y[:, A_V:A_V + A_HEADS * A_DV]), row(y[:, A_O:A_O + A_HEADS * A_DV]),
      sca(ig), sca(lf), c0.astype(F32), col(n0.astype(F32)), sca(m0),
      head_norm.astype(F32).reshape(A_HEADS, 1, A_DV))
    return (o.reshape(nb, A_HEADS * A_DV), c1, n1.reshape(nb, A_HEADS, A_DK), m1.reshape(nb, A_HEADS))


def _decode_attn_kernel(*refs, n_pat, d, group, scale, has_sink):
    q_ref, kn_ref, vn_ref, cos_ref, sin_ref = refs[:5]
    refs = refs[5:]
    if has_sink:
        sink_ref, refs = refs[0], refs[1:]
    k_refs, v_refs = refs[:n_pat], refs[n_pat:2 * n_pat]
    o_ref, knew_ref = refs[2 * n_pat:]
    n_seg = LANES // d
    cos, sin = cos_ref[...], sin_ref[...]
    kn = _rope_lanes(kn_ref[...], cos, sin, d)
    knew_ref[...] = kn
    vn = vn_ref[...]
    lane = lax.broadcasted_iota(jnp.int32, kn.shape, 1)

    def seg_sum(p):
        if n_seg == 1:
            return jnp.sum(p, axis=-1, keepdims=True)
        lane_p = lax.broadcasted_iota(jnp.int32, p.shape, p.ndim - 1)
        lo = jnp.sum(jnp.where(lane_p < d, p, 0.0), axis=-1, keepdims=True)
        hi = jnp.sum(jnp.where(lane_p >= d, p, 0.0), axis=-1, keepdims=True)
        return jnp.where(lane_p < d, lo, hi)

    for g in range(group):
        qg = _rope_lanes(q_ref[g], cos, sin, d)
        s_new = seg_sum(kn * qg) * scale
        s = [seg_sum(k_refs[p][...] * qg[None]) * scale for p in range(n_pat)]
        m = s_new
        for p in range(n_pat):
            m = jnp.maximum(m, jnp.max(s[p], axis=0))
        if has_sink:
            m = jnp.maximum(m, sink_ref[g])
        e_new = jnp.exp(s_new - m) * float(n_pat)
        l = e_new
        acc = e_new * vn
        for p in range(n_pat):
            e = jnp.exp(s[p] - m[None])
            l = l + jnp.sum(e, axis=0)
            acc = acc + jnp.sum(e * v_refs[p][...], axis=0)
        if has_sink:
            l = l + jnp.exp(sink_ref[g] - m)
        o_ref[g] = (acc / l).astype(o_ref.dtype)


def _decode_attention(q, k_new, v_new, k_cache, v_cache, sinks, pos, *, patterns, d, name):
    nb, n_heads, _ = q.shape
    n_kv, cache_len = k_cache.shape[2], k_cache.shape[1]
    group = n_heads // n_kv
    n_seg = LANES // d
    r_rows = n_kv // n_seg
    to_tiles = lambda a, inner: a.reshape(nb, r_rows, n_seg, inner, d).transpose(0, 3, 1, 2, 4).reshape(
        nb, inner, r_rows, LANES)
    qt = to_tiles(q, group)
    knt = k_new.reshape(nb, r_rows, LANES)
    vnt = v_new.reshape(nb, r_rows, LANES)
    cos, sin = _rope_tables(pos, d)
    kc = k_cache.reshape(nb, cache_len, r_rows, LANES)
    vc = v_cache.reshape(nb, cache_len, r_rows, LANES)
    tile = pl.BlockSpec((None, r_rows, LANES), lambda b: (b, 0, 0))
    one = pl.BlockSpec((1, LANES), lambda b: (0, 0))
    in_specs = [pl.BlockSpec((None, group, r_rows, LANES), lambda b: (b, 0, 0, 0)), tile, tile, one, one]
    args = [qt, knt, vnt, cos, sin]
    if sinks is not None:
        sk = jnp.broadcast_to(sinks.astype(F32).reshape(r_rows, n_seg, group, 1), (r_rows, n_seg, group, d))
        args.append(sk.transpose(2, 0, 1, 3).reshape(group, r_rows, LANES))
        in_specs.append(pl.BlockSpec((group, r_rows, LANES), lambda b: (0, 0, 0)))
    views = []
    for window, r in patterns:
        assert window // r == BAND and cache_len % (BAND * r) == 0
        blk = cache_len // r // BAND - 1
        views.append(pl.BlockSpec((None, BAND, None, r_rows, LANES), lambda b, blk=blk: (b, blk, 0, 0, 0)))
    in_specs += views + views
    args += [kc.reshape(nb, cache_len // r, r, r_rows, LANES) for _, r in patterns]
    args += [vc.reshape(nb, cache_len // r, r, r_rows, LANES) for _, r in patterns]
    o, kn = pl.pallas_call(
        functools.partial(_decode_attn_kernel, n_pat=len(patterns), d=d, group=group, scale=d ** -0.5,
                          has_sink=sinks is not None),
        grid=(nb,),
        in_specs=in_specs,
        out_specs=[pl.BlockSpec((None, group, r_rows, LANES), lambda b: (b, 0, 0, 0)), tile],
        out_shape=[jax.ShapeDtypeStruct((nb, group, r_rows, LANES), F32),
                   jax.ShapeDtypeStruct((nb, r_rows, LANES), F32)],
        compiler_params=_cparams(("parallel",)),
        name=name,
    )(*args)
    o = o.reshape(nb, group, r_rows, n_seg, d).transpose(0, 2, 3, 1, 4).reshape(nb, n_heads * d)
    return o, kn.reshape(nb, n_kv, d)


def _shift_kernel(c_ref, new_ref, o_ref, *, length, chunk):
    n_full = (length - 1) // chunk

    def body(i, _):
        o_ref[pl.ds(i * chunk, chunk)] = c_ref[pl.ds(i * chunk + 1, chunk)]
        return 0

    lax.fori_loop(0, n_full, body, 0)
    done = n_full * chunk
    if done < length - 1:
        o_ref[done:length - 1] = c_ref[done + 1:length]
    o_ref[length - 1] = new_ref[...]


def _shift_cache(cache, new, name):
    nb, length, n_kv, d = cache.shape
    return pl.pallas_call(
        functools.partial(_shift_kernel, length=length, chunk=min(64, length - 1)),
        grid=(nb,),
        in_specs=[pl.BlockSpec((None, length, n_kv, d), lambda b: (b, 0, 0, 0)),
                  pl.BlockSpec((None, n_kv, d), lambda b: (b, 0, 0))],
        out_specs=pl.BlockSpec((None, length, n_kv, d), lambda b: (b, 0, 0, 0)),
        out_shape=jax.ShapeDtypeStruct(cache.shape, cache.dtype),
        compiler_params=_cparams(("parallel",)),
        name=name,
    )(cache, new.astype(cache.dtype))


N_IN_EVEN = 2 * A_HEADS * A_DK + 2 * A_HEADS * A_DV + 2 * A_HEADS + (B_HEADS + 2 * B_KV_HEADS) * B_HEAD_DIM
EVEN_IN = N_IN_EVEN // 512 * 512
A_Q, A_K, A_V, A_O = 0, 1024, 2048, 4096
GATE_COL = 6144
B_COL = GATE_COL + 2 * A_HEADS
B_Q, B_K, B_V = 0, 2048, 2304
C_Q, C_K, C_V = 0, 4096, 5120
SWA_PATTERN = ((B_WINDOW, 1),)


def _even_projection(h2, w_in, j, tm):
    y = _matmul([(h2, w_in, j, 0)], EVEN_IN, F32, tm, 512, "even_in")
    n_tail = N_IN_EVEN - EVEN_IN
    w_tail = jnp.pad(w_in[j, :, EVEN_IN:], ((0, 0), (0, LANES - n_tail)))
    y_tail = _matmul([(h2, w_tail, None, 0)], LANES, F32, tm, LANES, "even_in_tail")
    yb = jnp.concatenate([y[:, B_COL:], y_tail[:, :n_tail]], axis=-1)
    return y, y[:, GATE_COL:B_COL], yb


def _gates(yg, gate_b):
    gb = gate_b.astype(F32)
    ig = yg[..., :A_HEADS] + gb[:A_HEADS]
    lf = jax.nn.log_sigmoid(yg[..., A_HEADS:] + gb[A_HEADS:])
    return ig, lf


def _even_mixer_prompt(h2, nb, seq, j, w_in, gate_b, head_norm, sinks, w_out):
    y, yg, yb = _even_projection(h2, w_in, j, 1024)
    y3, yb3 = y.reshape(nb, seq, -1), yb.reshape(nb, seq, -1)
    ig, lf = _gates(yg.reshape(nb, seq, -1), gate_b)
    ha, c1, n1, m1 = _mlstm_prompt(y3, ig, lf, head_norm, q_col=A_Q, k_col=A_K, v_col=A_V, o_col=A_O)
    ob, kb = _band_attention(yb3, sinks, patterns=SWA_PATTERN, n_classes=1, d=B_HEAD_DIM, n_q_heads=B_HEADS,
                             n_kv_heads=B_KV_HEADS, q_col=B_Q, k_col=B_K, v_col=B_V, name="swa_prompt",
                             head_split=2)
    out = _matmul([(ha.reshape(nb * seq, -1), w_out, j, 0), (ob.reshape(nb * seq, -1), w_out, j, 1)],
                  D_MODEL, F32, 1024, 512, "even_out")
    kbuf = kb[:, -B_WINDOW:].reshape(nb, -1, B_KV_HEADS, B_HEAD_DIM)
    vbuf = yb3[:, -B_WINDOW:, B_V:].reshape(nb, -1, B_KV_HEADS, B_HEAD_DIM)
    return out, (c1, n1.reshape(nb, A_HEADS, A_DK), m1.reshape(nb, A_HEADS), kbuf, vbuf)


def _even_mixer_sample(h2, j, w_in, gate_b, head_norm, sinks, w_out, state):
    nb = h2.shape[0]
    y, yg, yb = _even_projection(h2, w_in, j, nb)
    ig, lf = _gates(yg, gate_b)
    c0, n0, m0, k_old, v_old = state
    ha, c1, n1, m1 = _mlstm_sample(y, ig, lf, head_norm, c0, n0, m0)
    q = yb[:, B_Q:B_K].reshape(nb, B_HEADS, B_HEAD_DIM)
    k_new = yb[:, B_K:B_V].reshape(nb, B_KV_HEADS, B_HEAD_DIM)
    v_new = yb[:, B_V:].reshape(nb, B_KV_HEADS, B_HEAD_DIM)
    ob, k_rot = _decode_attention(q, k_new, v_new, k_old, v_old, sinks, PAST_LEN + jnp.arange(1),
                                  patterns=SWA_PATTERN, d=B_HEAD_DIM, name="swa_step")
    out = _matmul([(ha.astype(BF16), w_out, j, 0), (ob.astype(BF16), w_out, j, 1)],
                  D_MODEL, F32, nb, 512, "even_out_s")
    return out, (c1, n1, m1, _shift_cache(k_old, k_rot, "swa_k_shift"), _shift_cache(v_old, v_new, "swa_v_shift"))


C_CLASSES = max(r for _, r in C_PATTERNS)


def _odd_mixer_prompt(h2, nb, seq, j, w_in, w_out):
    y3 = _matmul([(h2, w_in, j, 0)], w_in.shape[2], F32, 1024, 512, "odd_in").reshape(nb, seq, -1)
    o, kr = _band_attention(y3, None, patterns=C_PATTERNS, n_classes=C_CLASSES, d=C_HEAD_DIM, n_q_heads=C_HEADS,
                            n_kv_heads=C_KV_HEADS, q_col=C_Q, k_col=C_K, v_col=C_V, name="dil_prompt")
    out = _matmul([(o.reshape(nb * seq, -1), w_out, j, 0)], D_MODEL, F32, 1024, 512, "odd_out")
    natural = lambda a: a.reshape(nb, C_CLASSES, seq // C_CLASSES, C_KV_HEADS, C_HEAD_DIM).transpose(
        0, 2, 1, 3, 4).reshape(nb, seq, C_KV_HEADS, C_HEAD_DIM)
    return out, (natural(kr), natural(y3[:, :, C_V:]))


def _odd_mixer_sample(h2, j, w_in, w_out, state):
    nb = h2.shape[0]
    y = _matmul([(h2, w_in, j, 0)], w_in.shape[2], F32, nb, 512, "odd_in_s")
    q = y[:, C_Q:C_K].reshape(nb, C_HEADS, C_HEAD_DIM)
    k_new = y[:, C_K:C_V].reshape(nb, C_KV_HEADS, C_HEAD_DIM)
    v_new = y[:, C_V:].reshape(nb, C_KV_HEADS, C_HEAD_DIM)
    k_old, v_old = state
    o, k_rot = _decode_attention(q, k_new, v_new, k_old, v_old, None, PAST_LEN + jnp.arange(1),
                                 patterns=C_PATTERNS, d=C_HEAD_DIM, name="dil_step")
    out = _matmul([(o.astype(BF16), w_out, j, 0)], D_MODEL, F32, nb, 512, "odd_out_s")
    return out, (_shift_cache(k_old, k_rot, "dil_k_shift"), _shift_cache(v_old, v_new, "dil_v_shift"))


def _trunk(x2, rows, mixers, cm_layers, norms, ffn):
    g_mix_pre, g_mix_post, g_ffn_pre, g_ffn_post = norms
    wg, wu, wd = ffn
    assert 0 not in cm_layers
    h = _prenorm(rows, x2, g_mix_pre[0], 0, MOD_SC1, MOD_SH1)
    states = []
    for li in range(DEPTH):
        y, state = mixers[li](h)
        states.append(state)
        x2, h = _postnorm(rows, x2, y, g_mix_post[li], li, MOD_GT1, (g_ffn_pre[li], li, MOD_SC2, MOD_SH2),
                          y_cm=li in cm_layers)
        a = _swiglu(h, wg, wu, li, 1024, 256)
        f = _matmul([(a, wd, li, 0)], D_MODEL, F32, 512, 512, "ffn_down")
        nxt = (g_mix_pre[li + 1], li + 1, MOD_SC1, MOD_SH1) if li + 1 < DEPTH else None
        x2, h = _postnorm(rows, x2, f, g_ffn_post[li], li, MOD_GT2, nxt, h_cm=li + 1 in cm_layers)
    return x2, states


def kernel(x_prompt, x_sample, state_l0_mlstm_c, state_l0_mlstm_n, state_l0_mlstm_m, cache_l0_swa_k, cache_l0_swa_v, cache_l1_dil_k, cache_l1_dil_v, state_l2_mlstm_c, state_l2_mlstm_n, state_l2_mlstm_m, cache_l2_swa_k, cache_l2_swa_v, cache_l3_dil_k, cache_l3_dil_v, c_prompt, c_sample, ada_w, ada_b, norm_mix_pre, norm_mix_post, norm_ffn_pre, norm_ffn_post, even_w_in, even_gate_b, even_head_norm, even_sinks, even_w_out, odd_w_in, odd_w_out, ffn_w_gate, ffn_w_up, ffn_w_down):
    past = {0: (state_l0_mlstm_c, state_l0_mlstm_n, state_l0_mlstm_m, cache_l0_swa_k, cache_l0_swa_v),
            1: (cache_l1_dil_k, cache_l1_dil_v),
            2: (state_l2_mlstm_c, state_l2_mlstm_n, state_l2_mlstm_m, cache_l2_swa_k, cache_l2_swa_v),
            3: (cache_l3_dil_k, cache_l3_dil_v)}
    nbp, seq, dm = x_prompt.shape
    nbs = x_sample.shape[0]
    c_all = jnp.concatenate([c_sample, c_prompt], axis=0)
    pad = (-c_all.shape[0]) % 16
    sc_all = jnp.pad(jax.nn.silu(c_all), ((0, pad), (0, 0))).astype(BF16)
    mod_all = _ada(sc_all, ada_w, ada_b)
    mix_p, mix_s = [], []
    for li in range(DEPTH):
        j = li // 2
        if li % 2 == 0:
            mw = (j, even_w_in, even_gate_b[j], even_head_norm[j], even_sinks[j], even_w_out)
            mix_p.append(lambda h, mw=mw: _even_mixer_prompt(h, nbp, seq, *mw))
            mix_s.append(lambda h, mw=mw, st=past[li]: _even_mixer_sample(h, *mw, st))
        else:
            mw = (j, odd_w_in, odd_w_out)
            mix_p.append(lambda h, mw=mw: _odd_mixer_prompt(h, nbp, seq, *mw))
            mix_s.append(lambda h, mw=mw, st=past[li]: _odd_mixer_sample(h, *mw, st))
    norms = (norm_mix_pre, norm_mix_post, norm_ffn_pre, norm_ffn_post)
    ffn = (ffn_w_gate, ffn_w_up, ffn_w_down.astype(BF16))
    rows_p = _Rows(mod_all, nbp * seq, seq, nbs, 256, C_CLASSES)
    rows_s = _Rows(mod_all, nbs, 1, 0, nbs)
    odd_layers = tuple(range(1, DEPTH, 2))
    y_p, new_p = _trunk(x_prompt.reshape(nbp * seq, dm), rows_p, mix_p, odd_layers, norms, ffn)
    y_s, new_s = _trunk(x_sample.reshape(nbs, dm), rows_s, mix_s, (), norms, ffn)
    y_p, y_s = y_p.reshape(x_prompt.shape), y_s.reshape(x_sample.shape)
    (p0c, p0n, p0m, p0k, p0v), (p1k, p1v), (p2c, p2n, p2m, p2k, p2v), (p3k, p3v) = new_p
    (s0c, s0n, s0m, s0k, s0v), (s1k, s1v), (s2c, s2n, s2m, s2k, s2v), (s3k, s3v) = new_s
    return (y_p, y_s,
            p0c, p0n, p0m, p0k, p0v, p1k, p1v, p2c, p2n, p2m, p2k, p2v, p3k, p3v,
            s0c, s0n, s0m, s0k, s0v, s1k, s1v, s2c, s2n, s2m, s2k, s2v, s3k, s3v)
```

```python
import functools
import math

import jax
import jax.numpy as jnp
import numpy as np
from jax import lax
from jax.experimental import pallas as pl
from jax.experimental.pallas import tpu as pltpu

F32 = jnp.float32
BF16 = jnp.bfloat16

D_MODEL = 4096
DEPTH = 4
PAST_LEN = 8192
A_HEADS, A_DK, A_DV, A_CHUNK = 4, 256, 512, 64
B_HEADS, B_KV_HEADS, B_HEAD_DIM, B_WINDOW = 32, 4, 64, 128
C_HEADS, C_KV_HEADS, C_HEAD_DIM = 32, 8, 128
C_PATTERNS = ((128, 1), (512, 4), (2048, 16))
BAND = 128
ROPE_THETA = 10000.0
NORM_EPS = 1e-6
LANES = 128
NEG = -1e30
VMEM_LIMIT = 56 * 1024 * 1024


def _cparams(sem):
    return pltpu.CompilerParams(dimension_semantics=sem, vmem_limit_bytes=VMEM_LIMIT)


def _mm_kernel(*refs, n_in):
    o_ref = refs[-1]
    acc = None
    for i in range(n_in):
        x = refs[2 * i][...].astype(BF16)
        w = refs[2 * i + 1][...].astype(BF16)
        d = jnp.dot(x, w, preferred_element_type=F32)
        acc = d if acc is None else acc + d
    o_ref[...] = acc.astype(o_ref.dtype)


def _w_spec(w, layer, kblk, k, tn):
    if w.ndim == 3:
        return pl.BlockSpec((None, k, tn), lambda i, j: (layer, kblk, j))
    return pl.BlockSpec((k, tn), lambda i, j: (kblk, j))


def _matmul(pairs, n, out_dtype, tm, tn, name):
    m = pairs[0][0].shape[0]
    tm = min(tm, m)
    tn = min(tn, n)
    assert m % tm == 0 and n % tn == 0, (m, n, tm, tn)
    in_specs, args = [], []
    for x, w, layer, kblk in pairs:
        k = x.shape[1]
        in_specs += [pl.BlockSpec((tm, k), lambda i, j: (i, 0)), _w_spec(w, layer, kblk, k, tn)]
        args += [x, w]
    return pl.pallas_call(
        functools.partial(_mm_kernel, n_in=len(pairs)),
        grid=(m // tm, n // tn),
        in_specs=in_specs,
        out_specs=pl.BlockSpec((tm, tn), lambda i, j: (i, j)),
        out_shape=jax.ShapeDtypeStruct((m, n), out_dtype),
        compiler_params=_cparams(("parallel", "arbitrary")),
        name=name,
    )(*args)


def _mm2_kernel(*refs, n_in, swiglu, cast):
    n_w = 2 if swiglu else 1
    per = 2 + n_w
    ins, rest = refs[:n_in * per], refs[n_in * per:]
    op_ref, os_ref = rest[:2]
    w_refs = [ins[(t // n_w) * per + 2 + t % n_w] for t in range(n_in * n_w)]
    i, last = pl.program_id(1), pl.num_programs(1) - 1
    if cast:
        w_scr = rest[2:]

        @pl.when(i == 0)
        def _():
            for t in range(n_in * n_w):
                w_scr[t][...] = w_refs[t][...].astype(BF16)

        w_refs = w_scr

    def compute(xsel, o_ref):
        accs = [None] * n_w
        for p in range(n_in):
            x = ins[p * per + xsel][...]
            for t in range(n_w):
                dt = jnp.dot(x, w_refs[p * n_w + t][...], preferred_element_type=F32)
                accs[t] = dt if accs[t] is None else accs[t] + dt
        r = accs[0] * jax.nn.sigmoid(accs[0]) * accs[1] if swiglu else accs[0]
        o_ref[...] = r.astype(o_ref.dtype)

    @pl.when(i < last)
    def _():
        compute(0, op_ref)

    @pl.when(i == last)
    def _():
        compute(1, os_ref)


def _matmul2(pairs, n, out_dtype, tm, tn, name, swiglu=False):
    m, ms = pairs[0][0].shape[0], pairs[0][1].shape[0]
    assert m % tm == 0 and n % tn == 0, (m, n, tm, tn)
    n_i = m // tm
    in_specs, args, scratch = [], [], []
    cast = (pairs[0][2][0] if swiglu else pairs[0][2]).dtype != BF16
    for xp, xs, ws, layer, kblk in pairs:
        k = xp.shape[1]
        in_specs += [pl.BlockSpec((tm, k), lambda j, i: (jnp.minimum(i, n_i - 1), 0)),
                     pl.BlockSpec((ms, k), lambda j, i: (0, 0))]
        args += [xp, xs]
        for w in (ws if swiglu else (ws,)):
            assert (w.dtype != BF16) == cast
            in_specs.append(pl.BlockSpec((None, k, tn), lambda j, i, layer=layer, kblk=kblk: (layer, kblk, j)))
            args.append(w)
            if cast:
                scratch.append(pltpu.VMEM((k, tn), BF16))
    return pl.pallas_call(
        functools.partial(_mm2_kernel, n_in=len(pairs), swiglu=swiglu, cast=cast),
        grid=(n // tn, n_i + 1),
        in_specs=in_specs,
        out_specs=[pl.BlockSpec((tm, tn), lambda j, i: (jnp.minimum(i, n_i - 1), j)),
                   pl.BlockSpec((ms, tn), lambda j, i: (0, j))],
        out_shape=[jax.ShapeDtypeStruct((m, n), out_dtype), jax.ShapeDtypeStruct((ms, n), out_dtype)],
        scratch_shapes=scratch,
        compiler_params=_cparams(("parallel", "arbitrary")),
        name=name,
    )(*args)


def _digit_swap(n):
    p = lax.broadcasted_iota(jnp.int32, (n * n, n * n), 0)
    q = lax.broadcasted_iota(jnp.int32, (n * n, n * n), 1)
    shift = n.bit_length() - 1
    return (q == ((p & (n - 1)) << shift) + (p >> shift)).astype(BF16)


def _mm_regroup_kernel(x_ref, w_ref, o_ref, xp_scr, *, n_cls, to_classes):
    blk = n_cls * n_cls
    tm = xp_scr.shape[0]

    @pl.when(pl.program_id(1) == 0)
    def _():
        perm = _digit_swap(n_cls)
        for k in range(tm // blk):
            if to_classes:
                xs = x_ref[k * blk:(k + 1) * blk, :]
            else:
                xs = jnp.concatenate([x_ref[c, k * n_cls:(k + 1) * n_cls, :] for c in range(n_cls)], axis=0)
            xp_scr[k * blk:(k + 1) * blk, :] = jnp.dot(perm, xs, preferred_element_type=F32).astype(BF16)

    r = jnp.dot(xp_scr[...], w_ref[...].astype(BF16), preferred_element_type=F32)
    if to_classes:
        for k in range(tm // blk):
            for c in range(n_cls):
                o_ref[c, k * n_cls:(k + 1) * n_cls, :] = r[k * blk + c * n_cls:k * blk + (c + 1) * n_cls]
    else:
        o_ref[...] = r


def _matmul_regroup(x, w, layer, n, nb, n_cls, to_classes, tm, tn, name):
    k = x.shape[-1]
    seq = x.shape[0] // nb if to_classes else n_cls * x.shape[2]
    tiles = seq // tm
    assert seq % tm == 0 and tm % (n_cls * n_cls) == 0 and n % tn == 0
    cls_block = lambda cols: (None, n_cls, tm // n_cls, cols)
    if to_classes:
        x_spec = pl.BlockSpec((tm, k), lambda i, j: (i, 0))
        o_spec = pl.BlockSpec(cls_block(tn), lambda i, j: (i // tiles, 0, i % tiles, j))
        o_shape = (nb, n_cls, seq // n_cls, n)
    else:
        x_spec = pl.BlockSpec(cls_block(k), lambda i, j: (i // tiles, 0, i % tiles, 0))
        o_spec = pl.BlockSpec((tm, tn), lambda i, j: (i, j))
        o_shape = (nb * seq, n)
    return pl.pallas_call(
        functools.partial(_mm_regroup_kernel, n_cls=n_cls, to_classes=to_classes),
        grid=(nb * tiles, n // tn),
        in_specs=[x_spec, _w_spec(w, layer, 0, k, tn)],
        out_specs=o_spec,
        out_shape=jax.ShapeDtypeStruct(o_shape, F32),
        scratch_shapes=[pltpu.VMEM((tm, k), BF16)],
        compiler_params=_cparams(("parallel", "arbitrary")),
        name=name,
    )(x, w)


MOD_SH1, MOD_SC1, MOD_GT1, MOD_SH2, MOD_SC2, MOD_GT2 = range(6)


def _rms_rows(x, g):
    return x * lax.rsqrt(jnp.mean(x * x, axis=-1, keepdims=True) + NORM_EPS) * g


def _prenorm_kernel(x_ref, g_ref, sc_ref, sh_ref, h_ref):
    h_ref[...] = (_rms_rows(x_ref[...], g_ref[...]) * (1.0 + sc_ref[...]) + sh_ref[...]).astype(h_ref.dtype)


def _postnorm_kernel(*refs, has_next):
    x_ref, y_ref, gt_ref, gpost_ref = refs[:4]
    xn = x_ref[...] + gt_ref[...] * _rms_rows(y_ref[...], gpost_ref[...])
    if has_next:
        gpre_ref, sc_ref, sh_ref, xo_ref, h_ref = refs[4:]
        h_ref[...] = (_rms_rows(xn, gpre_ref[...]) * (1.0 + sc_ref[...]) + sh_ref[...]).astype(h_ref.dtype)
    else:
        xo_ref = refs[4]
    xo_ref[...] = xn


class _Rows:
    def __init__(self, mod_all, n_rows, rows_per_batch, mod_row0, tm):
        self.dm = mod_all.shape[-1] // 6
        self.tm = min(tm, n_rows)
        self.n_rows = n_rows
        self.per_token = rows_per_batch == 1
        if self.per_token:
            self.mod = mod_all
            assert mod_row0 % self.tm == 0 and n_rows == self.tm
        else:
            self.mod = mod_all.reshape(mod_all.shape[0], mod_all.shape[1], 1, mod_all.shape[2])
            assert rows_per_batch % self.tm == 0
        self.tiles_per_batch = max(rows_per_batch // self.tm, 1)
        self.mod_row0 = mod_row0

    def rows(self):
        return pl.BlockSpec((self.tm, self.dm), lambda i: (i, 0))

    def vec(self, layer, chunk):
        if self.per_token:
            return pl.BlockSpec((None, self.tm, self.dm), lambda i: (layer, self.mod_row0 // self.tm, chunk))
        return pl.BlockSpec((None, None, 1, self.dm),
                            lambda i: (layer, self.mod_row0 + i // self.tiles_per_batch, 0, chunk))


def _gain_spec(dm):
    return pl.BlockSpec((None, 1, dm), lambda i: (0, 0, 0))


def _prenorm(rows, x2, gain, layer, sc_chunk, sh_chunk):
    dm = rows.dm
    return pl.pallas_call(
        _prenorm_kernel,
        grid=(rows.n_rows // rows.tm,),
        in_specs=[rows.rows(), _gain_spec(dm), rows.vec(layer, sc_chunk), rows.vec(layer, sh_chunk)],
        out_specs=rows.rows(),
        out_shape=jax.ShapeDtypeStruct((rows.n_rows, dm), BF16),
        compiler_params=_cparams(("parallel",)),
        name="prenorm",
    )(x2, gain.astype(F32).reshape(1, 1, dm), rows.mod, rows.mod)


def _postnorm(rows, x2, y2, gain_post, layer, gt_chunk, nxt):
    dm = rows.dm
    in_specs = [rows.rows(), rows.rows(), rows.vec(layer, gt_chunk), _gain_spec(dm)]
    args = [x2, y2, rows.mod, gain_post.astype(F32).reshape(1, 1, dm)]
    out_specs = [rows.rows()]
    out_shape = [jax.ShapeDtypeStruct((rows.n_rows, dm), F32)]
    if nxt is not None:
        gain_pre, nl, sc_chunk, sh_chunk = nxt
        in_specs += [_gain_spec(dm), rows.vec(nl, sc_chunk), rows.vec(nl, sh_chunk)]
        args += [gain_pre.astype(F32).reshape(1, 1, dm), rows.mod, rows.mod]
        out_specs.append(rows.rows())
        out_shape.append(jax.ShapeDtypeStruct((rows.n_rows, dm), BF16))
    res = pl.pallas_call(
        functools.partial(_postnorm_kernel, has_next=nxt is not None),
        grid=(rows.n_rows // rows.tm,),
        in_specs=in_specs, out_specs=out_specs, out_shape=out_shape,
        compiler_params=_cparams(("parallel",)),
        name="postnorm",
    )(*args)
    return (res[0], res[1]) if nxt is not None else (res[0], None)


def _ada_kernel(c_ref, w_ref, b_ref, o_ref):
    o_ref[...] = jnp.dot(c_ref[...], w_ref[...].astype(BF16), preferred_element_type=F32) + b_ref[...]


def _ada(sc_all, ada_w, ada_b, tn=512):
    nl, k, n = ada_w.shape
    r = sc_all.shape[0]
    return pl.pallas_call(
        _ada_kernel,
        grid=(nl, n // tn),
        in_specs=[pl.BlockSpec((r, k), lambda l, j: (0, 0)),
                  pl.BlockSpec((None, k, tn), lambda l, j: (l, 0, j)),
                  pl.BlockSpec((None, 1, tn), lambda l, j: (l, 0, j))],
        out_specs=pl.BlockSpec((None, r, tn), lambda l, j: (l, 0, j)),
        out_shape=jax.ShapeDtypeStruct((nl, r, n), F32),
        compiler_params=_cparams(("parallel", "arbitrary")),
        name="ada",
    )(sc_all, ada_w, ada_b.astype(F32).reshape(nl, 1, n))


def _rope_tables(pos, d):
    half = d // 2
    inv = jnp.exp(jnp.arange(half, dtype=F32) * (-2.0 * math.log(ROPE_THETA) / d))
    ang = pos.astype(F32)[:, None] * inv[None, :]
    cos, sin = jnp.cos(ang), jnp.sin(ang)
    reps = LANES // d
    return (jnp.tile(jnp.concatenate([cos, cos], axis=-1), (1, reps)),
            jnp.tile(jnp.concatenate([-sin, sin], axis=-1), (1, reps)))


def _rope_lanes(x, cos, sin, d):
    if d == LANES:
        partner = pltpu.roll(x, LANES // 2, axis=1)
    else:
        lane = lax.broadcasted_iota(jnp.int32, x.shape, 1)
        half = d // 2
        partner = jnp.where((lane % d) < half, pltpu.roll(x, LANES - half, axis=1), pltpu.roll(x, half, axis=1))
    return x * cos + partner * sin


def _band_attn_kernel(*refs, patterns, d, kv_per_block, head_split, n_classes, seq, scale, has_sink):
    if has_sink:
        sink_ref, refs = refs[0], refs[1:]
    q_ref, k_ref, v_ref, cos_ref, sin_ref, o_ref, krot_ref = refs[:7]
    scratch = refs[7:]
    n_pat = len(patterns)
    if n_pat > 1:
        acc_scr, m_scr, l_scr, qrot_scr = scratch
    heads_per_unit = LANES // d
    n_units = q_ref.shape[1] // LANES
    units_per_kv = n_units // kv_per_block
    heads_per_kv = units_per_kv * heads_per_unit
    units_per_blk = units_per_kv // head_split
    n_rs = units_per_blk * heads_per_unit
    rows_q = n_rs * BAND
    npc = seq // n_classes
    log2e = math.log2(math.e)
    q_scale = scale * log2e

    chunk = min(256, seq)

    def rope_all(i, _):
        rows = pl.ds(pl.multiple_of(i * chunk, chunk), chunk)
        cos_c, sin_c = cos_ref[rows, :], sin_ref[rows, :]
        krot_ref[rows, :] = _rope_lanes(k_ref[rows, :], cos_c, sin_c, d)
        if n_pat > 1:
            for u in range(n_units):
                lanes = slice(u * LANES, (u + 1) * LANES)
                qrot_scr[rows, lanes] = _rope_lanes(q_ref[rows, lanes], cos_c, sin_c, d) * q_scale
        return 0

    lax.fori_loop(0, seq // chunk, rope_all, 0)

    lane_q = lax.broadcasted_iota(jnp.int32, (BAND, LANES), 1)

    def band_mask(s, first):
        qlen = BAND // s
        klen = qlen if first else 2 * qlen
        nk = s * klen
        rq = lax.broadcasted_iota(jnp.int32, (BAND, nk), 0)
        rk = lax.broadcasted_iota(jnp.int32, (BAND, nk), 1)
        lq, lk = qlen.bit_length() - 1, klen.bit_length() - 1
        rel = s * ((rq & (qlen - 1)) - (rk & (klen - 1))) + ((rq >> lq) - (rk >> lk)) + (0 if first else BAND)
        return (rel >= 0) & (rel <= BAND)

    def gather(ref, runs, lanes=None):
        rd = lambda st, ln: ref[pl.ds(st, ln), :] if lanes is None else ref[pl.ds(st, ln), lanes]
        parts = [rd(st, ln) for st, ln in runs]
        return parts[0] if len(parts) == 1 else jnp.concatenate(parts, axis=0)

    def scatter(store, runs, value):
        off = 0
        for st, ln in runs:
            store(pl.ds(st, ln), value[off:off + ln])
            off += ln

    def dup(a, j):
        if d == LANES:
            return a
        lane = lax.broadcasted_iota(jnp.int32, a.shape, 1)
        rolled = pltpu.roll(a, LANES // 2, axis=1)
        return jnp.where(lane < d, a, rolled) if j == 0 else jnp.where(lane < d, rolled, a)

    def block(p_idx, kvj, ug, q_runs, k_runs, valid):
        first, last = p_idx == 0, p_idx == n_pat - 1
        nk = sum(ln for _, ln in k_runs)
        if n_pat == 1:
            cosq, sinq = gather(cos_ref, q_runs), gather(sin_ref, q_runs)
        parts = []
        unit0 = kvj * units_per_kv + ug * units_per_blk
        hs0 = ug * n_rs
        for u in range(units_per_blk):
            lo = (unit0 + u) * LANES
            if n_pat > 1:
                xr = gather(qrot_scr, q_runs, slice(lo, lo + LANES))
            else:
                xr = _rope_lanes(gather(q_ref, q_runs, slice(lo, lo + LANES)), cosq, sinq, d) * q_scale
            if heads_per_unit == 1:
                parts.append(xr)
            else:
                parts += [jnp.where(lane_q < d, xr, 0.0), jnp.where(lane_q >= d, xr, 0.0)]
        qst = jnp.concatenate(parts, axis=0).astype(BF16)
        kk = dup(gather(krot_ref, k_runs), kvj).astype(BF16)
        vv = dup(gather(v_ref, k_runs), kvj).astype(BF16)
        vaug = jnp.concatenate([vv, jnp.ones((nk, LANES), BF16)], axis=1)
        s = lax.dot_general(qst, kk, (((1,), (1,)), ((), ())), preferred_element_type=F32)
        s = jnp.where(valid[None], s.reshape(n_rs, BAND, nk), NEG).reshape(rows_q, nk)
        m_blk = jnp.max(s, axis=-1, keepdims=True)
        if first:
            if has_sink:
                head0 = (pl.program_id(1) * kv_per_block + kvj) * heads_per_kv + hs0
                m_old = jnp.concatenate(
                    [jnp.full((BAND, LANES), sink_ref[head0 + h] * log2e, F32) for h in range(n_rs)], axis=0)
                l_old = 1.0
            else:
                m_old, l_old = jnp.full((rows_q, LANES), NEG, F32), 0.0
        else:
            m_old = jnp.concatenate([gather(m_scr.at[hs0 + h], q_runs) for h in range(n_rs)], axis=0)
        m_new = jnp.maximum(m_old, m_blk)
        p = jnp.exp2(s - jnp.concatenate([m_new] * (nk // LANES), axis=1))
        pv = jnp.dot(p.astype(BF16), vaug, preferred_element_type=F32)
        alpha = jnp.exp2(m_old - m_new)
        if first:
            acc = pv[:, :LANES]
            l_new = pv[:, LANES:] + alpha * l_old
        else:
            acc = alpha * jnp.concatenate([gather(acc_scr.at[hs0 + h], q_runs) for h in range(n_rs)], axis=0)
            acc = acc + pv[:, :LANES]
            l_new = alpha * jnp.concatenate([gather(l_scr.at[hs0 + h], q_runs) for h in range(n_rs)], axis=0)
            l_new = l_new + pv[:, LANES:]
        if last:
            out = acc / l_new
            for u in range(units_per_blk):
                lo = (unit0 + u) * LANES
                if heads_per_unit == 1:
                    piece = out[u * BAND:(u + 1) * BAND]
                else:
                    piece = jnp.where(lane_q < d, out[2 * u * BAND:(2 * u + 1) * BAND],
                                      out[(2 * u + 1) * BAND:(2 * u + 2) * BAND])

                def store_o(rows, val, lo=lo):
                    o_ref[rows, lo:lo + LANES] = val

                scatter(store_o, q_runs, piece.astype(o_ref.dtype))
        else:
            for h in range(n_rs):
                for scr, val in ((acc_scr, acc), (l_scr, l_new), (m_scr, m_new)):
                    def store_s(rows, v, scr=scr, h=hs0 + h):
                        scr[h, rows, :] = v

                    scatter(store_s, q_runs, val[h * BAND:(h + 1) * BAND])

    for p_idx, (window, r) in enumerate(patterns):
        assert window // r == BAND and n_classes % r == 0
        s = n_classes // r
        qlen = BAND // s
        nblk = npc // qlen
        valid_first, valid_rest = band_mask(s, True), band_mask(s, False)
        streams = [(kvj, ug, cr) for kvj in range(kv_per_block) for ug in range(head_split) for cr in range(r)]

        def one_block(kvj, ug, cr, m, first_blk, p_idx=p_idx, r=r, s=s, qlen=qlen,
                      valid_first=valid_first, valid_rest=valid_rest):
            base = [npc * (cr + r * a) for a in range(s)]
            if first_blk:
                q_runs = [(b0, qlen) for b0 in base]
                block(p_idx, kvj, ug, q_runs, q_runs, valid_first)
            else:
                off = pl.multiple_of(qlen * m, qlen)
                q_runs = [(b0 + off, qlen) for b0 in base]
                k_runs = [(b0 + off - qlen, 2 * qlen) for b0 in base]
                block(p_idx, kvj, ug, q_runs, k_runs, valid_rest)

        for kvj, ug, cr in streams:
            one_block(kvj, ug, cr, 0, True)
        if nblk > 1:
            unroll = 1
            if len(streams) < 3:
                unroll = next((u for u in (2, 3, 4, 5) if (nblk - 1) % u == 0 and len(streams) * u >= 3), 1)

            def body(it, _, streams=streams, unroll=unroll, one_block=one_block):
                for u in range(unroll):
                    for kvj, ug, cr in streams:
                        one_block(kvj, ug, cr, 1 + it * unroll + u, False)
                return 0

            lax.fori_loop(0, (nblk - 1) // unroll, body, 0)


def _class_major_positions(seq, n_classes):
    return jnp.arange(seq).reshape(seq // n_classes, n_classes).T.reshape(seq)


def _band_attention(y3, sinks, *, patterns, n_classes, d, n_q_heads, n_kv_heads, q_col, k_col, v_col, name,
                    head_split=1):
    nb, seq, _ = y3.shape
    kv_per_block = LANES // d
    n_kvb = n_kv_heads // kv_per_block
    qw = n_q_heads * d // n_kvb
    n_rs = n_q_heads // n_kv_heads
    cos, sin = _rope_tables(_class_major_positions(seq, n_classes), d)
    kern = functools.partial(_band_attn_kernel, patterns=patterns, d=d, kv_per_block=kv_per_block,
                             head_split=head_split, n_classes=n_classes, seq=seq, scale=d ** -0.5,
                             has_sink=sinks is not None)
    in_specs = [pl.BlockSpec((None, seq, qw), lambda b, j, *_: (b, 0, q_col // qw + j)),
                pl.BlockSpec((None, seq, LANES), lambda b, j, *_: (b, 0, k_col // LANES + j)),
                pl.BlockSpec((None, seq, LANES), lambda b, j, *_: (b, 0, v_col // LANES + j)),
                pl.BlockSpec((seq, LANES), lambda b, j, *_: (0, 0)),
                pl.BlockSpec((seq, LANES), lambda b, j, *_: (0, 0))]
    out_specs = [pl.BlockSpec((None, seq, qw), lambda b, j, *_: (b, 0, j)),
                 pl.BlockSpec((None, seq, LANES), lambda b, j, *_: (b, 0, j))]
    out_shape = [jax.ShapeDtypeStruct((nb, seq, n_q_heads * d), BF16),
                 jax.ShapeDtypeStruct((nb, seq, n_kv_heads * d), F32)]
    scratch = []
    if len(patterns) > 1:
        scratch = [pltpu.VMEM((n_rs, seq, LANES), F32) for _ in range(3)] + [pltpu.VMEM((seq, qw), F32)]
    assert q_col % qw == 0 and k_col % LANES == 0 and v_col % LANES == 0
    if sinks is None:
        gs = pl.GridSpec(grid=(nb, n_kvb), in_specs=in_specs, out_specs=out_specs, scratch_shapes=scratch)
        args = (y3, y3, y3, cos, sin)
    else:
        gs = pltpu.PrefetchScalarGridSpec(num_scalar_prefetch=1, grid=(nb, n_kvb), in_specs=in_specs,
                                          out_specs=out_specs, scratch_shapes=scratch)
        args = (sinks.astype(F32), y3, y3, y3, cos, sin)
    return pl.pallas_call(kern, grid_spec=gs, out_shape=out_shape,
                          compiler_params=_cparams(("parallel", "arbitrary")), name=name)(*args)


def _mlstm_kernel(q_ref, k_ref, v_ref, og_ref, igc_ref, lfc_ref, igr_ref, lfr_ref, hn_ref,
                  o_ref, c_ref, n_ref, m_ref, *, seq, chunk):
    L = chunk
    ri = lax.broadcasted_iota(jnp.int32, (L, L), 0)
    ci = lax.broadcasted_iota(jnp.int32, (L, L), 1)
    tril = ri >= ci
    tril_f = tril.astype(F32)
    triu_f = (ri <= ci).astype(F32)
    c_ref[...] = jnp.zeros_like(c_ref)
    n_ref[...] = jnp.zeros_like(n_ref)
    m_ref[...] = jnp.zeros_like(m_ref)
    hp = lax.Precision.HIGHEST

    def step(c, _):
        rows = pl.ds(pl.multiple_of(c * L, L), L)
        m = m_ref[...]
        qc = q_ref[rows, :] * (A_DK ** -0.5)
        kc = k_ref[rows, :]
        vc = v_ref[rows, :].astype(BF16)
        icol, fcol = igc_ref[rows, :], lfc_ref[rows, :]
        irow, frow = igr_ref[c], lfr_ref[c]
        bcol = jnp.dot(tril_f, jnp.broadcast_to(fcol, (L, LANES)), precision=hp,
                       preferred_element_type=F32)[:, :1]
        brow = jnp.dot(jnp.broadcast_to(frow, (8, L)), triu_f, precision=hp, preferred_element_type=F32)[:1, :]
        acol, arow = icol - bcol, irow - brow
        log_d = jnp.where(tril, bcol + arow, NEG)
        inter = bcol + m
        mt = jnp.maximum(inter, jnp.max(log_d, axis=-1, keepdims=True))
        dmat = jnp.exp(log_d - mt)
        wi = jnp.exp(inter - mt)
        qb = qc.astype(BF16)
        sc = lax.dot_general(qb, kc.astype(BF16), (((1,), (1,)), ((), ())), preferred_element_type=F32) * dmat
        cm = c_ref[...]
        nv = n_ref[...]
        num = (jnp.dot(sc.astype(BF16), vc, preferred_element_type=F32)
               + wi * jnp.dot(qb, cm.astype(BF16), preferred_element_type=F32))
        den = jnp.sum(sc, axis=-1, keepdims=True) + wi * jnp.sum(qc * nv, axis=-1, keepdims=True)
        hc = num / jnp.maximum(jnp.abs(den), jnp.exp(-mt))
        hn = hc * lax.rsqrt(jnp.mean(hc * hc, axis=-1, keepdims=True) + NORM_EPS) * hn_ref[...]
        o_ref[rows, :] = (hn * jax.nn.sigmoid(og_ref[rows, :])).astype(o_ref.dtype)
        blast = bcol[L - 1:L, :]
        m_new = jnp.maximum(blast + m, jnp.max(blast + arow, axis=-1, keepdims=True))
        w_c = jnp.exp(blast + m - m_new)
        kw = kc * jnp.exp(blast + acol - m_new)
        c_ref[...] = w_c * cm + lax.dot_general(kw.astype(BF16), vc, (((0,), (0,)), ((), ())),
                                                preferred_element_type=F32)
        n_ref[...] = w_c * nv + jnp.sum(kw, axis=0, keepdims=True)
        m_ref[...] = m_new
        return 0

    lax.fori_loop(0, seq // L, step, 0)


def _mlstm_prompt(y3, ig, lf, head_norm, *, q_col, k_col, v_col, o_col):
    nb, seq, _ = y3.shape
    L = A_CHUNK
    nc = seq // L
    igc = jnp.transpose(ig, (0, 2, 1))[..., None]
    lfc = jnp.transpose(lf, (0, 2, 1))[..., None]
    igr = igc.reshape(nb, A_HEADS, nc, 1, L)
    lfr = lfc.reshape(nb, A_HEADS, nc, 1, L)
    hn = head_norm.reshape(A_HEADS, 1, A_DV).astype(F32)
    col = lambda c0, w: (lambda b, h: (b, 0, c0 // w + h))
    gcol = pl.BlockSpec((None, None, seq, 1), lambda b, h: (b, h, 0, 0))
    grow = pl.BlockSpec((None, None, nc, 1, L), lambda b, h: (b, h, 0, 0, 0))
    return pl.pallas_call(
        functools.partial(_mlstm_kernel, seq=seq, chunk=L),
        grid=(nb, A_HEADS),
        in_specs=[pl.BlockSpec((None, seq, A_DK), col(q_col, A_DK)),
                  pl.BlockSpec((None, seq, A_DK), col(k_col, A_DK)),
                  pl.BlockSpec((None, seq, A_DV), col(v_col, A_DV)),
                  pl.BlockSpec((None, seq, A_DV), col(o_col, A_DV)),
                  gcol, gcol, grow, grow,
                  pl.BlockSpec((None, 1, A_DV), lambda b, h: (h, 0, 0))],
        out_specs=[pl.BlockSpec((None, seq, A_DV), lambda b, h: (b, 0, h)),
                   pl.BlockSpec((None, None, A_DK, A_DV), lambda b, h: (b, h, 0, 0)),
                   pl.BlockSpec((None, None, 1, A_DK), lambda b, h: (b, h, 0, 0)),
                   pl.BlockSpec((None, None, 1, 1), lambda b, h: (b, h, 0, 0))],
        out_shape=[jax.ShapeDtypeStruct((nb, seq, A_HEADS * A_DV), BF16),
                   jax.ShapeDtypeStruct((nb, A_HEADS, A_DK, A_DV), F32),
                   jax.ShapeDtypeStruct((nb, A_HEADS, 1, A_DK), F32),
                   jax.ShapeDtypeStruct((nb, A_HEADS, 1, 1), F32)],
        compiler_params=_cparams(("parallel", "arbitrary")),
        name="mlstm_prompt",
    )(y3, y3, y3, y3, igc, lfc, igr, lfr, hn)


def _mlstm_step_kernel(q_ref, k_ref, v_ref, og_ref, ig_ref, lf_ref, c0_ref, n0_ref, m0_ref, hn_ref,
                       o_ref, c1_ref, n1_ref, m1_ref):
    for h in range(A_HEADS):
        q = q_ref[h] * (A_DK ** -0.5)
        k = k_ref[h]
        v = v_ref[h]
        c0 = c0_ref[h]
        n0 = n0_ref[h]
        ig, lf, m0 = ig_ref[h], lf_ref[h], m0_ref[h]
        inter = lf + m0
        mt = jnp.maximum(inter, ig)
        dm = jnp.exp(ig - mt)
        wi = jnp.exp(inter - mt)
        sc = jnp.sum(q * k, axis=0, keepdims=True) * dm
        qc = jnp.sum(c0 * q, axis=0, keepdims=True)
        num = sc * v + wi * qc
        den = sc + wi * jnp.sum(q * n0, axis=0, keepdims=True)
        hc = num / jnp.maximum(jnp.abs(den), jnp.exp(-mt))
        hn = hc * lax.rsqrt(jnp.mean(hc * hc, axis=-1, keepdims=True) + NORM_EPS) * hn_ref[h]
        o_ref[h] = (hn * jax.nn.sigmoid(og_ref[h])).astype(o_ref.dtype)
        c1_ref[h] = wi * c0 + (k * dm) * v
        n1_ref[h] = wi * n0 + dm * k
        m1_ref[h] = mt


def _mlstm_sample(y, ig, lf, head_norm, c0, n0, m0):
    nb = y.shape[0]
    col = lambda a: a.reshape(nb, A_HEADS, A_DK, 1)
    row = lambda a: a.reshape(nb, A_HEADS, 1, A_DV)
    sca = lambda a: a.astype(F32).reshape(nb, A_HEADS, 1, 1)
    cspec = pl.BlockSpec((None, A_HEADS, A_DK, 1), lambda b: (b, 0, 0, 0))
    rspec = pl.BlockSpec((None, A_HEADS, 1, A_DV), lambda b: (b, 0, 0, 0))
    sspec = pl.BlockSpec((None, A_HEADS, 1, 1), lambda b: (b, 0, 0, 0))
    mspec = pl.BlockSpec((None, A_HEADS, A_DK, A_DV), lambda b: (b, 0, 0, 0))
    o, c1, n1, m1 = pl.pallas_call(
        _mlstm_step_kernel,
        grid=(nb,),
        in_specs=[cspec, cspec, rspec, rspec, sspec, sspec, mspec, cspec, sspec,
                  pl.BlockSpec((A_HEADS, 1, A_DV), lambda b: (0, 0, 0))],
        out_specs=[rspec, mspec, cspec, sspec],
        out_shape=[jax.ShapeDtypeStruct((nb, A_HEADS, 1, A_DV), F32),
                   jax.ShapeDtypeStruct((nb, A_HEADS, A_DK, A_DV), F32),
                   jax.ShapeDtypeStruct((nb, A_HEADS, A_DK, 1), F32),
                   jax.ShapeDtypeStruct((nb, A_HEADS, 1, 1), F32)],
        compiler_params=_cparams(("parallel",)),
        name="mlstm_step",
    )(col(y[:, A_Q:A_Q + A_HEADS * A_DK]), col(y[:, A_K:A_K + A_HEADS * A_DK]),
      row(y[:, A_V:A_V + A_HEADS * A_DV]), row(y[:, A_O:A_O + A_HEADS * A_DV]),
      sca(ig), sca(lf), c0.astype(F32), col(n0.astype(F32)), sca(m0),
      head_norm.astype(F32).reshape(A_HEADS, 1, A_DV))
    return (o.reshape(nb, A_HEADS * A_DV), c1, n1.reshape(nb, A_HEADS, A_DK), m1.reshape(nb, A_HEADS))


def _decode_attn_kernel(*refs, n_pat, d, group, scale, has_sink):
    q_ref, kn_ref, vn_ref, cos_ref, sin_ref = refs[:5]
    refs = refs[5:]
    if has_sink:
        sink_ref, refs = refs[0], refs[1:]
    k_refs, v_refs = refs[:n_pat], refs[n_pat:2 * n_pat]
    o_ref, knew_ref = refs[2 * n_pat:]
    n_seg = LANES // d
    cos, sin = cos_ref[...], sin_ref[...]
    kn = _rope_lanes(kn_ref[...], cos, sin, d)
    knew_ref[...] = kn
    vn = vn_ref[...]
    lane = lax.broadcasted_iota(jnp.int32, kn.shape, 1)

    def seg_sum(p):
        if n_seg == 1:
            return jnp.sum(p, axis=-1, keepdims=True)
        lane_p = lax.broadcasted_iota(jnp.int32, p.shape, p.ndim - 1)
        lo = jnp.sum(jnp.where(lane_p < d, p, 0.0), axis=-1, keepdims=True)
        hi = jnp.sum(jnp.where(lane_p >= d, p, 0.0), axis=-1, keepdims=True)
        return jnp.where(lane_p < d, lo, hi)

    for g in range(group):
        qg = _rope_lanes(q_ref[g], cos, sin, d)
        s_new = seg_sum(kn * qg) * scale
        s = [seg_sum(k_refs[p][...] * qg[None]) * scale for p in range(n_pat)]
        m = s_new
        for p in range(n_pat):
            m = jnp.maximum(m, jnp.max(s[p], axis=0))
        if has_sink:
            m = jnp.maximum(m, sink_ref[g])
        e_new = jnp.exp(s_new - m) * float(n_pat)
        l = e_new
        acc = e_new * vn
        for p in range(n_pat):
            e = jnp.exp(s[p] - m[None])
            l = l + jnp.sum(e, axis=0)
            acc = acc + jnp.sum(e * v_refs[p][...], axis=0)
        if has_sink:
            l = l + jnp.exp(sink_ref[g] - m)
        o_ref[g] = (acc / l).astype(o_ref.dtype)


def _decode_attention(q, k_new, v_new, k_cache, v_cache, sinks, pos, *, patterns, d, name):
    nb, n_heads, _ = q.shape
    n_kv, cache_len = k_cache.shape[2], k_cache.shape[1]
    group = n_heads // n_kv
    n_seg = LANES // d
    r_rows = n_kv // n_seg
    to_tiles = lambda a, inner: a.reshape(nb, r_rows, n_seg, inner, d).transpose(0, 3, 1, 2, 4).reshape(
        nb, inner, r_rows, LANES)
    qt = to_tiles(q, group)
    knt = k_new.reshape(nb, r_rows, LANES)
    vnt = v_new.reshape(nb, r_rows, LANES)
    cos, sin = _rope_tables(pos, d)
    kc = k_cache.reshape(nb, cache_len, r_rows, LANES)
    vc = v_cache.reshape(nb, cache_len, r_rows, LANES)
    tile = pl.BlockSpec((None, r_rows, LANES), lambda b: (b, 0, 0))
    one = pl.BlockSpec((1, LANES), lambda b: (0, 0))
    in_specs = [pl.BlockSpec((None, group, r_rows, LANES), lambda b: (b, 0, 0, 0)), tile, tile, one, one]
    args = [qt, knt, vnt, cos, sin]
    if sinks is not None:
        sk = jnp.broadcast_to(sinks.astype(F32).reshape(r_rows, n_seg, group, 1), (r_rows, n_seg, group, d))
        args.append(sk.transpose(2, 0, 1, 3).reshape(group, r_rows, LANES))
        in_specs.append(pl.BlockSpec((group, r_rows, LANES), lambda b: (0, 0, 0)))
    views = []
    for window, r in patterns:
        assert window // r == BAND and cache_len % (BAND * r) == 0
        blk = cache_len // r // BAND - 1
        views.append(pl.BlockSpec((None, BAND, None, r_rows, LANES), lambda b, blk=blk: (b, blk, 0, 0, 0)))
    in_specs += views + views
    args += [kc.reshape(nb, cache_len // r, r, r_rows, LANES) for _, r in patterns]
    args += [vc.reshape(nb, cache_len // r, r, r_rows, LANES) for _, r in patterns]
    o, kn = pl.pallas_call(
        functools.partial(_decode_attn_kernel, n_pat=len(patterns), d=d, group=group, scale=d ** -0.5,
                          has_sink=sinks is not None),
        grid=(nb,),
        in_specs=in_specs,
        out_specs=[pl.BlockSpec((None, group, r_rows, LANES), lambda b: (b, 0, 0, 0)), tile],
        out_shape=[jax.ShapeDtypeStruct((nb, group, r_rows, LANES), F32),
                   jax.ShapeDtypeStruct((nb, r_rows, LANES), F32)],
        compiler_params=_cparams(("parallel",)),
        name=name,
    )(*args)
    o = o.reshape(nb, group, r_rows, n_seg, d).transpose(0, 2, 3, 1, 4).reshape(nb, n_heads * d)
    return o, kn.reshape(nb, n_kv, d)


def _shift_kernel(c_ref, new_ref, o_ref, *, length, chunk):
    n_full = (length - 1) // chunk

    def body(i, _):
        o_ref[pl.ds(i * chunk, chunk)] = c_ref[pl.ds(i * chunk + 1, chunk)]
        return 0

    lax.fori_loop(0, n_full, body, 0)
    done = n_full * chunk
    if done < length - 1:
        o_ref[done:length - 1] = c_ref[done + 1:length]
    o_ref[length - 1] = new_ref[...]


def _shift_cache(cache, new, name):
    nb, length, n_kv, d = cache.shape
    return pl.pallas_call(
        functools.partial(_shift_kernel, length=length, chunk=min(64, length - 1)),
        grid=(nb,),
        in_specs=[pl.BlockSpec((None, length, n_kv, d), lambda b: (b, 0, 0, 0)),
                  pl.BlockSpec((None, n_kv, d), lambda b: (b, 0, 0))],
        out_specs=pl.BlockSpec((None, length, n_kv, d), lambda b: (b, 0, 0, 0)),
        out_shape=jax.ShapeDtypeStruct(cache.shape, cache.dtype),
        compiler_params=_cparams(("parallel",)),
        name=name,
    )(cache, new.astype(cache.dtype))


N_IN_EVEN = 2 * A_HEADS * A_DK + 2 * A_HEADS * A_DV + 2 * A_HEADS + (B_HEADS + 2 * B_KV_HEADS) * B_HEAD_DIM
EVEN_IN = N_IN_EVEN // 512 * 512
A_Q, A_K, A_V, A_O = 0, 1024, 2048, 4096
GATE_COL = 6144
B_COL = GATE_COL + 2 * A_HEADS
B_Q, B_K, B_V = 0, 2048, 2304
C_Q, C_K, C_V = 0, 4096, 5120
SWA_PATTERN = ((B_WINDOW, 1),)


def _even_projection(hp, hs, w_in, j):
    ys = _matmul2([(hp, hs, w_in, j, 0)], EVEN_IN, F32, 1024, 512, "even_in")
    n_tail = N_IN_EVEN - EVEN_IN
    w_tail = jnp.pad(w_in[j, :, EVEN_IN:], ((0, 0), (0, LANES - n_tail)))[None]
    tails = _matmul2([(hp, hs, w_tail, 0, 0)], LANES, F32, 1024, LANES, "even_in_tail")
    return [(y, y[:, GATE_COL:B_COL], jnp.concatenate([y[:, B_COL:], t[:, :n_tail]], axis=-1))
            for y, t in zip(ys, tails)]


def _gates(yg, gate_b):
    gb = gate_b.astype(F32)
    ig = yg[..., :A_HEADS] + gb[:A_HEADS]
    lf = jax.nn.log_sigmoid(yg[..., A_HEADS:] + gb[A_HEADS:])
    return ig, lf


def _even_mixer_prompt(proj, nb, seq, gate_b, head_norm, sinks):
    y, yg, yb = proj
    y3, yb3 = y.reshape(nb, seq, -1), yb.reshape(nb, seq, -1)
    ig, lf = _gates(yg.reshape(nb, seq, -1), gate_b)
    ha, c1, n1, m1 = _mlstm_prompt(y3, ig, lf, head_norm, q_col=A_Q, k_col=A_K, v_col=A_V, o_col=A_O)
    ob, kb = _band_attention(yb3, sinks, patterns=SWA_PATTERN, n_classes=1, d=B_HEAD_DIM, n_q_heads=B_HEADS,
                             n_kv_heads=B_KV_HEADS, q_col=B_Q, k_col=B_K, v_col=B_V, name="swa_prompt",
                             head_split=2)
    kbuf = kb[:, -B_WINDOW:].reshape(nb, -1, B_KV_HEADS, B_HEAD_DIM)
    vbuf = yb3[:, -B_WINDOW:, B_V:].reshape(nb, -1, B_KV_HEADS, B_HEAD_DIM)
    state = (c1, n1.reshape(nb, A_HEADS, A_DK), m1.reshape(nb, A_HEADS), kbuf, vbuf)
    return ha.reshape(nb * seq, -1), ob.reshape(nb * seq, -1), state


def _even_mixer_sample(proj, gate_b, head_norm, sinks, state):
    y, yg, yb = proj
    nb = y.shape[0]
    ig, lf = _gates(yg, gate_b)
    c0, n0, m0, k_old, v_old = state
    ha, c1, n1, m1 = _mlstm_sample(y, ig, lf, head_norm, c0, n0, m0)
    q = yb[:, B_Q:B_K].reshape(nb, B_HEADS, B_HEAD_DIM)
    k_new = yb[:, B_K:B_V].reshape(nb, B_KV_HEADS, B_HEAD_DIM)
    v_new = yb[:, B_V:].reshape(nb, B_KV_HEADS, B_HEAD_DIM)
    ob, k_rot = _decode_attention(q, k_new, v_new, k_old, v_old, sinks, PAST_LEN + jnp.arange(1),
                                  patterns=SWA_PATTERN, d=B_HEAD_DIM, name="swa_step")
    state = (c1, n1, m1, _shift_cache(k_old, k_rot, "swa_k_shift"), _shift_cache(v_old, v_new, "swa_v_shift"))
    return ha.astype(BF16), ob.astype(BF16), state


def _even_mixer(hp, hs, nb, seq, j, w_in, gate_b, head_norm, sinks, w_out, state):
    proj_p, proj_s = _even_projection(hp, hs, w_in, j)
    ha_p, ob_p, st_p = _even_mixer_prompt(proj_p, nb, seq, gate_b, head_norm, sinks)
    ha_s, ob_s, st_s = _even_mixer_sample(proj_s, gate_b, head_norm, sinks, state)
    outs = _matmul2([(ha_p, ha_s, w_out, j, 0), (ob_p, ob_s, w_out, j, 1)], D_MODEL, F32, 1024, 512, "even_out")
    return outs, (st_p, st_s)


def _odd_mixer(hp, hs, nb, seq, j, w_in, w_out, state):
    out_p, st_p = _odd_mixer_prompt(hp, nb, seq, j, w_in, w_out)
    out_s, st_s = _odd_mixer_sample(hs, j, w_in, w_out, state)
    return (out_p, out_s), (st_p, st_s)


C_CLASSES = max(r for _, r in C_PATTERNS)


def _odd_mixer_prompt(h2, nb, seq, j, w_in, w_out):
    y3 = _matmul_regroup(h2, w_in, j, w_in.shape[2], nb, C_CLASSES, True, 1024, 512, "odd_in").reshape(nb, seq, -1)
    o, kr = _band_attention(y3, None, patterns=C_PATTERNS, n_classes=C_CLASSES, d=C_HEAD_DIM, n_q_heads=C_HEADS,
                            n_kv_heads=C_KV_HEADS, q_col=C_Q, k_col=C_K, v_col=C_V, name="dil_prompt")
    out = _matmul_regroup(o.reshape(nb, C_CLASSES, seq // C_CLASSES, -1), w_out, j, D_MODEL, nb, C_CLASSES, False,
                          1024, 512, "odd_out")
    natural = lambda a: a.reshape(nb, C_CLASSES, seq // C_CLASSES, C_KV_HEADS, C_HEAD_DIM).transpose(
        0, 2, 1, 3, 4).reshape(nb, seq, C_KV_HEADS, C_HEAD_DIM)
    return out, (natural(kr), natural(y3[:, :, C_V:]))


def _odd_mixer_sample(h2, j, w_in, w_out, state):
    nb = h2.shape[0]
    y = _matmul([(h2, w_in, j, 0)], w_in.shape[2], F32, nb, 512, "odd_in_s")
    q = y[:, C_Q:C_K].reshape(nb, C_HEADS, C_HEAD_DIM)
    k_new = y[:, C_K:C_V].reshape(nb, C_KV_HEADS, C_HEAD_DIM)
    v_new = y[:, C_V:].reshape(nb, C_KV_HEADS, C_HEAD_DIM)
    k_old, v_old = state
    o, k_rot = _decode_attention(q, k_new, v_new, k_old, v_old, None, PAST_LEN + jnp.arange(1),
                                 patterns=C_PATTERNS, d=C_HEAD_DIM, name="dil_step")
    out = _matmul([(o.astype(BF16), w_out, j, 0)], D_MODEL, F32, nb, 512, "odd_out_s")
    return out, (_shift_cache(k_old, k_rot, "dil_k_shift"), _shift_cache(v_old, v_new, "dil_v_shift"))


def _trunk(xs, rows, mixers, norms, ffn):
    g_mix_pre, g_mix_post, g_ffn_pre, g_ffn_post = norms
    wg, wu, wd = ffn
    xs = list(xs)
    hs = [_prenorm(r, x, g_mix_pre[0], 0, MOD_SC1, MOD_SH1) for r, x in zip(rows, xs)]
    states = []
    for li in range(DEPTH):
        ys, state = mixers[li](*hs)
        states.append(state)
        for g in range(2):
            xs[g], hs[g] = _postnorm(rows[g], xs[g], ys[g], g_mix_post[li], li, MOD_GT1,
                                     (g_ffn_pre[li], li, MOD_SC2, MOD_SH2))
        acts = _matmul2([(hs[0], hs[1], (wg, wu), li, 0)], wg.shape[2], BF16, 1024, 256, "swiglu", swiglu=True)
        fs = _matmul2([(acts[0], acts[1], wd, li, 0)], D_MODEL, F32, 512, 512, "ffn_down")
        nxt = (g_mix_pre[li + 1], li + 1, MOD_SC1, MOD_SH1) if li + 1 < DEPTH else None
        for g in range(2):
            xs[g], hs[g] = _postnorm(rows[g], xs[g], fs[g], g_ffn_post[li], li, MOD_GT2, nxt)
    return xs, states


def kernel(x_prompt, x_sample, state_l0_mlstm_c, state_l0_mlstm_n, state_l0_mlstm_m, cache_l0_swa_k, cache_l0_swa_v, cache_l1_dil_k, cache_l1_dil_v, state_l2_mlstm_c, state_l2_mlstm_n, state_l2_mlstm_m, cache_l2_swa_k, cache_l2_swa_v, cache_l3_dil_k, cache_l3_dil_v, c_prompt, c_sample, ada_w, ada_b, norm_mix_pre, norm_mix_post, norm_ffn_pre, norm_ffn_post, even_w_in, even_gate_b, even_head_norm, even_sinks, even_w_out, odd_w_in, odd_w_out, ffn_w_gate, ffn_w_up, ffn_w_down):
    past = {0: (state_l0_mlstm_c, state_l0_mlstm_n, state_l0_mlstm_m, cache_l0_swa_k, cache_l0_swa_v),
            1: (cache_l1_dil_k, cache_l1_dil_v),
            2: (state_l2_mlstm_c, state_l2_mlstm_n, state_l2_mlstm_m, cache_l2_swa_k, cache_l2_swa_v),
            3: (cache_l3_dil_k, cache_l3_dil_v)}
    nbp, seq, dm = x_prompt.shape
    nbs = x_sample.shape[0]
    c_all = jnp.concatenate([c_sample, c_prompt], axis=0)
    pad = (-c_all.shape[0]) % 16
    sc_all = jnp.pad(jax.nn.silu(c_all), ((0, pad), (0, 0))).astype(BF16)
    mod_all = _ada(sc_all, ada_w, ada_b)
    mixers = []
    for li in range(DEPTH):
        j = li // 2
        if li % 2 == 0:
            mw = (j, even_w_in, even_gate_b[j], even_head_norm[j], even_sinks[j], even_w_out, past[li])
            mixers.append(lambda hp, hs, mw=mw: _even_mixer(hp, hs, nbp, seq, *mw))
        else:
            mw = (j, odd_w_in, odd_w_out, past[li])
            mixers.append(lambda hp, hs, mw=mw: _odd_mixer(hp, hs, nbp, seq, *mw))
    norms = (norm_mix_pre, norm_mix_post, norm_ffn_pre, norm_ffn_post)
    ffn = (ffn_w_gate, ffn_w_up, ffn_w_down.astype(BF16))
    rows = (_Rows(mod_all, nbp * seq, seq, nbs, 256), _Rows(mod_all, nbs, 1, 0, nbs))
    (y_p, y_s), states = _trunk((x_prompt.reshape(nbp * seq, dm), x_sample.reshape(nbs, dm)), rows, mixers,
                                norms, ffn)
    y_p, y_s = y_p.reshape(x_prompt.shape), y_s.reshape(x_sample.shape)
    new_p, new_s = [s[0] for s in states], [s[1] for s in states]
    (p0c, p0n, p0m, p0k, p0v), (p1k, p1v), (p2c, p2n, p2m, p2k, p2v), (p3k, p3v) = new_p
    (s0c, s0n, s0m, s0k, s0v), (s1k, s1v), (s2c, s2n, s2m, s2k, s2v), (s3k, s3v) = new_s
    return (y_p, y_s,
            p0c, p0n, p0m, p0k, p0v, p1k, p1v, p2c, p2n, p2m, p2k, p2v, p3k, p3v,
            s0c, s0n, s0m, s0k, s0v, s1k, s1v, s2c, s2n, s2m, s2k, s2v, s3k, s3v)
```

```python
import functools
import math

import jax
import jax.numpy as jnp
import numpy as np
from jax import lax
from jax.experimental import pallas as pl
from jax.experimental.pallas import tpu as pltpu

F32 = jnp.float32
BF16 = jnp.bfloat16

D_MODEL = 4096
DEPTH = 4
PAST_LEN = 8192
A_HEADS, A_DK, A_DV, A_CHUNK = 4, 256, 512, 64
B_HEADS, B_KV_HEADS, B_HEAD_DIM, B_WINDOW = 32, 4, 64, 128
C_HEADS, C_KV_HEADS, C_HEAD_DIM = 32, 8, 128
C_PATTERNS = ((128, 1), (512, 4), (2048, 16))
BAND = 128
ROPE_THETA = 10000.0
NORM_EPS = 1e-6
LANES = 128
NEG = -1e30
VMEM_LIMIT = 56 * 1024 * 1024


def _cparams(sem):
    return pltpu.CompilerParams(dimension_semantics=sem, vmem_limit_bytes=VMEM_LIMIT)


def _mm_kernel(*refs, n_in):
    o_ref = refs[-1]
    acc = None
    for i in range(n_in):
        x = refs[2 * i][...].astype(BF16)
        w = refs[2 * i + 1][...].astype(BF16)
        d = jnp.dot(x, w, preferred_element_type=F32)
        acc = d if acc is None else acc + d
    o_ref[...] = acc.astype(o_ref.dtype)


def _w_spec(w, layer, kblk, k, tn):
    if w.ndim == 3:
        return pl.BlockSpec((None, k, tn), lambda i, j: (layer, kblk, j))
    return pl.BlockSpec((k, tn), lambda i, j: (kblk, j))


def _matmul(pairs, n, out_dtype, tm, tn, name):
    m = pairs[0][0].shape[0]
    tm = min(tm, m)
    tn = min(tn, n)
    assert m % tm == 0 and n % tn == 0, (m, n, tm, tn)
    in_specs, args = [], []
    for x, w, layer, kblk in pairs:
        k = x.shape[1]
        in_specs += [pl.BlockSpec((tm, k), lambda i, j: (i, 0)), _w_spec(w, layer, kblk, k, tn)]
        args += [x, w]
    return pl.pallas_call(
        functools.partial(_mm_kernel, n_in=len(pairs)),
        grid=(m // tm, n // tn),
        in_specs=in_specs,
        out_specs=pl.BlockSpec((tm, tn), lambda i, j: (i, j)),
        out_shape=jax.ShapeDtypeStruct((m, n), out_dtype),
        compiler_params=_cparams(("parallel", "arbitrary")),
        name=name,
    )(*args)


def _mm2_kernel(*refs, n_in, swiglu):
    n_w = 2 if swiglu else 1
    per = 2 + n_w
    ins = refs[:n_in * per]
    op_ref, os_ref = refs[n_in * per:]
    ws = [ins[(t // n_w) * per + 2 + t % n_w][...].astype(BF16) for t in range(n_in * n_w)]

    def compute(xsel, o_ref):
        accs = [None] * n_w
        for p in range(n_in):
            x = ins[p * per + xsel][...]
            for t in range(n_w):
                dt = jnp.dot(x, ws[p * n_w + t], preferred_element_type=F32)
                accs[t] = dt if accs[t] is None else accs[t] + dt
        r = accs[0] * jax.nn.sigmoid(accs[0]) * accs[1] if swiglu else accs[0]
        o_ref[...] = r.astype(o_ref.dtype)

    compute(0, op_ref)

    @pl.when(pl.program_id(0) == 0)
    def _():
        compute(1, os_ref)


def _matmul2(pairs, n, out_dtype, tm, tn, name, swiglu=False):
    m, ms = pairs[0][0].shape[0], pairs[0][1].shape[0]
    assert m % tm == 0 and n % tn == 0, (m, n, tm, tn)
    n_j = n // tn
    in_specs, args = [], []
    for xp, xs, ws, layer, kblk in pairs:
        k = xp.shape[1]
        in_specs += [pl.BlockSpec((tm, k), lambda i, j: (i, 0)), pl.BlockSpec((ms, k), lambda i, j: (0, 0))]
        args += [xp, xs]
        for w in (ws if swiglu else (ws,)):
            in_specs.append(pl.BlockSpec((None, k, tn), lambda i, j, layer=layer, kblk=kblk: (layer, kblk, j)))
            args.append(w)
    return pl.pallas_call(
        functools.partial(_mm2_kernel, n_in=len(pairs), swiglu=swiglu),
        grid=(m // tm, n_j),
        in_specs=in_specs,
        out_specs=[pl.BlockSpec((tm, tn), lambda i, j: (i, j)),
                   pl.BlockSpec((ms, tn), lambda i, j: (0, jnp.where(i == 0, j, n_j - 1)))],
        out_shape=[jax.ShapeDtypeStruct((m, n), out_dtype), jax.ShapeDtypeStruct((ms, n), out_dtype)],
        compiler_params=_cparams(("arbitrary", "arbitrary")),
        name=name,
    )(*args)


def _digit_swap(n):
    p = lax.broadcasted_iota(jnp.int32, (n * n, n * n), 0)
    q = lax.broadcasted_iota(jnp.int32, (n * n, n * n), 1)
    shift = n.bit_length() - 1
    return (q == ((p & (n - 1)) << shift) + (p >> shift)).astype(BF16)


def _mm_regroup_kernel(x_ref, w_ref, o_ref, xp_scr, *, n_cls, to_classes):
    blk = n_cls * n_cls
    tm = xp_scr.shape[0]

    @pl.when(pl.program_id(1) == 0)
    def _():
        perm = _digit_swap(n_cls)
        for k in range(tm // blk):
            if to_classes:
                xs = x_ref[k * blk:(k + 1) * blk, :]
            else:
                xs = jnp.concatenate([x_ref[c, k * n_cls:(k + 1) * n_cls, :] for c in range(n_cls)], axis=0)
            xp_scr[k * blk:(k + 1) * blk, :] = jnp.dot(perm, xs, preferred_element_type=F32).astype(BF16)

    r = jnp.dot(xp_scr[...], w_ref[...].astype(BF16), preferred_element_type=F32)
    if to_classes:
        for k in range(tm // blk):
            for c in range(n_cls):
                o_ref[c, k * n_cls:(k + 1) * n_cls, :] = r[k * blk + c * n_cls:k * blk + (c + 1) * n_cls]
    else:
        o_ref[...] = r


def _matmul_regroup(x, w, layer, n, nb, n_cls, to_classes, tm, tn, name):
    k = x.shape[-1]
    seq = x.shape[0] // nb if to_classes else n_cls * x.shape[2]
    tiles = seq // tm
    assert seq % tm == 0 and tm % (n_cls * n_cls) == 0 and n % tn == 0
    cls_block = lambda cols: (None, n_cls, tm // n_cls, cols)
    if to_classes:
        x_spec = pl.BlockSpec((tm, k), lambda i, j: (i, 0))
        o_spec = pl.BlockSpec(cls_block(tn), lambda i, j: (i // tiles, 0, i % tiles, j))
        o_shape = (nb, n_cls, seq // n_cls, n)
    else:
        x_spec = pl.BlockSpec(cls_block(k), lambda i, j: (i // tiles, 0, i % tiles, 0))
        o_spec = pl.BlockSpec((tm, tn), lambda i, j: (i, j))
        o_shape = (nb * seq, n)
    return pl.pallas_call(
        functools.partial(_mm_regroup_kernel, n_cls=n_cls, to_classes=to_classes),
        grid=(nb * tiles, n // tn),
        in_specs=[x_spec, _w_spec(w, layer, 0, k, tn)],
        out_specs=o_spec,
        out_shape=jax.ShapeDtypeStruct(o_shape, F32),
        scratch_shapes=[pltpu.VMEM((tm, k), BF16)],
        compiler_params=_cparams(("parallel", "arbitrary")),
        name=name,
    )(x, w)


MOD_SH1, MOD_SC1, MOD_GT1, MOD_SH2, MOD_SC2, MOD_GT2 = range(6)


def _rms_rows(x, g):
    return x * lax.rsqrt(jnp.mean(x * x, axis=-1, keepdims=True) + NORM_EPS) * g


def _prenorm_kernel(x_ref, g_ref, sc_ref, sh_ref, h_ref):
    h_ref[...] = (_rms_rows(x_ref[...], g_ref[...]) * (1.0 + sc_ref[...]) + sh_ref[...]).astype(h_ref.dtype)


def _postnorm_kernel(*refs, has_next):
    x_ref, y_ref, gt_ref, gpost_ref = refs[:4]
    xn = x_ref[...] + gt_ref[...] * _rms_rows(y_ref[...], gpost_ref[...])
    if has_next:
        gpre_ref, sc_ref, sh_ref, xo_ref, h_ref = refs[4:]
        h_ref[...] = (_rms_rows(xn, gpre_ref[...]) * (1.0 + sc_ref[...]) + sh_ref[...]).astype(h_ref.dtype)
    else:
        xo_ref = refs[4]
    xo_ref[...] = xn


class _Rows:
    def __init__(self, mod_all, n_rows, rows_per_batch, mod_row0, tm):
        self.dm = mod_all.shape[-1] // 6
        self.tm = min(tm, n_rows)
        self.n_rows = n_rows
        self.per_token = rows_per_batch == 1
        if self.per_token:
            self.mod = mod_all
            assert mod_row0 % self.tm == 0 and n_rows == self.tm
        else:
            self.mod = mod_all.reshape(mod_all.shape[0], mod_all.shape[1], 1, mod_all.shape[2])
            assert rows_per_batch % self.tm == 0
        self.tiles_per_batch = max(rows_per_batch // self.tm, 1)
        self.mod_row0 = mod_row0

    def rows(self):
        return pl.BlockSpec((self.tm, self.dm), lambda i: (i, 0))

    def vec(self, layer, chunk):
        if self.per_token:
            return pl.BlockSpec((None, self.tm, self.dm), lambda i: (layer, self.mod_row0 // self.tm, chunk))
        return pl.BlockSpec((None, None, 1, self.dm),
                            lambda i: (layer, self.mod_row0 + i // self.tiles_per_batch, 0, chunk))


def _gain_spec(dm):
    return pl.BlockSpec((None, 1, dm), lambda i: (0, 0, 0))


def _prenorm(rows, x2, gain, layer, sc_chunk, sh_chunk):
    dm = rows.dm
    return pl.pallas_call(
        _prenorm_kernel,
        grid=(rows.n_rows // rows.tm,),
        in_specs=[rows.rows(), _gain_spec(dm), rows.vec(layer, sc_chunk), rows.vec(layer, sh_chunk)],
        out_specs=rows.rows(),
        out_shape=jax.ShapeDtypeStruct((rows.n_rows, dm), BF16),
        compiler_params=_cparams(("parallel",)),
        name="prenorm",
    )(x2, gain.astype(F32).reshape(1, 1, dm), rows.mod, rows.mod)


def _postnorm(rows, x2, y2, gain_post, layer, gt_chunk, nxt):
    dm = rows.dm
    in_specs = [rows.rows(), rows.rows(), rows.vec(layer, gt_chunk), _gain_spec(dm)]
    args = [x2, y2, rows.mod, gain_post.astype(F32).reshape(1, 1, dm)]
    out_specs = [rows.rows()]
    out_shape = [jax.ShapeDtypeStruct((rows.n_rows, dm), F32)]
    if nxt is not None:
        gain_pre, nl, sc_chunk, sh_chunk = nxt
        in_specs += [_gain_spec(dm), rows.vec(nl, sc_chunk), rows.vec(nl, sh_chunk)]
        args += [gain_pre.astype(F32).reshape(1, 1, dm), rows.mod, rows.mod]
        out_specs.append(rows.rows())
        out_shape.append(jax.ShapeDtypeStruct((rows.n_rows, dm), BF16))
    res = pl.pallas_call(
        functools.partial(_postnorm_kernel, has_next=nxt is not None),
        grid=(rows.n_rows // rows.tm,),
        in_specs=in_specs, out_specs=out_specs, out_shape=out_shape,
        compiler_params=_cparams(("parallel",)),
        name="postnorm",
    )(*args)
    return (res[0], res[1]) if nxt is not None else (res[0], None)


def _ada_kernel(c_ref, w_ref, b_ref, o_ref):
    o_ref[...] = jnp.dot(c_ref[...], w_ref[...].astype(BF16), preferred_element_type=F32) + b_ref[...]


def _ada(sc_all, ada_w, ada_b, tn=512):
    nl, k, n = ada_w.shape
    r = sc_all.shape[0]
    return pl.pallas_call(
        _ada_kernel,
        grid=(nl, n // tn),
        in_specs=[pl.BlockSpec((r, k), lambda l, j: (0, 0)),
                  pl.BlockSpec((None, k, tn), lambda l, j: (l, 0, j)),
                  pl.BlockSpec((None, 1, tn), lambda l, j: (l, 0, j))],
        out_specs=pl.BlockSpec((None, r, tn), lambda l, j: (l, 0, j)),
        out_shape=jax.ShapeDtypeStruct((nl, r, n), F32),
        compiler_params=_cparams(("parallel", "arbitrary")),
        name="ada",
    )(sc_all, ada_w, ada_b.astype(F32).reshape(nl, 1, n))


def _rope_tables(pos, d):
    half = d // 2
    inv = jnp.exp(jnp.arange(half, dtype=F32) * (-2.0 * math.log(ROPE_THETA) / d))
    ang = pos.astype(F32)[:, None] * inv[None, :]
    cos, sin = jnp.cos(ang), jnp.sin(ang)
    reps = LANES // d
    return (jnp.tile(jnp.concatenate([cos, cos], axis=-1), (1, reps)),
            jnp.tile(jnp.concatenate([-sin, sin], axis=-1), (1, reps)))


def _rope_lanes(x, cos, sin, d):
    if d == LANES:
        partner = pltpu.roll(x, LANES // 2, axis=1)
    else:
        lane = lax.broadcasted_iota(jnp.int32, x.shape, 1)
        half = d // 2
        partner = jnp.where((lane % d) < half, pltpu.roll(x, LANES - half, axis=1), pltpu.roll(x, half, axis=1))
    return x * cos + partner * sin


def _band_attn_kernel(*refs, patterns, d, kv_per_block, head_split, n_classes, seq, scale, has_sink):
    if has_sink:
        sink_ref, refs = refs[0], refs[1:]
    q_ref, k_ref, v_ref, cos_ref, sin_ref, o_ref, krot_ref = refs[:7]
    scratch = refs[7:]
    n_pat = len(patterns)
    if n_pat > 1:
        acc_scr, m_scr, l_scr, qrot_scr = scratch
    heads_per_unit = LANES // d
    n_units = q_ref.shape[1] // LANES
    units_per_kv = n_units // kv_per_block
    heads_per_kv = units_per_kv * heads_per_unit
    units_per_blk = units_per_kv // head_split
    n_rs = units_per_blk * heads_per_unit
    rows_q = n_rs * BAND
    npc = seq // n_classes
    log2e = math.log2(math.e)
    q_scale = scale * log2e

    chunk = min(256, seq)

    def rope_all(i, _):
        rows = pl.ds(pl.multiple_of(i * chunk, chunk), chunk)
        cos_c, sin_c = cos_ref[rows, :], sin_ref[rows, :]
        krot_ref[rows, :] = _rope_lanes(k_ref[rows, :], cos_c, sin_c, d)
        if n_pat > 1:
            for u in range(n_units):
                lanes = slice(u * LANES, (u + 1) * LANES)
                qrot_scr[rows, lanes] = _rope_lanes(q_ref[rows, lanes], cos_c, sin_c, d) * q_scale
        return 0

    lax.fori_loop(0, seq // chunk, rope_all, 0)

    lane_q = lax.broadcasted_iota(jnp.int32, (BAND, LANES), 1)

    def band_mask(s, first):
        qlen = BAND // s
        klen = qlen if first else 2 * qlen
        nk = s * klen
        rq = lax.broadcasted_iota(jnp.int32, (BAND, nk), 0)
        rk = lax.broadcasted_iota(jnp.int32, (BAND, nk), 1)
        lq, lk = qlen.bit_length() - 1, klen.bit_length() - 1
        rel = s * ((rq & (qlen - 1)) - (rk & (klen - 1))) + ((rq >> lq) - (rk >> lk)) + (0 if first else BAND)
        return (rel >= 0) & (rel <= BAND)

    def gather(ref, runs, lanes=None):
        rd = lambda st, ln: ref[pl.ds(st, ln), :] if lanes is None else ref[pl.ds(st, ln), lanes]
        parts = [rd(st, ln) for st, ln in runs]
        return parts[0] if len(parts) == 1 else jnp.concatenate(parts, axis=0)

    def scatter(store, runs, value):
        off = 0
        for st, ln in runs:
            store(pl.ds(st, ln), value[off:off + ln])
            off += ln

    def dup(a, j):
        if d == LANES:
            return a
        lane = lax.broadcasted_iota(jnp.int32, a.shape, 1)
        rolled = pltpu.roll(a, LANES // 2, axis=1)
        return jnp.where(lane < d, a, rolled) if j == 0 else jnp.where(lane < d, rolled, a)

    def block(p_idx, kvj, ug, q_runs, k_runs, valid):
        first, last = p_idx == 0, p_idx == n_pat - 1
        nk = sum(ln for _, ln in k_runs)
        if n_pat == 1:
            cosq, sinq = gather(cos_ref, q_runs), gather(sin_ref, q_runs)
        parts = []
        unit0 = kvj * units_per_kv + ug * units_per_blk
        hs0 = ug * n_rs
        for u in range(units_per_blk):
            lo = (unit0 + u) * LANES
            if n_pat > 1:
                xr = gather(qrot_scr, q_runs, slice(lo, lo + LANES))
            else:
                xr = _rope_lanes(gather(q_ref, q_runs, slice(lo, lo + LANES)), cosq, sinq, d) * q_scale
            if heads_per_unit == 1:
                parts.append(xr)
            else:
                parts += [jnp.where(lane_q < d, xr, 0.0), jnp.where(lane_q >= d, xr, 0.0)]
        qst = jnp.concatenate(parts, axis=0).astype(BF16)
        kk = dup(gather(krot_ref, k_runs), kvj).astype(BF16)
        vv = dup(gather(v_ref, k_runs), kvj).astype(BF16)
        vaug = jnp.concatenate([vv, jnp.ones((nk, LANES), BF16)], axis=1)
        s = lax.dot_general(qst, kk, (((1,), (1,)), ((), ())), preferred_element_type=F32)
        s = jnp.where(valid[None], s.reshape(n_rs, BAND, nk), NEG).reshape(rows_q, nk)
        m_blk = jnp.max(s, axis=-1, keepdims=True)
        if first:
            if has_sink:
                head0 = (pl.program_id(1) * kv_per_block + kvj) * heads_per_kv + hs0
                m_old = jnp.concatenate(
                    [jnp.full((BAND, LANES), sink_ref[head0 + h] * log2e, F32) for h in range(n_rs)], axis=0)
                l_old = 1.0
            else:
                m_old, l_old = jnp.full((rows_q, LANES), NEG, F32), 0.0
        else:
            m_old = jnp.concatenate([gather(m_scr.at[hs0 + h], q_runs) for h in range(n_rs)], axis=0)
        m_new = jnp.maximum(m_old, m_blk)
        p = jnp.exp2(s - jnp.concatenate([m_new] * (nk // LANES), axis=1))
        pv = jnp.dot(p.astype(BF16), vaug, preferred_element_type=F32)
        alpha = jnp.exp2(m_old - m_new)
        if first:
            acc = pv[:, :LANES]
            l_new = pv[:, LANES:] + alpha * l_old
        else:
            acc = alpha * jnp.concatenate([gather(acc_scr.at[hs0 + h], q_runs) for h in range(n_rs)], axis=0)
            acc = acc + pv[:, :LANES]
            l_new = alpha * jnp.concatenate([gather(l_scr.at[hs0 + h], q_runs) for h in range(n_rs)], axis=0)
            l_new = l_new + pv[:, LANES:]
        if last:
            out = acc / l_new
            for u in range(units_per_blk):
                lo = (unit0 + u) * LANES
                if heads_per_unit == 1:
                    piece = out[u * BAND:(u + 1) * BAND]
                else:
                    piece = jnp.where(lane_q < d, out[2 * u * BAND:(2 * u + 1) * BAND],
                                      out[(2 * u + 1) * BAND:(2 * u + 2) * BAND])

                def store_o(rows, val, lo=lo):
                    o_ref[rows, lo:lo + LANES] = val

                scatter(store_o, q_runs, piece.astype(o_ref.dtype))
        else:
            for h in range(n_rs):
                for scr, val in ((acc_scr, acc), (l_scr, l_new), (m_scr, m_new)):
                    def store_s(rows, v, scr=scr, h=hs0 + h):
                        scr[h, rows, :] = v

                    scatter(store_s, q_runs, val[h * BAND:(h + 1) * BAND])

    for p_idx, (window, r) in enumerate(patterns):
        assert window // r == BAND and n_classes % r == 0
        s = n_classes // r
        qlen = BAND // s
        nblk = npc // qlen
        valid_first, valid_rest = band_mask(s, True), band_mask(s, False)
        streams = [(kvj, ug, cr) for kvj in range(kv_per_block) for ug in range(head_split) for cr in range(r)]

        def one_block(kvj, ug, cr, m, first_blk, p_idx=p_idx, r=r, s=s, qlen=qlen,
                      valid_first=valid_first, valid_rest=valid_rest):
            base = [npc * (cr + r * a) for a in range(s)]
            if first_blk:
                q_runs = [(b0, qlen) for b0 in base]
                block(p_idx, kvj, ug, q_runs, q_runs, valid_first)
            else:
                off = pl.multiple_of(qlen * m, qlen)
                q_runs = [(b0 + off, qlen) for b0 in base]
                k_runs = [(b0 + off - qlen, 2 * qlen) for b0 in base]
                block(p_idx, kvj, ug, q_runs, k_runs, valid_rest)

        for kvj, ug, cr in streams:
            one_block(kvj, ug, cr, 0, True)
        if nblk > 1:
            unroll = 1
            if len(streams) < 3:
                unroll = next((u for u in (2, 3, 4, 5) if (nblk - 1) % u == 0 and len(streams) * u >= 3), 1)

            def body(it, _, streams=streams, unroll=unroll, one_block=one_block):
                for u in range(unroll):
                    for kvj, ug, cr in streams:
                        one_block(kvj, ug, cr, 1 + it * unroll + u, False)
                return 0

            lax.fori_loop(0, (nblk - 1) // unroll, body, 0)


def _class_major_positions(seq, n_classes):
    return jnp.arange(seq).reshape(seq // n_classes, n_classes).T.reshape(seq)


def _band_attention(y3, sinks, *, patterns, n_classes, d, n_q_heads, n_kv_heads, q_col, k_col, v_col, name,
                    head_split=1):
    nb, seq, _ = y3.shape
    kv_per_block = LANES // d
    n_kvb = n_kv_heads // kv_per_block
    qw = n_q_heads * d // n_kvb
    n_rs = n_q_heads // n_kv_heads
    cos, sin = _rope_tables(_class_major_positions(seq, n_classes), d)
    kern = functools.partial(_band_attn_kernel, patterns=patterns, d=d, kv_per_block=kv_per_block,
                             head_split=head_split, n_classes=n_classes, seq=seq, scale=d ** -0.5,
                             has_sink=sinks is not None)
    in_specs = [pl.BlockSpec((None, seq, qw), lambda b, j, *_: (b, 0, q_col // qw + j)),
                pl.BlockSpec((None, seq, LANES), lambda b, j, *_: (b, 0, k_col // LANES + j)),
                pl.BlockSpec((None, seq, LANES), lambda b, j, *_: (b, 0, v_col // LANES + j)),
                pl.BlockSpec((seq, LANES), lambda b, j, *_: (0, 0)),
                pl.BlockSpec((seq, LANES), lambda b, j, *_: (0, 0))]
    out_specs = [pl.BlockSpec((None, seq, qw), lambda b, j, *_: (b, 0, j)),
                 pl.BlockSpec((None, seq, LANES), lambda b, j, *_: (b, 0, j))]
    out_shape = [jax.ShapeDtypeStruct((nb, seq, n_q_heads * d), BF16),
                 jax.ShapeDtypeStruct((nb, seq, n_kv_heads * d), F32)]
    scratch = []
    if len(patterns) > 1:
        scratch = [pltpu.VMEM((n_rs, seq, LANES), F32) for _ in range(3)] + [pltpu.VMEM((seq, qw), F32)]
    assert q_col % qw == 0 and k_col % LANES == 0 and v_col % LANES == 0
    if sinks is None:
        gs = pl.GridSpec(grid=(nb, n_kvb), in_specs=in_specs, out_specs=out_specs, scratch_shapes=scratch)
        args = (y3, y3, y3, cos, sin)
    else:
        gs = pltpu.PrefetchScalarGridSpec(num_scalar_prefetch=1, grid=(nb, n_kvb), in_specs=in_specs,
                                          out_specs=out_specs, scratch_shapes=scratch)
        args = (sinks.astype(F32), y3, y3, y3, cos, sin)
    return pl.pallas_call(kern, grid_spec=gs, out_shape=out_shape,
                          compiler_params=_cparams(("parallel", "arbitrary")), name=name)(*args)


def _mlstm_kernel(q_ref, k_ref, v_ref, og_ref, igc_ref, lfc_ref, igr_ref, lfr_ref, hn_ref,
                  o_ref, c_ref, n_ref, m_ref, *, rows_per_step, chunk, heads):
    L = chunk
    ri = lax.broadcasted_iota(jnp.int32, (L, L), 0)
    ci = lax.broadcasted_iota(jnp.int32, (L, L), 1)
    tril = ri >= ci
    tril_f = tril.astype(F32)
    triu_f = (ri <= ci).astype(F32)
    hp = lax.Precision.HIGHEST

    @pl.when(pl.program_id(2) == 0)
    def _():
        c_ref[...] = jnp.zeros_like(c_ref)
        n_ref[...] = jnp.zeros_like(n_ref)
        m_ref[...] = jnp.zeros_like(m_ref)

    def head_step(h, c):
        rows = pl.ds(pl.multiple_of(c * L, L), L)
        kq, kv = slice(h * A_DK, (h + 1) * A_DK), slice(h * A_DV, (h + 1) * A_DV)
        m = m_ref[h]
        qc = q_ref[rows, kq] * (A_DK ** -0.5)
        kc = k_ref[rows, kq]
        vc = v_ref[rows, kv].astype(BF16)
        icol, fcol = igc_ref[h, rows, :], lfc_ref[h, rows, :]
        irow, frow = igr_ref[h, c], lfr_ref[h, c]
        bcol = jnp.dot(tril_f, jnp.broadcast_to(fcol, (L, LANES)), precision=hp,
                       preferred_element_type=F32)[:, :1]
        brow = jnp.dot(jnp.broadcast_to(frow, (8, L)), triu_f, precision=hp, preferred_element_type=F32)[:1, :]
        acol, arow = icol - bcol, irow - brow
        log_d = jnp.where(tril, bcol + arow, NEG)
        inter = bcol + m
        mt = jnp.maximum(inter, jnp.max(log_d, axis=-1, keepdims=True))
        dmat = jnp.exp(log_d - mt)
        wi = jnp.exp(inter - mt)
        qb = qc.astype(BF16)
        sc = lax.dot_general(qb, kc.astype(BF16), (((1,), (1,)), ((), ())), preferred_element_type=F32) * dmat
        cm = c_ref[h]
        nv = n_ref[h]
        num = (jnp.dot(sc.astype(BF16), vc, preferred_element_type=F32)
               + wi * jnp.dot(qb, cm.astype(BF16), preferred_element_type=F32))
        den = jnp.sum(sc, axis=-1, keepdims=True) + wi * jnp.sum(qc * nv, axis=-1, keepdims=True)
        hc = num / jnp.maximum(jnp.abs(den), jnp.exp(-mt))
        hn = hc * lax.rsqrt(jnp.mean(hc * hc, axis=-1, keepdims=True) + NORM_EPS) * hn_ref[h]
        o_ref[rows, kv] = (hn * jax.nn.sigmoid(og_ref[rows, kv])).astype(o_ref.dtype)
        blast = bcol[L - 1:L, :]
        m_new = jnp.maximum(blast + m, jnp.max(blast + arow, axis=-1, keepdims=True))
        w_c = jnp.exp(blast + m - m_new)
        kw = kc * jnp.exp(blast + acol - m_new)
        c_ref[h] = w_c * cm + lax.dot_general(kw.astype(BF16), vc, (((0,), (0,)), ((), ())),
                                              preferred_element_type=F32)
        n_ref[h] = w_c * nv + jnp.sum(kw, axis=0, keepdims=True)
        m_ref[h] = m_new

    def step(c, _):
        for h in range(heads):
            head_step(h, c)
        return 0

    lax.fori_loop(0, rows_per_step // L, step, 0)


def _mlstm_prompt(y3, ig, lf, head_norm, *, q_col, k_col, v_col, o_col, heads=2, rows_per_step=512):
    nb, seq, _ = y3.shape
    L = A_CHUNK
    nc = seq // L
    rows_per_step = min(rows_per_step, seq)
    cps = rows_per_step // L
    igc = jnp.transpose(ig, (0, 2, 1))[..., None]
    lfc = jnp.transpose(lf, (0, 2, 1))[..., None]
    igr = igc.reshape(nb, A_HEADS, nc, 1, L)
    lfr = lfc.reshape(nb, A_HEADS, nc, 1, L)
    hn = head_norm.reshape(A_HEADS, 1, A_DV).astype(F32)
    col = lambda c0, w: pl.BlockSpec((None, rows_per_step, heads * w),
                                     lambda b, g, s: (b, s, c0 // (heads * w) + g))
    gcol = pl.BlockSpec((None, heads, rows_per_step, 1), lambda b, g, s: (b, g, s, 0))
    grow = pl.BlockSpec((None, heads, cps, 1, L), lambda b, g, s: (b, g, s, 0, 0))
    state = lambda *tail: pl.BlockSpec((None, heads) + tail, lambda b, g, s: (b, g) + (0,) * len(tail))
    assert seq % rows_per_step == 0 and A_HEADS % heads == 0 and all(
        c0 % (heads * w) == 0 for c0, w in ((q_col, A_DK), (k_col, A_DK), (v_col, A_DV), (o_col, A_DV)))
    return pl.pallas_call(
        functools.partial(_mlstm_kernel, rows_per_step=rows_per_step, chunk=L, heads=heads),
        grid=(nb, A_HEADS // heads, seq // rows_per_step),
        in_specs=[col(q_col, A_DK), col(k_col, A_DK), col(v_col, A_DV), col(o_col, A_DV),
                  gcol, gcol, grow, grow,
                  pl.BlockSpec((heads, 1, A_DV), lambda b, g, s: (g, 0, 0))],
        out_specs=[pl.BlockSpec((None, rows_per_step, heads * A_DV), lambda b, g, s: (b, s, g)),
                   state(A_DK, A_DV), state(1, A_DK), state(1, 1)],
        out_shape=[jax.ShapeDtypeStruct((nb, seq, A_HEADS * A_DV), BF16),
                   jax.ShapeDtypeStruct((nb, A_HEADS, A_DK, A_DV), F32),
                   jax.ShapeDtypeStruct((nb, A_HEADS, 1, A_DK), F32),
                   jax.ShapeDtypeStruct((nb, A_HEADS, 1, 1), F32)],
        compiler_params=_cparams(("parallel", "parallel", "arbitrary")),
        name="mlstm_prompt",
    )(y3, y3, y3, y3, igc, lfc, igr, lfr, hn)


def _mlstm_step_kernel(q_ref, k_ref, v_ref, og_ref, ig_ref, lf_ref, c0_ref, n0_ref, m0_ref, hn_ref,
                       o_ref, c1_ref, n1_ref, m1_ref):
    for h in range(A_HEADS):
        q = q_ref[h] * (A_DK ** -0.5)
        k = k_ref[h]
        v = v_ref[h]
        c0 = c0_ref[h]
        n0 = n0_ref[h]
        ig, lf, m0 = ig_ref[h], lf_ref[h], m0_ref[h]
        inter = lf + m0
        mt = jnp.maximum(inter, ig)
        dm = jnp.exp(ig - mt)
        wi = jnp.exp(inter - mt)
        sc = jnp.sum(q * k, axis=0, keepdims=True) * dm
        qc = jnp.sum(c0 * q, axis=0, keepdims=True)
        num = sc * v + wi * qc
        den = sc + wi * jnp.sum(q * n0, axis=0, keepdims=True)
        hc = num / jnp.maximum(jnp.abs(den), jnp.exp(-mt))
        hn = hc * lax.rsqrt(jnp.mean(hc * hc, axis=-1, keepdims=True) + NORM_EPS) * hn_ref[h]
        o_ref[h] = (hn * jax.nn.sigmoid(og_ref[h])).astype(o_ref.dtype)
        c1_ref[h] = wi * c0 + (k * dm) * v
        n1_ref[h] = wi * n0 + dm * k
        m1_ref[h] = mt


def _mlstm_sample(y, ig, lf, head_norm, c0, n0, m0):
    nb = y.shape[0]
    col = lambda a: a.reshape(nb, A_HEADS, A_DK, 1)
    row = lambda a: a.reshape(nb, A_HEADS, 1, A_DV)
    sca = lambda a: a.astype(F32).reshape(nb, A_HEADS, 1, 1)
    cspec = pl.BlockSpec((None, A_HEADS, A_DK, 1), lambda b: (b, 0, 0, 0))
    rspec = pl.BlockSpec((None, A_HEADS, 1, A_DV), lambda b: (b, 0, 0, 0))
    sspec = pl.BlockSpec((None, A_HEADS, 1, 1), lambda b: (b, 0, 0, 0))
    mspec = pl.BlockSpec((None, A_HEADS, A_DK, A_DV), lambda b: (b, 0, 0, 0))
    o, c1, n1, m1 = pl.pallas_call(
        _mlstm_step_kernel,
        grid=(nb,),
        in_specs=[cspec, cspec, rspec, rspec, sspec, sspec, mspec, cspec, sspec,
                  pl.BlockSpec((A_HEADS, 1, A_DV), lambda b: (0, 0, 0))],
        out_specs=[rspec, mspec, cspec, sspec],
        out_shape=[jax.ShapeDtypeStruct((nb, A_HEADS, 1, A_DV), F32),
                   jax.ShapeDtypeStruct((nb, A_HEADS, A_DK, A_DV), F32),
                   jax.ShapeDtypeStruct((nb, A_HEADS, A_DK, 1), F32),
                   jax.ShapeDtypeStruct((nb, A_HEADS, 1, 1), F32)],
        compiler_params=_cparams(("parallel",)),
        name="mlstm_step",
    )(col(y[:, A_Q:A_Q + A_HEADS * A_DK]), col(y[:, A_K:A_K + A_HEADS * A_DK]),
      row(y[:, A_V:A_V + A_HEADS * A_DV]), row(y[:, A_O:A_O + A_HEADS * A_DV]),
      sca(ig), sca(lf), c0.astype(F32), col(n0.astype(F32)), sca(m0),
      head_norm.astype(F32).reshape(A_HEADS, 1, A_DV))
    return (o.reshape(nb, A_HEADS * A_DV), c1, n1.reshape(nb, A_HEADS, A_DK), m1.reshape(nb, A_HEADS))


def _decode_attn_kernel(*refs, n_pat, d, group, scale, has_sink):
    q_ref, kn_ref, vn_ref, cos_ref, sin_ref = refs[:5]
    refs = refs[5:]
    if has_sink:
        sink_ref, refs = refs[0], refs[1:]
    k_refs, v_refs = refs[:n_pat], refs[n_pat:2 * n_pat]
    o_ref, knew_ref = refs[2 * n_pat:]
    n_seg = LANES // d
    cos, sin = cos_ref[...], sin_ref[...]
    kn = _rope_lanes(kn_ref[...], cos, sin, d)
    knew_ref[...] = kn
    vn = vn_ref[...]
    lane = lax.broadcasted_iota(jnp.int32, kn.shape, 1)

    def seg_sum(p):
        if n_seg == 1:
            return jnp.sum(p, axis=-1, keepdims=True)
        lane_p = lax.broadcasted_iota(jnp.int32, p.shape, p.ndim - 1)
        lo = jnp.sum(jnp.where(lane_p < d, p, 0.0), axis=-1, keepdims=True)
        hi = jnp.sum(jnp.where(lane_p >= d, p, 0.0), axis=-1, keepdims=True)
        return jnp.where(lane_p < d, lo, hi)

    for g in range(group):
        qg = _rope_lanes(q_ref[g], cos, sin, d)
        s_new = seg_sum(kn * qg) * scale
        s = [seg_sum(k_refs[p][...] * qg[None]) * scale for p in range(n_pat)]
        m = s_new
        for p in range(n_pat):
            m = jnp.maximum(m, jnp.max(s[p], axis=0))
        if has_sink:
            m = jnp.maximum(m, sink_ref[g])
        e_new = jnp.exp(s_new - m) * float(n_pat)
        l = e_new
        acc = e_new * vn
        for p in range(n_pat):
            e = jnp.exp(s[p] - m[None])
            l = l + jnp.sum(e, axis=0)
            acc = acc + jnp.sum(e * v_refs[p][...], axis=0)
        if has_sink:
            l = l + jnp.exp(sink_ref[g] - m)
        o_ref[g] = (acc / l).astype(o_ref.dtype)


def _decode_attention(q, k_new, v_new, k_cache, v_cache, sinks, pos, *, patterns, d, name):
    nb, n_heads, _ = q.shape
    n_kv, cache_len = k_cache.shape[2], k_cache.shape[1]
    group = n_heads // n_kv
    n_seg = LANES // d
    r_rows = n_kv // n_seg
    to_tiles = lambda a, inner: a.reshape(nb, r_rows, n_seg, inner, d).transpose(0, 3, 1, 2, 4).reshape(
        nb, inner, r_rows, LANES)
    qt = to_tiles(q, group)
    knt = k_new.reshape(nb, r_rows, LANES)
    vnt = v_new.reshape(nb, r_rows, LANES)
    cos, sin = _rope_tables(pos, d)
    kc = k_cache.reshape(nb, cache_len, r_rows, LANES)
    vc = v_cache.reshape(nb, cache_len, r_rows, LANES)
    tile = pl.BlockSpec((None, r_rows, LANES), lambda b: (b, 0, 0))
    one = pl.BlockSpec((1, LANES), lambda b: (0, 0))
    in_specs = [pl.BlockSpec((None, group, r_rows, LANES), lambda b: (b, 0, 0, 0)), tile, tile, one, one]
    args = [qt, knt, vnt, cos, sin]
    if sinks is not None:
        sk = jnp.broadcast_to(sinks.astype(F32).reshape(r_rows, n_seg, group, 1), (r_rows, n_seg, group, d))
        args.append(sk.transpose(2, 0, 1, 3).reshape(group, r_rows, LANES))
        in_specs.append(pl.BlockSpec((group, r_rows, LANES), lambda b: (0, 0, 0)))
    views = []
    for window, r in patterns:
        assert window // r == BAND and cache_len % (BAND * r) == 0
        blk = cache_len // r // BAND - 1
        views.append(pl.BlockSpec((None, BAND, None, r_rows, LANES), lambda b, blk=blk: (b, blk, 0, 0, 0)))
    in_specs += views + views
    args += [kc.reshape(nb, cache_len // r, r, r_rows, LANES) for _, r in patterns]
    args += [vc.reshape(nb, cache_len // r, r, r_rows, LANES) for _, r in patterns]
    o, kn = pl.pallas_call(
        functools.partial(_decode_attn_kernel, n_pat=len(patterns), d=d, group=group, scale=d ** -0.5,
                          has_sink=sinks is not None),
        grid=(nb,),
        in_specs=in_specs,
        out_specs=[pl.BlockSpec((None, group, r_rows, LANES), lambda b: (b, 0, 0, 0)), tile],
        out_shape=[jax.ShapeDtypeStruct((nb, group, r_rows, LANES), F32),
                   jax.ShapeDtypeStruct((nb, r_rows, LANES), F32)],
        compiler_params=_cparams(("parallel",)),
        name=name,
    )(*args)
    o = o.reshape(nb, group, r_rows, n_seg, d).transpose(0, 2, 3, 1, 4).reshape(nb, n_heads * d)
    return o, kn.reshape(nb, n_kv, d)


def _shift_kernel(c_ref, new_ref, o_ref, *, length, chunk):
    n_full = (length - 1) // chunk

    def body(i, _):
        o_ref[pl.ds(i * chunk, chunk)] = c_ref[pl.ds(i * chunk + 1, chunk)]
        return 0

    lax.fori_loop(0, n_full, body, 0)
    done = n_full * chunk
    if done < length - 1:
        o_ref[done:length - 1] = c_ref[done + 1:length]
    o_ref[length - 1] = new_ref[...]


def _shift_cache(cache, new, name):
    nb, length, n_kv, d = cache.shape
    return pl.pallas_call(
        functools.partial(_shift_kernel, length=length, chunk=min(64, length - 1)),
        grid=(nb,),
        in_specs=[pl.BlockSpec((None, length, n_kv, d), lambda b: (b, 0, 0, 0)),
                  pl.BlockSpec((None, n_kv, d), lambda b: (b, 0, 0))],
        out_specs=pl.BlockSpec((None, length, n_kv, d), lambda b: (b, 0, 0, 0)),
        out_shape=jax.ShapeDtypeStruct(cache.shape, cache.dtype),
        compiler_params=_cparams(("parallel",)),
        name=name,
    )(cache, new.astype(cache.dtype))


N_IN_EVEN = 2 * A_HEADS * A_DK + 2 * A_HEADS * A_DV + 2 * A_HEADS + (B_HEADS + 2 * B_KV_HEADS) * B_HEAD_DIM
EVEN_IN = N_IN_EVEN // 512 * 512
A_Q, A_K, A_V, A_O = 0, 1024, 2048, 4096
GATE_COL = 6144
B_COL = GATE_COL + 2 * A_HEADS
B_Q, B_K, B_V = 0, 2048, 2304
C_Q, C_K, C_V = 0, 4096, 5120
SWA_PATTERN = ((B_WINDOW, 1),)


def _even_projection(hp, hs, w_in, j):
    ys = _matmul2([(hp, hs, w_in, j, 0)], EVEN_IN, F32, 1024, 512, "even_in")
    n_tail = N_IN_EVEN - EVEN_IN
    w_tail = jnp.pad(w_in[j, :, EVEN_IN:], ((0, 0), (0, LANES - n_tail)))[None]
    tails = _matmul2([(hp, hs, w_tail, 0, 0)], LANES, F32, 1024, LANES, "even_in_tail")
    return [(y, y[:, GATE_COL:B_COL], jnp.concatenate([y[:, B_COL:], t[:, :n_tail]], axis=-1))
            for y, t in zip(ys, tails)]


def _gates(yg, gate_b):
    gb = gate_b.astype(F32)
    ig = yg[..., :A_HEADS] + gb[:A_HEADS]
    lf = jax.nn.log_sigmoid(yg[..., A_HEADS:] + gb[A_HEADS:])
    return ig, lf


def _even_mixer_prompt(proj, nb, seq, gate_b, head_norm, sinks):
    y, yg, yb = proj
    y3, yb3 = y.reshape(nb, seq, -1), yb.reshape(nb, seq, -1)
    ig, lf = _gates(yg.reshape(nb, seq, -1), gate_b)
    ha, c1, n1, m1 = _mlstm_prompt(y3, ig, lf, head_norm, q_col=A_Q, k_col=A_K, v_col=A_V, o_col=A_O)
    ob, kb = _band_attention(yb3, sinks, patterns=SWA_PATTERN, n_classes=1, d=B_HEAD_DIM, n_q_heads=B_HEADS,
                             n_kv_heads=B_KV_HEADS, q_col=B_Q, k_col=B_K, v_col=B_V, name="swa_prompt",
                             head_split=2)
    kbuf = kb[:, -B_WINDOW:].reshape(nb, -1, B_KV_HEADS, B_HEAD_DIM)
    vbuf = yb3[:, -B_WINDOW:, B_V:].reshape(nb, -1, B_KV_HEADS, B_HEAD_DIM)
    state = (c1, n1.reshape(nb, A_HEADS, A_DK), m1.reshape(nb, A_HEADS), kbuf, vbuf)
    return ha.reshape(nb * seq, -1), ob.reshape(nb * seq, -1), state


def _even_mixer_sample(proj, gate_b, head_norm, sinks, state):
    y, yg, yb = proj
    nb = y.shape[0]
    ig, lf = _gates(yg, gate_b)
    c0, n0, m0, k_old, v_old = state
    ha, c1, n1, m1 = _mlstm_sample(y, ig, lf, head_norm, c0, n0, m0)
    q = yb[:, B_Q:B_K].reshape(nb, B_HEADS, B_HEAD_DIM)
    k_new = yb[:, B_K:B_V].reshape(nb, B_KV_HEADS, B_HEAD_DIM)
    v_new = yb[:, B_V:].reshape(nb, B_KV_HEADS, B_HEAD_DIM)
    ob, k_rot = _decode_attention(q, k_new, v_new, k_old, v_old, sinks, PAST_LEN + jnp.arange(1),
                                  patterns=SWA_PATTERN, d=B_HEAD_DIM, name="swa_step")
    state = (c1, n1, m1, _shift_cache(k_old, k_rot, "swa_k_shift"), _shift_cache(v_old, v_new, "swa_v_shift"))
    return ha.astype(BF16), ob.astype(BF16), state


def _even_mixer(hp, hs, nb, seq, j, w_in, gate_b, head_norm, sinks, w_out, state):
    proj_p, proj_s = _even_projection(hp, hs, w_in, j)
    ha_p, ob_p, st_p = _even_mixer_prompt(proj_p, nb, seq, gate_b, head_norm, sinks)
    ha_s, ob_s, st_s = _even_mixer_sample(proj_s, gate_b, head_norm, sinks, state)
    outs = _matmul2([(ha_p, ha_s, w_out, j, 0), (ob_p, ob_s, w_out, j, 1)], D_MODEL, F32, 1024, 512, "even_out")
    return outs, (st_p, st_s)


def _odd_mixer(hp, hs, nb, seq, j, w_in, w_out, state):
    out_p, st_p = _odd_mixer_prompt(hp, nb, seq, j, w_in, w_out)
    out_s, st_s = _odd_mixer_sample(hs, j, w_in, w_out, state)
    return (out_p, out_s), (st_p, st_s)


C_CLASSES = max(r for _, r in C_PATTERNS)


def _odd_mixer_prompt(h2, nb, seq, j, w_in, w_out):
    y3 = _matmul_regroup(h2, w_in, j, w_in.shape[2], nb, C_CLASSES, True, 1024, 512, "odd_in").reshape(nb, seq, -1)
    o, kr = _band_attention(y3, None, patterns=C_PATTERNS, n_classes=C_CLASSES, d=C_HEAD_DIM, n_q_heads=C_HEADS,
                            n_kv_heads=C_KV_HEADS, q_col=C_Q, k_col=C_K, v_col=C_V, name="dil_prompt")
    out = _matmul_regroup(o.reshape(nb, C_CLASSES, seq // C_CLASSES, -1), w_out, j, D_MODEL, nb, C_CLASSES, False,
                          1024, 512, "odd_out")
    natural = lambda a: a.reshape(nb, C_CLASSES, seq // C_CLASSES, C_KV_HEADS, C_HEAD_DIM).transpose(
        0, 2, 1, 3, 4).reshape(nb, seq, C_KV_HEADS, C_HEAD_DIM)
    return out, (natural(kr), natural(y3[:, :, C_V:]))


def _odd_mixer_sample(h2, j, w_in, w_out, state):
    nb = h2.shape[0]
    y = _matmul([(h2, w_in, j, 0)], w_in.shape[2], F32, nb, 512, "odd_in_s")
    q = y[:, C_Q:C_K].reshape(nb, C_HEADS, C_HEAD_DIM)
    k_new = y[:, C_K:C_V].reshape(nb, C_KV_HEADS, C_HEAD_DIM)
    v_new = y[:, C_V:].reshape(nb, C_KV_HEADS, C_HEAD_DIM)
    k_old, v_old = state
    o, k_rot = _decode_attention(q, k_new, v_new, k_old, v_old, None, PAST_LEN + jnp.arange(1),
                                 patterns=C_PATTERNS, d=C_HEAD_DIM, name="dil_step")
    out = _matmul([(o.astype(BF16), w_out, j, 0)], D_MODEL, F32, nb, 512, "odd_out_s")
    return out, (_shift_cache(k_old, k_rot, "dil_k_shift"), _shift_cache(v_old, v_new, "dil_v_shift"))


def _trunk(xs, rows, mixers, norms, ffn):
    g_mix_pre, g_mix_post, g_ffn_pre, g_ffn_post = norms
    wg, wu, wd = ffn
    xs = list(xs)
    hs = [_prenorm(r, x, g_mix_pre[0], 0, MOD_SC1, MOD_SH1) for r, x in zip(rows, xs)]
    states = []
    for li in range(DEPTH):
        ys, state = mixers[li](*hs)
        states.append(state)
        for g in range(2):
            xs[g], hs[g] = _postnorm(rows[g], xs[g], ys[g], g_mix_post[li], li, MOD_GT1,
                                     (g_ffn_pre[li], li, MOD_SC2, MOD_SH2))
        acts = _matmul2([(hs[0], hs[1], (wg, wu), li, 0)], wg.shape[2], BF16, 1024, 256, "swiglu", swiglu=True)
        fs = [_matmul([(a, wd, li, 0)], D_MODEL, F32, 512, 512, "ffn_down") for a in acts]
        nxt = (g_mix_pre[li + 1], li + 1, MOD_SC1, MOD_SH1) if li + 1 < DEPTH else None
        for g in range(2):
            xs[g], hs[g] = _postnorm(rows[g], xs[g], fs[g], g_ffn_post[li], li, MOD_GT2, nxt)
    return xs, states


def kernel(x_prompt, x_sample, state_l0_mlstm_c, state_l0_mlstm_n, state_l0_mlstm_m, cache_l0_swa_k, cache_l0_swa_v, cache_l1_dil_k, cache_l1_dil_v, state_l2_mlstm_c, state_l2_mlstm_n, state_l2_mlstm_m, cache_l2_swa_k, cache_l2_swa_v, cache_l3_dil_k, cache_l3_dil_v, c_prompt, c_sample, ada_w, ada_b, norm_mix_pre, norm_mix_post, norm_ffn_pre, norm_ffn_post, even_w_in, even_gate_b, even_head_norm, even_sinks, even_w_out, odd_w_in, odd_w_out, ffn_w_gate, ffn_w_up, ffn_w_down):
    past = {0: (state_l0_mlstm_c, state_l0_mlstm_n, state_l0_mlstm_m, cache_l0_swa_k, cache_l0_swa_v),
            1: (cache_l1_dil_k, cache_l1_dil_v),
            2: (state_l2_mlstm_c, state_l2_mlstm_n, state_l2_mlstm_m, cache_l2_swa_k, cache_l2_swa_v),
            3: (cache_l3_dil_k, cache_l3_dil_v)}
    nbp, seq, dm = x_prompt.shape
    nbs = x_sample.shape[0]
    c_all = jnp.concatenate([c_sample, c_prompt], axis=0)
    pad = (-c_all.shape[0]) % 16
    sc_all = jnp.pad(jax.nn.silu(c_all), ((0, pad), (0, 0))).astype(BF16)
    mod_all = _ada(sc_all, ada_w, ada_b)
    mixers = []
    for li in range(DEPTH):
        j = li // 2
        if li % 2 == 0:
            mw = (j, even_w_in, even_gate_b[j], even_head_norm[j], even_sinks[j], even_w_out, past[li])
            mixers.append(lambda hp, hs, mw=mw: _even_mixer(hp, hs, nbp, seq, *mw))
        else:
            mw = (j, odd_w_in, odd_w_out, past[li])
            mixers.append(lambda hp, hs, mw=mw: _odd_mixer(hp, hs, nbp, seq, *mw))
    norms = (norm_mix_pre, norm_mix_post, norm_ffn_pre, norm_ffn_post)
    ffn = (ffn_w_gate, ffn_w_up, ffn_w_down.astype(BF16))
    rows = (_Rows(mod_all, nbp * seq, seq, nbs, 256), _Rows(mod_all, nbs, 1, 0, nbs))
    (y_p, y_s), states = _trunk((x_prompt.reshape(nbp * seq, dm), x_sample.reshape(nbs, dm)), rows, mixers,
                                norms, ffn)
    y_p, y_s = y_p.reshape(x_prompt.shape), y_s.reshape(x_sample.shape)
    new_p, new_s = [s[0] for s in states], [s[1] for s in states]
    (p0c, p0n, p0m, p0k, p0v), (p1k, p1v), (p2c, p2n, p2m, p2k, p2v), (p3k, p3v) = new_p
    (s0c, s0n, s0m, s0k, s0v), (s1k, s1v), (s2c, s2n, s2m, s2k, s2v), (s3k, s3v) = new_s
    return (y_p, y_s,
            p0c, p0n, p0m, p0k, p0v, p1k, p1v, p2c, p2n, p2m, p2k, p2v, p3k, p3v,
            s0c, s0n, s0m, s0k, s0v, s1k, s1v, s2c, s2n, s2m, s2k, s2v, s3k, s3v)
```

```python
import functools
import math

import jax
import jax.numpy as jnp
import numpy as np
from jax import lax
from jax.experimental import pallas as pl
from jax.experimental.pallas import tpu as pltpu

F32 = jnp.float32
BF16 = jnp.bfloat16

D_MODEL = 4096
DEPTH = 4
PAST_LEN = 8192
A_HEADS, A_DK, A_DV, A_CHUNK = 4, 256, 512, 64
B_HEADS, B_KV_HEADS, B_HEAD_DIM, B_WINDOW = 32, 4, 64, 128
C_HEADS, C_KV_HEADS, C_HEAD_DIM = 32, 8, 128
C_PATTERNS = ((128, 1), (512, 4), (2048, 16))
BAND = 128
ROPE_THETA = 10000.0
NORM_EPS = 1e-6
LANES = 128
NEG = -1e30
VMEM_LIMIT = 56 * 1024 * 1024


def _cparams(sem):
    return pltpu.CompilerParams(dimension_semantics=sem, vmem_limit_bytes=VMEM_LIMIT)


def _mm_kernel(*refs, n_in):
    o_ref = refs[-1]
    acc = None
    for i in range(n_in):
        x = refs[2 * i][...].astype(BF16)
        w = refs[2 * i + 1][...].astype(BF16)
        d = jnp.dot(x, w, preferred_element_type=F32)
        acc = d if acc is None else acc + d
    o_ref[...] = acc.astype(o_ref.dtype)


def _w_spec(w, layer, kblk, k, tn):
    if w.ndim == 3:
        return pl.BlockSpec((None, k, tn), lambda i, j: (layer, kblk, j))
    return pl.BlockSpec((k, tn), lambda i, j: (kblk, j))


def _matmul(pairs, n, out_dtype, tm, tn, name):
    m = pairs[0][0].shape[0]
    tm = min(tm, m)
    tn = min(tn, n)
    assert m % tm == 0 and n % tn == 0, (m, n, tm, tn)
    in_specs, args = [], []
    for x, w, layer, kblk in pairs:
        k = x.shape[1]
        in_specs += [pl.BlockSpec((tm, k), lambda i, j: (i, 0)), _w_spec(w, layer, kblk, k, tn)]
        args += [x, w]
    return pl.pallas_call(
        functools.partial(_mm_kernel, n_in=len(pairs)),
        grid=(m // tm, n // tn),
        in_specs=in_specs,
        out_specs=pl.BlockSpec((tm, tn), lambda i, j: (i, j)),
        out_shape=jax.ShapeDtypeStruct((m, n), out_dtype),
        compiler_params=_cparams(("parallel", "arbitrary")),
        name=name,
    )(*args)


def _mm2_kernel(*refs, n_in, swiglu, w_rows_are_outputs):
    n_w = 2 if swiglu else 1
    per = 2 + n_w
    ins = refs[:n_in * per]
    op_ref, os_ref = refs[n_in * per:]
    ws = [ins[(t // n_w) * per + 2 + t % n_w][...].astype(BF16) for t in range(n_in * n_w)]

    def compute(xsel, o_ref):
        accs = [None] * n_w
        for p in range(n_in):
            x = ins[p * per + xsel][...]
            for t in range(n_w):
                contract = (((1,), (1 if w_rows_are_outputs else 0,)), ((), ()))
                dt = lax.dot_general(x, ws[p * n_w + t], contract, preferred_element_type=F32)
                accs[t] = dt if accs[t] is None else accs[t] + dt
        r = accs[0] * jax.nn.sigmoid(accs[0]) * accs[1] if swiglu else accs[0]
        o_ref[...] = r.astype(o_ref.dtype)

    compute(0, op_ref)

    @pl.when(pl.program_id(0) == 0)
    def _():
        compute(1, os_ref)


def _matmul2(pairs, n, out_dtype, tm, tn, name, swiglu=False, w_rows_are_outputs=False):
    m, ms = pairs[0][0].shape[0], pairs[0][1].shape[0]
    assert m % tm == 0 and n % tn == 0, (m, n, tm, tn)
    n_j = n // tn
    in_specs, args = [], []
    for xp, xs, ws, layer, kblk in pairs:
        k = xp.shape[1]
        in_specs += [pl.BlockSpec((tm, k), lambda i, j: (i, 0)), pl.BlockSpec((ms, k), lambda i, j: (0, 0))]
        args += [xp, xs]
        for w in (ws if swiglu else (ws,)):
            if w_rows_are_outputs:
                in_specs.append(pl.BlockSpec((None, tn, k), lambda i, j, layer=layer, kblk=kblk: (layer, j, kblk)))
            else:
                in_specs.append(pl.BlockSpec((None, k, tn), lambda i, j, layer=layer, kblk=kblk: (layer, kblk, j)))
            args.append(w)
    return pl.pallas_call(
        functools.partial(_mm2_kernel, n_in=len(pairs), swiglu=swiglu, w_rows_are_outputs=w_rows_are_outputs),
        grid=(m // tm, n_j),
        in_specs=in_specs,
        out_specs=[pl.BlockSpec((tm, tn), lambda i, j: (i, j)),
                   pl.BlockSpec((ms, tn), lambda i, j: (0, jnp.where(i == 0, j, n_j - 1)))],
        out_shape=[jax.ShapeDtypeStruct((m, n), out_dtype), jax.ShapeDtypeStruct((ms, n), out_dtype)],
        compiler_params=_cparams(("arbitrary", "arbitrary")),
        name=name,
    )(*args)


def _digit_swap(n):
    p = lax.broadcasted_iota(jnp.int32, (n * n, n * n), 0)
    q = lax.broadcasted_iota(jnp.int32, (n * n, n * n), 1)
    shift = n.bit_length() - 1
    return (q == ((p & (n - 1)) << shift) + (p >> shift)).astype(BF16)


def _mm_regroup_kernel(x_ref, w_ref, o_ref, xp_scr, *, n_cls, to_classes):
    blk = n_cls * n_cls
    tm = xp_scr.shape[0]

    @pl.when(pl.program_id(1) == 0)
    def _():
        perm = _digit_swap(n_cls)
        for k in range(tm // blk):
            if to_classes:
                xs = x_ref[k * blk:(k + 1) * blk, :]
            else:
                xs = jnp.concatenate([x_ref[c, k * n_cls:(k + 1) * n_cls, :] for c in range(n_cls)], axis=0)
            xp_scr[k * blk:(k + 1) * blk, :] = jnp.dot(perm, xs, preferred_element_type=F32).astype(BF16)

    r = jnp.dot(xp_scr[...], w_ref[...].astype(BF16), preferred_element_type=F32)
    if to_classes:
        for k in range(tm // blk):
            for c in range(n_cls):
                o_ref[c, k * n_cls:(k + 1) * n_cls, :] = r[k * blk + c * n_cls:k * blk + (c + 1) * n_cls]
    else:
        o_ref[...] = r


def _matmul_regroup(x, w, layer, n, nb, n_cls, to_classes, tm, tn, name):
    k = x.shape[-1]
    seq = x.shape[0] // nb if to_classes else n_cls * x.shape[2]
    tiles = seq // tm
    assert seq % tm == 0 and tm % (n_cls * n_cls) == 0 and n % tn == 0
    cls_block = lambda cols: (None, n_cls, tm // n_cls, cols)
    if to_classes:
        x_spec = pl.BlockSpec((tm, k), lambda i, j: (i, 0))
        o_spec = pl.BlockSpec(cls_block(tn), lambda i, j: (i // tiles, 0, i % tiles, j))
        o_shape = (nb, n_cls, seq // n_cls, n)
    else:
        x_spec = pl.BlockSpec(cls_block(k), lambda i, j: (i // tiles, 0, i % tiles, 0))
        o_spec = pl.BlockSpec((tm, tn), lambda i, j: (i, j))
        o_shape = (nb * seq, n)
    return pl.pallas_call(
        functools.partial(_mm_regroup_kernel, n_cls=n_cls, to_classes=to_classes),
        grid=(nb * tiles, n // tn),
        in_specs=[x_spec, _w_spec(w, layer, 0, k, tn)],
        out_specs=o_spec,
        out_shape=jax.ShapeDtypeStruct(o_shape, F32),
        scratch_shapes=[pltpu.VMEM((tm, k), BF16)],
        compiler_params=_cparams(("parallel", "arbitrary")),
        name=name,
    )(x, w)


MOD_SH1, MOD_SC1, MOD_GT1, MOD_SH2, MOD_SC2, MOD_GT2 = range(6)


def _rms_rows(x, g):
    return x * lax.rsqrt(jnp.mean(x * x, axis=-1, keepdims=True) + NORM_EPS) * g


def _prenorm_kernel(x_ref, g_ref, sc_ref, sh_ref, h_ref):
    h_ref[...] = (_rms_rows(x_ref[...], g_ref[...]) * (1.0 + sc_ref[...]) + sh_ref[...]).astype(h_ref.dtype)


def _postnorm_kernel(*refs, has_next):
    x_ref, y_ref, gt_ref, gpost_ref = refs[:4]
    xn = x_ref[...] + gt_ref[...] * _rms_rows(y_ref[...], gpost_ref[...])
    if has_next:
        gpre_ref, sc_ref, sh_ref, xo_ref, h_ref = refs[4:]
        h_ref[...] = (_rms_rows(xn, gpre_ref[...]) * (1.0 + sc_ref[...]) + sh_ref[...]).astype(h_ref.dtype)
    else:
        xo_ref = refs[4]
    xo_ref[...] = xn


class _Rows:
    def __init__(self, mod_all, n_rows, rows_per_batch, mod_row0, tm):
        self.dm = mod_all.shape[-1] // 6
        self.tm = min(tm, n_rows)
        self.n_rows = n_rows
        self.per_token = rows_per_batch == 1
        if self.per_token:
            self.mod = mod_all
            assert mod_row0 % self.tm == 0 and n_rows == self.tm
        else:
            self.mod = mod_all.reshape(mod_all.shape[0], mod_all.shape[1], 1, mod_all.shape[2])
            assert rows_per_batch % self.tm == 0
        self.tiles_per_batch = max(rows_per_batch // self.tm, 1)
        self.mod_row0 = mod_row0

    def rows(self):
        return pl.BlockSpec((self.tm, self.dm), lambda i: (i, 0))

    def vec(self, layer, chunk):
        if self.per_token:
            return pl.BlockSpec((None, self.tm, self.dm), lambda i: (layer, self.mod_row0 // self.tm, chunk))
        return pl.BlockSpec((None, None, 1, self.dm),
                            lambda i: (layer, self.mod_row0 + i // self.tiles_per_batch, 0, chunk))


def _gain_spec(dm):
    return pl.BlockSpec((None, 1, dm), lambda i: (0, 0, 0))


def _prenorm(rows, x2, gain, layer, sc_chunk, sh_chunk):
    dm = rows.dm
    return pl.pallas_call(
        _prenorm_kernel,
        grid=(rows.n_rows // rows.tm,),
        in_specs=[rows.rows(), _gain_spec(dm), rows.vec(layer, sc_chunk), rows.vec(layer, sh_chunk)],
        out_specs=rows.rows(),
        out_shape=jax.ShapeDtypeStruct((rows.n_rows, dm), BF16),
        compiler_params=_cparams(("parallel",)),
        name="prenorm",
    )(x2, gain.astype(F32).reshape(1, 1, dm), rows.mod, rows.mod)


def _postnorm(rows, x2, y2, gain_post, layer, gt_chunk, nxt):
    dm = rows.dm
    in_specs = [rows.rows(), rows.rows(), rows.vec(layer, gt_chunk), _gain_spec(dm)]
    args = [x2, y2, rows.mod, gain_post.astype(F32).reshape(1, 1, dm)]
    out_specs = [rows.rows()]
    out_shape = [jax.ShapeDtypeStruct((rows.n_rows, dm), F32)]
    if nxt is not None:
        gain_pre, nl, sc_chunk, sh_chunk = nxt
        in_specs += [_gain_spec(dm), rows.vec(nl, sc_chunk), rows.vec(nl, sh_chunk)]
        args += [gain_pre.astype(F32).reshape(1, 1, dm), rows.mod, rows.mod]
        out_specs.append(rows.rows())
        out_shape.append(jax.ShapeDtypeStruct((rows.n_rows, dm), BF16))
    res = pl.pallas_call(
        functools.partial(_postnorm_kernel, has_next=nxt is not None),
        grid=(rows.n_rows // rows.tm,),
        in_specs=in_specs, out_specs=out_specs, out_shape=out_shape,
        compiler_params=_cparams(("parallel",)),
        name="postnorm",
    )(*args)
    return (res[0], res[1]) if nxt is not None else (res[0], None)


def _ada_kernel(c_ref, w_ref, b_ref, o_ref):
    o_ref[...] = jnp.dot(c_ref[...], w_ref[...].astype(BF16), preferred_element_type=F32) + b_ref[...]


def _ada(sc_all, ada_w, ada_b, tn=512):
    nl, k, n = ada_w.shape
    r = sc_all.shape[0]
    return pl.pallas_call(
        _ada_kernel,
        grid=(nl, n // tn),
        in_specs=[pl.BlockSpec((r, k), lambda l, j: (0, 0)),
                  pl.BlockSpec((None, k, tn), lambda l, j: (l, 0, j)),
                  pl.BlockSpec((None, 1, tn), lambda l, j: (l, 0, j))],
        out_specs=pl.BlockSpec((None, r, tn), lambda l, j: (l, 0, j)),
        out_shape=jax.ShapeDtypeStruct((nl, r, n), F32),
        compiler_params=_cparams(("parallel", "arbitrary")),
        name="ada",
    )(sc_all, ada_w, ada_b.astype(F32).reshape(nl, 1, n))


def _rope_tables(pos, d):
    half = d // 2
    inv = jnp.exp(jnp.arange(half, dtype=F32) * (-2.0 * math.log(ROPE_THETA) / d))
    ang = pos.astype(F32)[:, None] * inv[None, :]
    cos, sin = jnp.cos(ang), jnp.sin(ang)
    reps = LANES // d
    return (jnp.tile(jnp.concatenate([cos, cos], axis=-1), (1, reps)),
            jnp.tile(jnp.concatenate([-sin, sin], axis=-1), (1, reps)))


def _rope_lanes(x, cos, sin, d):
    if d == LANES:
        partner = pltpu.roll(x, LANES // 2, axis=1)
    else:
        lane = lax.broadcasted_iota(jnp.int32, x.shape, 1)
        half = d // 2
        partner = jnp.where((lane % d) < half, pltpu.roll(x, LANES - half, axis=1), pltpu.roll(x, half, axis=1))
    return x * cos + partner * sin


def _band_attn_kernel(*refs, patterns, d, kv_per_block, head_split, n_classes, seq, scale, has_sink):
    if has_sink:
        sink_ref, refs = refs[0], refs[1:]
    q_ref, k_ref, v_ref, cos_ref, sin_ref, o_ref, krot_ref = refs[:7]
    scratch = refs[7:]
    n_pat = len(patterns)
    if n_pat > 1:
        acc_scr, m_scr, l_scr, qrot_scr = scratch
    heads_per_unit = LANES // d
    n_units = q_ref.shape[1] // LANES
    units_per_kv = n_units // kv_per_block
    heads_per_kv = units_per_kv * heads_per_unit
    units_per_blk = units_per_kv // head_split
    n_rs = units_per_blk * heads_per_unit
    rows_q = n_rs * BAND
    npc = seq // n_classes
    log2e = math.log2(math.e)
    q_scale = scale * log2e

    chunk = min(256, seq)

    def rope_all(i, _):
        rows = pl.ds(pl.multiple_of(i * chunk, chunk), chunk)
        cos_c, sin_c = cos_ref[rows, :], sin_ref[rows, :]
        krot_ref[rows, :] = _rope_lanes(k_ref[rows, :], cos_c, sin_c, d)
        if n_pat > 1:
            for u in range(n_units):
                lanes = slice(u * LANES, (u + 1) * LANES)
                qrot_scr[rows, lanes] = _rope_lanes(q_ref[rows, lanes], cos_c, sin_c, d) * q_scale
        return 0

    lax.fori_loop(0, seq // chunk, rope_all, 0)

    lane_q = lax.broadcasted_iota(jnp.int32, (BAND, LANES), 1)

    def band_mask(s, first):
        qlen = BAND // s
        klen = qlen if first else 2 * qlen
        nk = s * klen
        rq = lax.broadcasted_iota(jnp.int32, (BAND, nk), 0)
        rk = lax.broadcasted_iota(jnp.int32, (BAND, nk), 1)
        lq, lk = qlen.bit_length() - 1, klen.bit_length() - 1
        rel = s * ((rq & (qlen - 1)) - (rk & (klen - 1))) + ((rq >> lq) - (rk >> lk)) + (0 if first else BAND)
        return (rel >= 0) & (rel <= BAND)

    def gather(ref, runs, lanes=None):
        rd = lambda st, ln: ref[pl.ds(st, ln), :] if lanes is None else ref[pl.ds(st, ln), lanes]
        parts = [rd(st, ln) for st, ln in runs]
        return parts[0] if len(parts) == 1 else jnp.concatenate(parts, axis=0)

    def scatter(store, runs, value):
        off = 0
        for st, ln in runs:
            store(pl.ds(st, ln), value[off:off + ln])
            off += ln

    def dup(a, j):
        if d == LANES:
            return a
        lane = lax.broadcasted_iota(jnp.int32, a.shape, 1)
        rolled = pltpu.roll(a, LANES // 2, axis=1)
        return jnp.where(lane < d, a, rolled) if j == 0 else jnp.where(lane < d, rolled, a)

    def block(p_idx, kvj, ug, q_runs, k_runs, valid):
        first, last = p_idx == 0, p_idx == n_pat - 1
        nk = sum(ln for _, ln in k_runs)
        if n_pat == 1:
            cosq, sinq = gather(cos_ref, q_runs), gather(sin_ref, q_runs)
        parts = []
        unit0 = kvj * units_per_kv + ug * units_per_blk
        hs0 = ug * n_rs
        for u in range(units_per_blk):
            lo = (unit0 + u) * LANES
            if n_pat > 1:
                xr = gather(qrot_scr, q_runs, slice(lo, lo + LANES))
            else:
                xr = _rope_lanes(gather(q_ref, q_runs, slice(lo, lo + LANES)), cosq, sinq, d) * q_scale
            if heads_per_unit == 1:
                parts.append(xr)
            else:
                parts += [jnp.where(lane_q < d, xr, 0.0), jnp.where(lane_q >= d, xr, 0.0)]
        qst = jnp.concatenate(parts, axis=0).astype(BF16)
        kk = dup(gather(krot_ref, k_runs), kvj).astype(BF16)
        vv = dup(gather(v_ref, k_runs), kvj).astype(BF16)
        vaug = jnp.concatenate([vv, jnp.ones((nk, LANES), BF16)], axis=1)
        s = lax.dot_general(qst, kk, (((1,), (1,)), ((), ())), preferred_element_type=F32)
        s = jnp.where(valid[None], s.reshape(n_rs, BAND, nk), NEG).reshape(rows_q, nk)
        m_blk = jnp.max(s, axis=-1, keepdims=True)
        if first:
            if has_sink:
                head0 = (pl.program_id(1) * kv_per_block + kvj) * heads_per_kv + hs0
                m_old = jnp.concatenate(
                    [jnp.full((BAND, LANES), sink_ref[head0 + h] * log2e, F32) for h in range(n_rs)], axis=0)
                l_old = 1.0
            else:
                m_old, l_old = jnp.full((rows_q, LANES), NEG, F32), 0.0
        else:
            m_old = jnp.concatenate([gather(m_scr.at[hs0 + h], q_runs) for h in range(n_rs)], axis=0)
        m_new = jnp.maximum(m_old, m_blk)
        p = jnp.exp2(s - jnp.concatenate([m_new] * (nk // LANES), axis=1))
        pv = jnp.dot(p.astype(BF16), vaug, preferred_element_type=F32)
        alpha = jnp.exp2(m_old - m_new)
        if first:
            acc = pv[:, :LANES]
            l_new = pv[:, LANES:] + alpha * l_old
        else:
            acc = alpha * jnp.concatenate([gather(acc_scr.at[hs0 + h], q_runs) for h in range(n_rs)], axis=0)
            acc = acc + pv[:, :LANES]
            l_new = alpha * jnp.concatenate([gather(l_scr.at[hs0 + h], q_runs) for h in range(n_rs)], axis=0)
            l_new = l_new + pv[:, LANES:]
        if last:
            out = acc / l_new
            for u in range(units_per_blk):
                lo = (unit0 + u) * LANES
                if heads_per_unit == 1:
                    piece = out[u * BAND:(u + 1) * BAND]
                else:
                    piece = jnp.where(lane_q < d, out[2 * u * BAND:(2 * u + 1) * BAND],
                                      out[(2 * u + 1) * BAND:(2 * u + 2) * BAND])

                def store_o(rows, val, lo=lo):
                    o_ref[rows, lo:lo + LANES] = val

                scatter(store_o, q_runs, piece.astype(o_ref.dtype))
        else:
            for h in range(n_rs):
                for scr, val in ((acc_scr, acc), (l_scr, l_new), (m_scr, m_new)):
                    def store_s(rows, v, scr=scr, h=hs0 + h):
                        scr[h, rows, :] = v

                    scatter(store_s, q_runs, val[h * BAND:(h + 1) * BAND])

    for p_idx, (window, r) in enumerate(patterns):
        assert window // r == BAND and n_classes % r == 0
        s = n_classes // r
        qlen = BAND // s
        nblk = npc // qlen
        valid_first, valid_rest = band_mask(s, True), band_mask(s, False)
        streams = [(kvj, ug, cr) for kvj in range(kv_per_block) for ug in range(head_split) for cr in range(r)]

        def one_block(kvj, ug, cr, m, first_blk, p_idx=p_idx, r=r, s=s, qlen=qlen,
                      valid_first=valid_first, valid_rest=valid_rest):
            base = [npc * (cr + r * a) for a in range(s)]
            if first_blk:
                q_runs = [(b0, qlen) for b0 in base]
                block(p_idx, kvj, ug, q_runs, q_runs, valid_first)
            else:
                off = pl.multiple_of(qlen * m, qlen)
                q_runs = [(b0 + off, qlen) for b0 in base]
                k_runs = [(b0 + off - qlen, 2 * qlen) for b0 in base]
                block(p_idx, kvj, ug, q_runs, k_runs, valid_rest)

        for kvj, ug, cr in streams:
            one_block(kvj, ug, cr, 0, True)
        if nblk > 1:
            unroll = 1
            if len(streams) < 3:
                unroll = next((u for u in (2, 3, 4, 5) if (nblk - 1) % u == 0 and len(streams) * u >= 3), 1)

            def body(it, _, streams=streams, unroll=unroll, one_block=one_block):
                for u in range(unroll):
                    for kvj, ug, cr in streams:
                        one_block(kvj, ug, cr, 1 + it * unroll + u, False)
                return 0

            lax.fori_loop(0, (nblk - 1) // unroll, body, 0)


def _class_major_positions(seq, n_classes):
    return jnp.arange(seq).reshape(seq // n_classes, n_classes).T.reshape(seq)


def _band_attention(y3, sinks, *, patterns, n_classes, d, n_q_heads, n_kv_heads, q_col, k_col, v_col, name,
                    head_split=1):
    nb, seq, _ = y3.shape
    kv_per_block = LANES // d
    n_kvb = n_kv_heads // kv_per_block
    qw = n_q_heads * d // n_kvb
    n_rs = n_q_heads // n_kv_heads
    cos, sin = _rope_tables(_class_major_positions(seq, n_classes), d)
    kern = functools.partial(_band_attn_kernel, patterns=patterns, d=d, kv_per_block=kv_per_block,
                             head_split=head_split, n_classes=n_classes, seq=seq, scale=d ** -0.5,
                             has_sink=sinks is not None)
    in_specs = [pl.BlockSpec((None, seq, qw), lambda b, j, *_: (b, 0, q_col // qw + j)),
                pl.BlockSpec((None, seq, LANES), lambda b, j, *_: (b, 0, k_col // LANES + j)),
                pl.BlockSpec((None, seq, LANES), lambda b, j, *_: (b, 0, v_col // LANES + j)),
                pl.BlockSpec((seq, LANES), lambda b, j, *_: (0, 0)),
                pl.BlockSpec((seq, LANES), lambda b, j, *_: (0, 0))]
    out_specs = [pl.BlockSpec((None, seq, qw), lambda b, j, *_: (b, 0, j)),
                 pl.BlockSpec((None, seq, LANES), lambda b, j, *_: (b, 0, j))]
    out_shape = [jax.ShapeDtypeStruct((nb, seq, n_q_heads * d), BF16),
                 jax.ShapeDtypeStruct((nb, seq, n_kv_heads * d), F32)]
    scratch = []
    if len(patterns) > 1:
        scratch = [pltpu.VMEM((n_rs, seq, LANES), F32) for _ in range(3)] + [pltpu.VMEM((seq, qw), F32)]
    assert q_col % qw == 0 and k_col % LANES == 0 and v_col % LANES == 0
    if sinks is None:
        gs = pl.GridSpec(grid=(nb, n_kvb), in_specs=in_specs, out_specs=out_specs, scratch_shapes=scratch)
        args = (y3, y3, y3, cos, sin)
    else:
        gs = pltpu.PrefetchScalarGridSpec(num_scalar_prefetch=1, grid=(nb, n_kvb), in_specs=in_specs,
                                          out_specs=out_specs, scratch_shapes=scratch)
        args = (sinks.astype(F32), y3, y3, y3, cos, sin)
    return pl.pallas_call(kern, grid_spec=gs, out_shape=out_shape,
                          compiler_params=_cparams(("parallel", "arbitrary")), name=name)(*args)


def _mlstm_kernel(q_ref, k_ref, v_ref, og_ref, igc_ref, lfc_ref, igr_ref, lfr_ref, hn_ref,
                  o_ref, c_ref, n_ref, m_ref, *, rows_per_step, chunk, heads):
    L = chunk
    ri = lax.broadcasted_iota(jnp.int32, (L, L), 0)
    ci = lax.broadcasted_iota(jnp.int32, (L, L), 1)
    tril = ri >= ci
    tril_f = tril.astype(F32)
    triu_f = (ri <= ci).astype(F32)
    hp = lax.Precision.HIGHEST

    @pl.when(pl.program_id(2) == 0)
    def _():
        c_ref[...] = jnp.zeros_like(c_ref)
        n_ref[...] = jnp.zeros_like(n_ref)
        m_ref[...] = jnp.zeros_like(m_ref)

    def head_step(h, c):
        rows = pl.ds(pl.multiple_of(c * L, L), L)
        kq, kv = slice(h * A_DK, (h + 1) * A_DK), slice(h * A_DV, (h + 1) * A_DV)
        m = m_ref[h]
        qc = q_ref[rows, kq] * (A_DK ** -0.5)
        kc = k_ref[rows, kq]
        vc = v_ref[rows, kv].astype(BF16)
        icol, fcol = igc_ref[h, rows, :], lfc_ref[h, rows, :]
        irow, frow = igr_ref[h, c], lfr_ref[h, c]
        bcol = jnp.dot(tril_f, jnp.broadcast_to(fcol, (L, LANES)), precision=hp,
                       preferred_element_type=F32)[:, :1]
        brow = jnp.dot(jnp.broadcast_to(frow, (8, L)), triu_f, precision=hp, preferred_element_type=F32)[:1, :]
        acol, arow = icol - bcol, irow - brow
        log_d = jnp.where(tril, bcol + arow, NEG)
        inter = bcol + m
        mt = jnp.maximum(inter, jnp.max(log_d, axis=-1, keepdims=True))
        dmat = jnp.exp(log_d - mt)
        wi = jnp.exp(inter - mt)
        qb = qc.astype(BF16)
        sc = lax.dot_general(qb, kc.astype(BF16), (((1,), (1,)), ((), ())), preferred_element_type=F32) * dmat
        cm = c_ref[h]
        nv = n_ref[h]
        num = (jnp.dot(sc.astype(BF16), vc, preferred_element_type=F32)
               + wi * jnp.dot(qb, cm.astype(BF16), preferred_element_type=F32))
        den = jnp.sum(sc, axis=-1, keepdims=True) + wi * jnp.sum(qc * nv, axis=-1, keepdims=True)
        hc = num / jnp.maximum(jnp.abs(den), jnp.exp(-mt))
        hn = hc * lax.rsqrt(jnp.mean(hc * hc, axis=-1, keepdims=True) + NORM_EPS) * hn_ref[h]
        o_ref[rows, kv] = (hn * jax.nn.sigmoid(og_ref[rows, kv])).astype(o_ref.dtype)
        blast = bcol[L - 1:L, :]
        m_new = jnp.maximum(blast + m, jnp.max(blast + arow, axis=-1, keepdims=True))
        w_c = jnp.exp(blast + m - m_new)
        kw = kc * jnp.exp(blast + acol - m_new)
        c_ref[h] = w_c * cm + lax.dot_general(kw.astype(BF16), vc, (((0,), (0,)), ((), ())),
                                              preferred_element_type=F32)
        n_ref[h] = w_c * nv + jnp.sum(kw, axis=0, keepdims=True)
        m_ref[h] = m_new

    def step(c, _):
        for h in range(heads):
            head_step(h, c)
        return 0

    lax.fori_loop(0, rows_per_step // L, step, 0)


def _mlstm_prompt(y3, ig, lf, head_norm, *, q_col, k_col, v_col, o_col, heads=A_HEADS, rows_per_step=512):
    nb, seq, _ = y3.shape
    L = A_CHUNK
    nc = seq // L
    rows_per_step = min(rows_per_step, seq)
    cps = rows_per_step // L
    igc = jnp.transpose(ig, (0, 2, 1))[..., None]
    lfc = jnp.transpose(lf, (0, 2, 1))[..., None]
    igr = igc.reshape(nb, A_HEADS, nc, 1, L)
    lfr = lfc.reshape(nb, A_HEADS, nc, 1, L)
    hn = head_norm.reshape(A_HEADS, 1, A_DV).astype(F32)
    col = lambda c0, w: pl.BlockSpec((None, rows_per_step, heads * w),
                                     lambda b, g, s: (b, s, c0 // (heads * w) + g))
    gcol = pl.BlockSpec((None, heads, rows_per_step, 1), lambda b, g, s: (b, g, s, 0))
    grow = pl.BlockSpec((None, heads, cps, 1, L), lambda b, g, s: (b, g, s, 0, 0))
    state = lambda *tail: pl.BlockSpec((None, heads) + tail, lambda b, g, s: (b, g) + (0,) * len(tail))
    assert seq % rows_per_step == 0 and A_HEADS % heads == 0 and all(
        c0 % (heads * w) == 0 for c0, w in ((q_col, A_DK), (k_col, A_DK), (v_col, A_DV), (o_col, A_DV)))
    return pl.pallas_call(
        functools.partial(_mlstm_kernel, rows_per_step=rows_per_step, chunk=L, heads=heads),
        grid=(nb, A_HEADS // heads, seq // rows_per_step),
        in_specs=[col(q_col, A_DK), col(k_col, A_DK), col(v_col, A_DV), col(o_col, A_DV),
                  gcol, gcol, grow, grow,
                  pl.BlockSpec((heads, 1, A_DV), lambda b, g, s: (g, 0, 0))],
        out_specs=[pl.BlockSpec((None, rows_per_step, heads * A_DV), lambda b, g, s: (b, s, g)),
                   state(A_DK, A_DV), state(1, A_DK), state(1, 1)],
        out_shape=[jax.ShapeDtypeStruct((nb, seq, A_HEADS * A_DV), BF16),
                   jax.ShapeDtypeStruct((nb, A_HEADS, A_DK, A_DV), F32),
                   jax.ShapeDtypeStruct((nb, A_HEADS, 1, A_DK), F32),
                   jax.ShapeDtypeStruct((nb, A_HEADS, 1, 1), F32)],
        compiler_params=_cparams(("parallel", "parallel", "arbitrary")),
        name="mlstm_prompt",
    )(y3, y3, y3, y3, igc, lfc, igr, lfr, hn)


def _mlstm_step_kernel(q_ref, k_ref, v_ref, og_ref, ig_ref, lf_ref, c0_ref, n0_ref, m0_ref, hn_ref,
                       o_ref, c1_ref, n1_ref, m1_ref):
    for h in range(A_HEADS):
        q = q_ref[h] * (A_DK ** -0.5)
        k = k_ref[h]
        v = v_ref[h]
        c0 = c0_ref[h]
        n0 = n0_ref[h]
        ig, lf, m0 = ig_ref[h], lf_ref[h], m0_ref[h]
        inter = lf + m0
        mt = jnp.maximum(inter, ig)
        dm = jnp.exp(ig - mt)
        wi = jnp.exp(inter - mt)
        sc = jnp.sum(q * k, axis=0, keepdims=True) * dm
        qc = jnp.sum(c0 * q, axis=0, keepdims=True)
        num = sc * v + wi * qc
        den = sc + wi * jnp.sum(q * n0, axis=0, keepdims=True)
        hc = num / jnp.maximum(jnp.abs(den), jnp.exp(-mt))
        hn = hc * lax.rsqrt(jnp.mean(hc * hc, axis=-1, keepdims=True) + NORM_EPS) * hn_ref[h]
        o_ref[h] = (hn * jax.nn.sigmoid(og_ref[h])).astype(o_ref.dtype)
        c1_ref[h] = wi * c0 + (k * dm) * v
        n1_ref[h] = wi * n0 + dm * k
        m1_ref[h] = mt


def _mlstm_sample(y, ig, lf, head_norm, c0, n0, m0):
    nb = y.shape[0]
    col = lambda a: a.reshape(nb, A_HEADS, A_DK, 1)
    row = lambda a: a.reshape(nb, A_HEADS, 1, A_DV)
    sca = lambda a: a.astype(F32).reshape(nb, A_HEADS, 1, 1)
    cspec = pl.BlockSpec((None, A_HEADS, A_DK, 1), lambda b: (b, 0, 0, 0))
    rspec = pl.BlockSpec((None, A_HEADS, 1, A_DV), lambda b: (b, 0, 0, 0))
    sspec = pl.BlockSpec((None, A_HEADS, 1, 1), lambda b: (b, 0, 0, 0))
    mspec = pl.BlockSpec((None, A_HEADS, A_DK, A_DV), lambda b: (b, 0, 0, 0))
    o, c1, n1, m1 = pl.pallas_call(
        _mlstm_step_kernel,
        grid=(nb,),
        in_specs=[cspec, cspec, rspec, rspec, sspec, sspec, mspec, cspec, sspec,
                  pl.BlockSpec((A_HEADS, 1, A_DV), lambda b: (0, 0, 0))],
        out_specs=[rspec, mspec, cspec, sspec],
        out_shape=[jax.ShapeDtypeStruct((nb, A_HEADS, 1, A_DV), F32),
                   jax.ShapeDtypeStruct((nb, A_HEADS, A_DK, A_DV), F32),
                   jax.ShapeDtypeStruct((nb, A_HEADS, A_DK, 1), F32),
                   jax.ShapeDtypeStruct((nb, A_HEADS, 1, 1), F32)],
        compiler_params=_cparams(("parallel",)),
        name="mlstm_step",
    )(col(y[:, A_Q:A_Q + A_HEADS * A_DK]), col(y[:, A_K:A_K + A_HEADS * A_DK]),
      row(y[:, A_V:A_V + A_HEADS * A_DV]), row(y[:, A_O:A_O + A_HEADS * A_DV]),
      sca(ig), sca(lf), c0.astype(F32), col(n0.astype(F32)), sca(m0),
      head_norm.astype(F32).reshape(A_HEADS, 1, A_DV))
    return (o.reshape(nb, A_HEADS * A_DV), c1, n1.reshape(nb, A_HEADS, A_DK), m1.reshape(nb, A_HEADS))


def _decode_attn_kernel(*refs, n_pat, d, group, scale, has_sink):
    q_ref, kn_ref, vn_ref, cos_ref, sin_ref = refs[:5]
    refs = refs[5:]
    if has_sink:
        sink_ref, refs = refs[0], refs[1:]
    k_refs, v_refs = refs[:n_pat], refs[n_pat:2 * n_pat]
    o_ref, knew_ref = refs[2 * n_pat:]
    n_seg = LANES // d
    cos, sin = cos_ref[...], sin_ref[...]
    kn = _rope_lanes(kn_ref[...], cos, sin, d)
    knew_ref[...] = kn
    vn = vn_ref[...]
    lane = lax.broadcasted_iota(jnp.int32, kn.shape, 1)

    def seg_sum(p):
        if n_seg == 1:
            return jnp.sum(p, axis=-1, keepdims=True)
        lane_p = lax.broadcasted_iota(jnp.int32, p.shape, p.ndim - 1)
        lo = jnp.sum(jnp.where(lane_p < d, p, 0.0), axis=-1, keepdims=True)
        hi = jnp.sum(jnp.where(lane_p >= d, p, 0.0), axis=-1, keepdims=True)
        return jnp.where(lane_p < d, lo, hi)

    for g in range(group):
        qg = _rope_lanes(q_ref[g], cos, sin, d)
        s_new = seg_sum(kn * qg) * scale
        s = [seg_sum(k_refs[p][...] * qg[None]) * scale for p in range(n_pat)]
        m = s_new
        for p in range(n_pat):
            m = jnp.maximum(m, jnp.max(s[p], axis=0))
        if has_sink:
            m = jnp.maximum(m, sink_ref[g])
        e_new = jnp.exp(s_new - m) * float(n_pat)
        l = e_new
        acc = e_new * vn
        for p in range(n_pat):
            e = jnp.exp(s[p] - m[None])
            l = l + jnp.sum(e, axis=0)
            acc = acc + jnp.sum(e * v_refs[p][...], axis=0)
        if has_sink:
            l = l + jnp.exp(sink_ref[g] - m)
        o_ref[g] = (acc / l).astype(o_ref.dtype)


def _decode_attention(q, k_new, v_new, k_cache, v_cache, sinks, pos, *, patterns, d, name):
    nb, n_heads, _ = q.shape
    n_kv, cache_len = k_cache.shape[2], k_cache.shape[1]
    group = n_heads // n_kv
    n_seg = LANES // d
    r_rows = n_kv // n_seg
    to_tiles = lambda a, inner: a.reshape(nb, r_rows, n_seg, inner, d).transpose(0, 3, 1, 2, 4).reshape(
        nb, inner, r_rows, LANES)
    qt = to_tiles(q, group)
    knt = k_new.reshape(nb, r_rows, LANES)
    vnt = v_new.reshape(nb, r_rows, LANES)
    cos, sin = _rope_tables(pos, d)
    kc = k_cache.reshape(nb, cache_len, r_rows, LANES)
    vc = v_cache.reshape(nb, cache_len, r_rows, LANES)
    tile = pl.BlockSpec((None, r_rows, LANES), lambda b: (b, 0, 0))
    one = pl.BlockSpec((1, LANES), lambda b: (0, 0))
    in_specs = [pl.BlockSpec((None, group, r_rows, LANES), lambda b: (b, 0, 0, 0)), tile, tile, one, one]
    args = [qt, knt, vnt, cos, sin]
    if sinks is not None:
        sk = jnp.broadcast_to(sinks.astype(F32).reshape(r_rows, n_seg, group, 1), (r_rows, n_seg, group, d))
        args.append(sk.transpose(2, 0, 1, 3).reshape(group, r_rows, LANES))
        in_specs.append(pl.BlockSpec((group, r_rows, LANES), lambda b: (0, 0, 0)))
    views = []
    for window, r in patterns:
        assert window // r == BAND and cache_len % (BAND * r) == 0
        blk = cache_len // r // BAND - 1
        views.append(pl.BlockSpec((None, BAND, None, r_rows, LANES), lambda b, blk=blk: (b, blk, 0, 0, 0)))
    in_specs += views + views
    args += [kc.reshape(nb, cache_len // r, r, r_rows, LANES) for _, r in patterns]
    args += [vc.reshape(nb, cache_len // r, r, r_rows, LANES) for _, r in patterns]
    o, kn = pl.pallas_call(
        functools.partial(_decode_attn_kernel, n_pat=len(patterns), d=d, group=group, scale=d ** -0.5,
                          has_sink=sinks is not None),
        grid=(nb,),
        in_specs=in_specs,
        out_specs=[pl.BlockSpec((None, group, r_rows, LANES), lambda b: (b, 0, 0, 0)), tile],
        out_shape=[jax.ShapeDtypeStruct((nb, group, r_rows, LANES), F32),
                   jax.ShapeDtypeStruct((nb, r_rows, LANES), F32)],
        compiler_params=_cparams(("parallel",)),
        name=name,
    )(*args)
    o = o.reshape(nb, group, r_rows, n_seg, d).transpose(0, 2, 3, 1, 4).reshape(nb, n_heads * d)
    return o, kn.reshape(nb, n_kv, d)


def _shift_kernel(c_ref, new_ref, o_ref, *, length, chunk):
    n_full = (length - 1) // chunk

    def body(i, _):
        o_ref[pl.ds(i * chunk, chunk)] = c_ref[pl.ds(i * chunk + 1, chunk)]
        return 0

    lax.fori_loop(0, n_full, body, 0)
    done = n_full * chunk
    if done < length - 1:
        o_ref[done:length - 1] = c_ref[done + 1:length]
    o_ref[length - 1] = new_ref[...]


def _shift_cache(cache, new, name):
    nb, length, n_kv, d = cache.shape
    return pl.pallas_call(
        functools.partial(_shift_kernel, length=length, chunk=min(64, length - 1)),
        grid=(nb,),
        in_specs=[pl.BlockSpec((None, length, n_kv, d), lambda b: (b, 0, 0, 0)),
                  pl.BlockSpec((None, n_kv, d), lambda b: (b, 0, 0))],
        out_specs=pl.BlockSpec((None, length, n_kv, d), lambda b: (b, 0, 0, 0)),
        out_shape=jax.ShapeDtypeStruct(cache.shape, cache.dtype),
        compiler_params=_cparams(("parallel",)),
        name=name,
    )(cache, new.astype(cache.dtype))


N_IN_EVEN = 2 * A_HEADS * A_DK + 2 * A_HEADS * A_DV + 2 * A_HEADS + (B_HEADS + 2 * B_KV_HEADS) * B_HEAD_DIM
EVEN_IN = N_IN_EVEN // 512 * 512
A_Q, A_K, A_V, A_O = 0, 1024, 2048, 4096
GATE_COL = 6144
B_COL = GATE_COL + 2 * A_HEADS
B_Q, B_K, B_V = 0, 2048, 2304
C_Q, C_K, C_V = 0, 4096, 5120
SWA_PATTERN = ((B_WINDOW, 1),)


def _even_projection(hp, hs, w_in, j):
    ys = _matmul2([(hp, hs, jnp.swapaxes(w_in, 1, 2), j, 0)], EVEN_IN, F32, 1024, 512, "even_in",
                  w_rows_are_outputs=True)
    n_tail = N_IN_EVEN - EVEN_IN
    w_tail = jnp.pad(w_in[j, :, EVEN_IN:], ((0, 0), (0, LANES - n_tail)))[None]
    tails = _matmul2([(hp, hs, w_tail, 0, 0)], LANES, F32, 1024, LANES, "even_in_tail")
    return [(y, y[:, GATE_COL:B_COL], jnp.concatenate([y[:, B_COL:], t[:, :n_tail]], axis=-1))
            for y, t in zip(ys, tails)]


def _gates(yg, gate_b):
    gb = gate_b.astype(F32)
    ig = yg[..., :A_HEADS] + gb[:A_HEADS]
    lf = jax.nn.log_sigmoid(yg[..., A_HEADS:] + gb[A_HEADS:])
    return ig, lf


def _even_mixer_prompt(proj, nb, seq, gate_b, head_norm, sinks):
    y, yg, yb = proj
    y3, yb3 = y.reshape(nb, seq, -1), yb.reshape(nb, seq, -1)
    ig, lf = _gates(yg.reshape(nb, seq, -1), gate_b)
    ha, c1, n1, m1 = _mlstm_prompt(y3, ig, lf, head_norm, q_col=A_Q, k_col=A_K, v_col=A_V, o_col=A_O)
    ob, kb = _band_attention(yb3, sinks, patterns=SWA_PATTERN, n_classes=1, d=B_HEAD_DIM, n_q_heads=B_HEADS,
                             n_kv_heads=B_KV_HEADS, q_col=B_Q, k_col=B_K, v_col=B_V, name="swa_prompt",
                             head_split=2)
    kbuf = kb[:, -B_WINDOW:].reshape(nb, -1, B_KV_HEADS, B_HEAD_DIM)
    vbuf = yb3[:, -B_WINDOW:, B_V:].reshape(nb, -1, B_KV_HEADS, B_HEAD_DIM)
    state = (c1, n1.reshape(nb, A_HEADS, A_DK), m1.reshape(nb, A_HEADS), kbuf, vbuf)
    return ha.reshape(nb * seq, -1), ob.reshape(nb * seq, -1), state


def _even_mixer_sample(proj, gate_b, head_norm, sinks, state):
    y, yg, yb = proj
    nb = y.shape[0]
    ig, lf = _gates(yg, gate_b)
    c0, n0, m0, k_old, v_old = state
    ha, c1, n1, m1 = _mlstm_sample(y, ig, lf, head_norm, c0, n0, m0)
    q = yb[:, B_Q:B_K].reshape(nb, B_HEADS, B_HEAD_DIM)
    k_new = yb[:, B_K:B_V].reshape(nb, B_KV_HEADS, B_HEAD_DIM)
    v_new = yb[:, B_V:].reshape(nb, B_KV_HEADS, B_HEAD_DIM)
    ob, k_rot = _decode_attention(q, k_new, v_new, k_old, v_old, sinks, PAST_LEN + jnp.arange(1),
                                  patterns=SWA_PATTERN, d=B_HEAD_DIM, name="swa_step")
    state = (c1, n1, m1, _shift_cache(k_old, k_rot, "swa_k_shift"), _shift_cache(v_old, v_new, "swa_v_shift"))
    return ha.astype(BF16), ob.astype(BF16), state


def _even_mixer(hp, hs, nb, seq, j, w_in, gate_b, head_norm, sinks, w_out, state):
    proj_p, proj_s = _even_projection(hp, hs, w_in, j)
    ha_p, ob_p, st_p = _even_mixer_prompt(proj_p, nb, seq, gate_b, head_norm, sinks)
    ha_s, ob_s, st_s = _even_mixer_sample(proj_s, gate_b, head_norm, sinks, state)
    outs = _matmul2([(ha_p, ha_s, w_out, j, 0), (ob_p, ob_s, w_out, j, 1)], D_MODEL, F32, 1024, 512, "even_out")
    return outs, (st_p, st_s)


def _odd_mixer(hp, hs, nb, seq, j, w_in, w_out, state):
    out_p, st_p = _odd_mixer_prompt(hp, nb, seq, j, w_in, w_out)
    out_s, st_s = _odd_mixer_sample(hs, j, w_in, w_out, state)
    return (out_p, out_s), (st_p, st_s)


C_CLASSES = max(r for _, r in C_PATTERNS)


def _odd_mixer_prompt(h2, nb, seq, j, w_in, w_out):
    y3 = _matmul_regroup(h2, w_in, j, w_in.shape[2], nb, C_CLASSES, True, 1024, 512, "odd_in").reshape(nb, seq, -1)
    o, kr = _band_attention(y3, None, patterns=C_PATTERNS, n_classes=C_CLASSES, d=C_HEAD_DIM, n_q_heads=C_HEADS,
                            n_kv_heads=C_KV_HEADS, q_col=C_Q, k_col=C_K, v_col=C_V, name="dil_prompt")
    out = _matmul_regroup(o.reshape(nb, C_CLASSES, seq // C_CLASSES, -1), w_out, j, D_MODEL, nb, C_CLASSES, False,
                          1024, 512, "odd_out")
    natural = lambda a: a.reshape(nb, C_CLASSES, seq // C_CLASSES, C_KV_HEADS, C_HEAD_DIM).transpose(
        0, 2, 1, 3, 4).reshape(nb, seq, C_KV_HEADS, C_HEAD_DIM)
    return out, (natural(kr), natural(y3[:, :, C_V:]))


def _odd_mixer_sample(h2, j, w_in, w_out, state):
    nb = h2.shape[0]
    y = _matmul([(h2, w_in, j, 0)], w_in.shape[2], F32, nb, 512, "odd_in_s")
    q = y[:, C_Q:C_K].reshape(nb, C_HEADS, C_HEAD_DIM)
    k_new = y[:, C_K:C_V].reshape(nb, C_KV_HEADS, C_HEAD_DIM)
    v_new = y[:, C_V:].reshape(nb, C_KV_HEADS, C_HEAD_DIM)
    k_old, v_old = state
    o, k_rot = _decode_attention(q, k_new, v_new, k_old, v_old, None, PAST_LEN + jnp.arange(1),
                                 patterns=C_PATTERNS, d=C_HEAD_DIM, name="dil_step")
    out = _matmul([(o.astype(BF16), w_out, j, 0)], D_MODEL, F32, nb, 512, "odd_out_s")
    return out, (_shift_cache(k_old, k_rot, "dil_k_shift"), _shift_cache(v_old, v_new, "dil_v_shift"))


def _trunk(xs, rows, mixers, norms, ffn):
    g_mix_pre, g_mix_post, g_ffn_pre, g_ffn_post = norms
    wg, wu, wd = ffn
    xs = list(xs)
    hs = [_prenorm(r, x, g_mix_pre[0], 0, MOD_SC1, MOD_SH1) for r, x in zip(rows, xs)]
    states = []
    for li in range(DEPTH):
        ys, state = mixers[li](*hs)
        states.append(state)
        for g in range(2):
            xs[g], hs[g] = _postnorm(rows[g], xs[g], ys[g], g_mix_post[li], li, MOD_GT1,
                                     (g_ffn_pre[li], li, MOD_SC2, MOD_SH2))
        acts = _matmul2([(hs[0], hs[1], (wg, wu), li, 0)], wg.shape[2], BF16, 1024, 256, "swiglu", swiglu=True)
        fs = [_matmul([(a, wd, li, 0)], D_MODEL, F32, 512, 512, "ffn_down") for a in acts]
        nxt = (g_mix_pre[li + 1], li + 1, MOD_SC1, MOD_SH1) if li + 1 < DEPTH else None
        for g in range(2):
            xs[g], hs[g] = _postnorm(rows[g], xs[g], fs[g], g_ffn_post[li], li, MOD_GT2, nxt)
    return xs, states


def kernel(x_prompt, x_sample, state_l0_mlstm_c, state_l0_mlstm_n, state_l0_mlstm_m, cache_l0_swa_k, cache_l0_swa_v, cache_l1_dil_k, cache_l1_dil_v, state_l2_mlstm_c, state_l2_mlstm_n, state_l2_mlstm_m, cache_l2_swa_k, cache_l2_swa_v, cache_l3_dil_k, cache_l3_dil_v, c_prompt, c_sample, ada_w, ada_b, norm_mix_pre, norm_mix_post, norm_ffn_pre, norm_ffn_post, even_w_in, even_gate_b, even_head_norm, even_sinks, even_w_out, odd_w_in, odd_w_out, ffn_w_gate, ffn_w_up, ffn_w_down):
    past = {0: (state_l0_mlstm_c, state_l0_mlstm_n, state_l0_mlstm_m, cache_l0_swa_k, cache_l0_swa_v),
            1: (cache_l1_dil_k, cache_l1_dil_v),
            2: (state_l2_mlstm_c, state_l2_mlstm_n, state_l2_mlstm_m, cache_l2_swa_k, cache_l2_swa_v),
            3: (cache_l3_dil_k, cache_l3_dil_v)}
    nbp, seq, dm = x_prompt.shape
    nbs = x_sample.shape[0]
    c_all = jnp.concatenate([c_sample, c_prompt], axis=0)
    pad = (-c_all.shape[0]) % 16
    sc_all = jnp.pad(jax.nn.silu(c_all), ((0, pad), (0, 0))).astype(BF16)
    mod_all = _ada(sc_all, ada_w, ada_b)
    mixers = []
    for li in range(DEPTH):
        j = li // 2
        if li % 2 == 0:
            mw = (j, even_w_in, even_gate_b[j], even_head_norm[j], even_sinks[j], even_w_out, past[li])
            mixers.append(lambda hp, hs, mw=mw: _even_mixer(hp, hs, nbp, seq, *mw))
        else:
            mw = (j, odd_w_in, odd_w_out, past[li])
            mixers.append(lambda hp, hs, mw=mw: _odd_mixer(hp, hs, nbp, seq, *mw))
    norms = (norm_mix_pre, norm_mix_post, norm_ffn_pre, norm_ffn_post)
    ffn = (ffn_w_gate, ffn_w_up, ffn_w_down.astype(BF16))
    rows = (_Rows(mod_all, nbp * seq, seq, nbs, 256), _Rows(mod_all, nbs, 1, 0, nbs))
    (y_p, y_s), states = _trunk((x_prompt.reshape(nbp * seq, dm), x_sample.reshape(nbs, dm)), rows, mixers,
                                norms, ffn)
    y_p, y_s = y_p.reshape(x_prompt.shape), y_s.reshape(x_sample.shape)
    new_p, new_s = [s[0] for s in states], [s[1] for s in states]
    (p0c, p0n, p0m, p0k, p0v), (p1k, p1v), (p2c, p2n, p2m, p2k, p2v), (p3k, p3v) = new_p
    (s0c, s0n, s0m, s0k, s0v), (s1k, s1v), (s2c, s2n, s2m, s2k, s2v), (s3k, s3v) = new_s
    return (y_p, y_s,
            p0c, p0n, p0m, p0k, p0v, p1k, p1v, p2c, p2n, p2m, p2k, p2v, p3k, p3v,
            s0c, s0n, s0m, s0k, s0v, s1k, s1v, s2c, s2n, s2m, s2k, s2v, s3k, s3v)
```

```python
import functools
import math

import jax
import jax.numpy as jnp
import numpy as np
from jax import lax
from jax.experimental import pallas as pl
from jax.experimental.pallas import tpu as pltpu

F32 = jnp.float32
BF16 = jnp.bfloat16

D_MODEL = 4096
DEPTH = 4
PAST_LEN = 8192
A_HEADS, A_DK, A_DV, A_CHUNK = 4, 256, 512, 64
B_HEADS, B_KV_HEADS, B_HEAD_DIM, B_WINDOW = 32, 4, 64, 128
C_HEADS, C_KV_HEADS, C_HEAD_DIM = 32, 8, 128
C_PATTERNS = ((128, 1), (512, 4), (2048, 16))
BAND = 128
ROPE_THETA = 10000.0
NORM_EPS = 1e-6
LANES = 128
NEG = -1e30
VMEM_LIMIT = 56 * 1024 * 1024


def _cparams(sem):
    return pltpu.CompilerParams(dimension_semantics=sem, vmem_limit_bytes=VMEM_LIMIT)


def _mm_kernel(*refs, n_in):
    o_ref = refs[-1]
    acc = None
    for i in range(n_in):
        x = refs[2 * i][...].astype(BF16)
        w = refs[2 * i + 1][...].astype(BF16)
        d = jnp.dot(x, w, preferred_element_type=F32)
        acc = d if acc is None else acc + d
    o_ref[...] = acc.astype(o_ref.dtype)


def _w_spec(w, layer, kblk, k, tn):
    if w.ndim == 3:
        return pl.BlockSpec((None, k, tn), lambda i, j: (layer, kblk, j))
    return pl.BlockSpec((k, tn), lambda i, j: (kblk, j))


def _matmul(pairs, n, out_dtype, tm, tn, name):
    m = pairs[0][0].shape[0]
    tm = min(tm, m)
    tn = min(tn, n)
    assert m % tm == 0 and n % tn == 0, (m, n, tm, tn)
    in_specs, args = [], []
    for x, w, layer, kblk in pairs:
        k = x.shape[1]
        in_specs += [pl.BlockSpec((tm, k), lambda i, j: (i, 0)), _w_spec(w, layer, kblk, k, tn)]
        args += [x, w]
    return pl.pallas_call(
        functools.partial(_mm_kernel, n_in=len(pairs)),
        grid=(m // tm, n // tn),
        in_specs=in_specs,
        out_specs=pl.BlockSpec((tm, tn), lambda i, j: (i, j)),
        out_shape=jax.ShapeDtypeStruct((m, n), out_dtype),
        compiler_params=_cparams(("parallel", "arbitrary")),
        name=name,
    )(*args)


def _mm2_kernel(*refs, n_in, swiglu, w_rows_are_outputs):
    n_w = 2 if swiglu else 1
    per = 2 + n_w
    ins = refs[:n_in * per]
    op_ref, os_ref = refs[n_in * per:]
    w_refs = [ins[(t // n_w) * per + 2 + t % n_w] for t in range(n_in * n_w)]
    ws = [(r[0] if len(r.shape) == 3 else r[...]).astype(BF16) for r in w_refs]

    def compute(xsel, o_ref):
        accs = [None] * n_w
        for p in range(n_in):
            x = ins[p * per + xsel][...]
            for t in range(n_w):
                contract = (((1,), (1 if w_rows_are_outputs else 0,)), ((), ()))
                dt = lax.dot_general(x, ws[p * n_w + t], contract, preferred_element_type=F32)
                accs[t] = dt if accs[t] is None else accs[t] + dt
        r = accs[0] * jax.nn.sigmoid(accs[0]) * accs[1] if swiglu else accs[0]
        o_ref[...] = r.astype(o_ref.dtype)

    compute(0, op_ref)

    @pl.when(pl.program_id(0) == 0)
    def _():
        compute(1, os_ref)


def _matmul2(pairs, n, out_dtype, tm, tn, name, swiglu=False, w_rows_are_outputs=False, w_row0=0):
    m, ms = pairs[0][0].shape[0], pairs[0][1].shape[0]
    assert m % tm == 0 and n % tn == 0, (m, n, tm, tn)
    n_j = n // tn
    in_specs, args = [], []
    for xp, xs, ws, layer, kblk in pairs:
        k = xp.shape[1]
        in_specs += [pl.BlockSpec((tm, k), lambda i, j: (i, 0)), pl.BlockSpec((ms, k), lambda i, j: (0, 0))]
        args += [xp, xs]
        for w in (ws if swiglu else (ws,)):
            if w_rows_are_outputs:
                in_specs.append(pl.BlockSpec(
                    (pl.Element(1), pl.Element(tn), pl.Element(k)),
                    lambda i, j, layer=layer, kblk=kblk: (layer, pl.multiple_of(w_row0 + j * tn, 8), kblk * k)))
            else:
                in_specs.append(pl.BlockSpec((None, k, tn), lambda i, j, layer=layer, kblk=kblk: (layer, kblk, j)))
            args.append(w)
    return pl.pallas_call(
        functools.partial(_mm2_kernel, n_in=len(pairs), swiglu=swiglu, w_rows_are_outputs=w_rows_are_outputs),
        grid=(m // tm, n_j),
        in_specs=in_specs,
        out_specs=[pl.BlockSpec((tm, tn), lambda i, j: (i, j)),
                   pl.BlockSpec((ms, tn), lambda i, j: (0, jnp.where(i == 0, j, n_j - 1)))],
        out_shape=[jax.ShapeDtypeStruct((m, n), out_dtype), jax.ShapeDtypeStruct((ms, n), out_dtype)],
        compiler_params=_cparams(("arbitrary", "arbitrary")),
        name=name,
    )(*args)


def _digit_swap(n):
    p = lax.broadcasted_iota(jnp.int32, (n * n, n * n), 0)
    q = lax.broadcasted_iota(jnp.int32, (n * n, n * n), 1)
    shift = n.bit_length() - 1
    return (q == ((p & (n - 1)) << shift) + (p >> shift)).astype(BF16)


def _mm_regroup_kernel(x_ref, w_ref, o_ref, xp_scr, *, n_cls, to_classes):
    blk = n_cls * n_cls
    tm = xp_scr.shape[0]

    @pl.when(pl.program_id(1) == 0)
    def _():
        perm = _digit_swap(n_cls)
        for k in range(tm // blk):
            if to_classes:
                xs = x_ref[k * blk:(k + 1) * blk, :]
            else:
                xs = jnp.concatenate([x_ref[c, k * n_cls:(k + 1) * n_cls, :] for c in range(n_cls)], axis=0)
            xp_scr[k * blk:(k + 1) * blk, :] = jnp.dot(perm, xs, preferred_element_type=F32).astype(BF16)

    r = jnp.dot(xp_scr[...], w_ref[...].astype(BF16), preferred_element_type=F32)
    if to_classes:
        for k in range(tm // blk):
            for c in range(n_cls):
                o_ref[c, k * n_cls:(k + 1) * n_cls, :] = r[k * blk + c * n_cls:k * blk + (c + 1) * n_cls]
    else:
        o_ref[...] = r


def _matmul_regroup(x, w, layer, n, nb, n_cls, to_classes, tm, tn, name):
    k = x.shape[-1]
    seq = x.shape[0] // nb if to_classes else n_cls * x.shape[2]
    tiles = seq // tm
    assert seq % tm == 0 and tm % (n_cls * n_cls) == 0 and n % tn == 0
    cls_block = lambda cols: (None, n_cls, tm // n_cls, cols)
    if to_classes:
        x_spec = pl.BlockSpec((tm, k), lambda i, j: (i, 0))
        o_spec = pl.BlockSpec(cls_block(tn), lambda i, j: (i // tiles, 0, i % tiles, j))
        o_shape = (nb, n_cls, seq // n_cls, n)
    else:
        x_spec = pl.BlockSpec(cls_block(k), lambda i, j: (i // tiles, 0, i % tiles, 0))
        o_spec = pl.BlockSpec((tm, tn), lambda i, j: (i, j))
        o_shape = (nb * seq, n)
    return pl.pallas_call(
        functools.partial(_mm_regroup_kernel, n_cls=n_cls, to_classes=to_classes),
        grid=(nb * tiles, n // tn),
        in_specs=[x_spec, _w_spec(w, layer, 0, k, tn)],
        out_specs=o_spec,
        out_shape=jax.ShapeDtypeStruct(o_shape, F32),
        scratch_shapes=[pltpu.VMEM((tm, k), BF16)],
        compiler_params=_cparams(("parallel", "arbitrary")),
        name=name,
    )(x, w)


MOD_SH1, MOD_SC1, MOD_GT1, MOD_SH2, MOD_SC2, MOD_GT2 = range(6)


def _rms_rows(x, g):
    return x * lax.rsqrt(jnp.mean(x * x, axis=-1, keepdims=True) + NORM_EPS) * g


def _prenorm_kernel(x_ref, g_ref, sc_ref, sh_ref, h_ref):
    h_ref[...] = (_rms_rows(x_ref[...], g_ref[...]) * (1.0 + sc_ref[...]) + sh_ref[...]).astype(h_ref.dtype)


def _postnorm_kernel(*refs, has_next):
    x_ref, y_ref, gt_ref, gpost_ref = refs[:4]
    xn = x_ref[...] + gt_ref[...] * _rms_rows(y_ref[...], gpost_ref[...])
    if has_next:
        gpre_ref, sc_ref, sh_ref, xo_ref, h_ref = refs[4:]
        h_ref[...] = (_rms_rows(xn, gpre_ref[...]) * (1.0 + sc_ref[...]) + sh_ref[...]).astype(h_ref.dtype)
    else:
        xo_ref = refs[4]
    xo_ref[...] = xn


class _Rows:
    def __init__(self, mod_all, n_rows, rows_per_batch, mod_row0, tm):
        self.dm = mod_all.shape[-1] // 6
        self.tm = min(tm, n_rows)
        self.n_rows = n_rows
        self.per_token = rows_per_batch == 1
        if self.per_token:
            self.mod = mod_all
            assert mod_row0 % self.tm == 0 and n_rows == self.tm
        else:
            self.mod = mod_all.reshape(mod_all.shape[0], mod_all.shape[1], 1, mod_all.shape[2])
            assert rows_per_batch % self.tm == 0
        self.tiles_per_batch = max(rows_per_batch // self.tm, 1)
        self.mod_row0 = mod_row0

    def rows(self):
        return pl.BlockSpec((self.tm, self.dm), lambda i: (i, 0))

    def vec(self, layer, chunk):
        if self.per_token:
            return pl.BlockSpec((None, self.tm, self.dm), lambda i: (layer, self.mod_row0 // self.tm, chunk))
        return pl.BlockSpec((None, None, 1, self.dm),
                            lambda i: (layer, self.mod_row0 + i // self.tiles_per_batch, 0, chunk))


def _gain_spec(dm):
    return pl.BlockSpec((None, 1, dm), lambda i: (0, 0, 0))


def _prenorm(rows, x2, gain, layer, sc_chunk, sh_chunk):
    dm = rows.dm
    return pl.pallas_call(
        _prenorm_kernel,
        grid=(rows.n_rows // rows.tm,),
        in_specs=[rows.rows(), _gain_spec(dm), rows.vec(layer, sc_chunk), rows.vec(layer, sh_chunk)],
        out_specs=rows.rows(),
        out_shape=jax.ShapeDtypeStruct((rows.n_rows, dm), BF16),
        compiler_params=_cparams(("parallel",)),
        name="prenorm",
    )(x2, gain.astype(F32).reshape(1, 1, dm), rows.mod, rows.mod)


def _postnorm(rows, x2, y2, gain_post, layer, gt_chunk, nxt):
    dm = rows.dm
    in_specs = [rows.rows(), rows.rows(), rows.vec(layer, gt_chunk), _gain_spec(dm)]
    args = [x2, y2, rows.mod, gain_post.astype(F32).reshape(1, 1, dm)]
    out_specs = [rows.rows()]
    out_shape = [jax.ShapeDtypeStruct((rows.n_rows, dm), F32)]
    if nxt is not None:
        gain_pre, nl, sc_chunk, sh_chunk = nxt
        in_specs += [_gain_spec(dm), rows.vec(nl, sc_chunk), rows.vec(nl, sh_chunk)]
        args += [gain_pre.astype(F32).reshape(1, 1, dm), rows.mod, rows.mod]
        out_specs.append(rows.rows())
        out_shape.append(jax.ShapeDtypeStruct((rows.n_rows, dm), BF16))
    res = pl.pallas_call(
        functools.partial(_postnorm_kernel, has_next=nxt is not None),
        grid=(rows.n_rows // rows.tm,),
        in_specs=in_specs, out_specs=out_specs, out_shape=out_shape,
        compiler_params=_cparams(("parallel",)),
        name="postnorm",
    )(*args)
    return (res[0], res[1]) if nxt is not None else (res[0], None)


def _ada_kernel(c_ref, w_ref, b_ref, o_ref):
    o_ref[...] = jnp.dot(c_ref[...], w_ref[...].astype(BF16), preferred_element_type=F32) + b_ref[...]


def _ada(sc_all, ada_w, ada_b, tn=512):
    nl, k, n = ada_w.shape
    r = sc_all.shape[0]
    return pl.pallas_call(
        _ada_kernel,
        grid=(nl, n // tn),
        in_specs=[pl.BlockSpec((r, k), lambda l, j: (0, 0)),
                  pl.BlockSpec((None, k, tn), lambda l, j: (l, 0, j)),
                  pl.BlockSpec((None, 1, tn), lambda l, j: (l, 0, j))],
        out_specs=pl.BlockSpec((None, r, tn), lambda l, j: (l, 0, j)),
        out_shape=jax.ShapeDtypeStruct((nl, r, n), F32),
        compiler_params=_cparams(("parallel", "arbitrary")),
        name="ada",
    )(sc_all, ada_w, ada_b.astype(F32).reshape(nl, 1, n))


def _rope_tables(pos, d):
    half = d // 2
    inv = jnp.exp(jnp.arange(half, dtype=F32) * (-2.0 * math.log(ROPE_THETA) / d))
    ang = pos.astype(F32)[:, None] * inv[None, :]
    cos, sin = jnp.cos(ang), jnp.sin(ang)
    reps = LANES // d
    return (jnp.tile(jnp.concatenate([cos, cos], axis=-1), (1, reps)),
            jnp.tile(jnp.concatenate([-sin, sin], axis=-1), (1, reps)))


def _rope_lanes(x, cos, sin, d):
    if d == LANES:
        partner = pltpu.roll(x, LANES // 2, axis=1)
    else:
        lane = lax.broadcasted_iota(jnp.int32, x.shape, 1)
        half = d // 2
        partner = jnp.where((lane % d) < half, pltpu.roll(x, LANES - half, axis=1), pltpu.roll(x, half, axis=1))
    return x * cos + partner * sin


def _band_attn_kernel(*refs, patterns, d, kv_per_block, head_split, n_classes, seq, scale, has_sink):
    if has_sink:
        sink_ref, refs = refs[0], refs[1:]
    q_ref, k_ref, v_ref, cos_ref, sin_ref, o_ref, krot_ref = refs[:7]
    scratch = refs[7:]
    n_pat = len(patterns)
    qrot_scr = scratch[0]
    if n_pat > 1:
        acc_scr, m_scr, l_scr = scratch[1:]
    heads_per_unit = LANES // d
    n_units = q_ref.shape[1] // LANES
    units_per_kv = n_units // kv_per_block
    heads_per_kv = units_per_kv * heads_per_unit
    units_per_blk = units_per_kv // head_split
    n_rs = units_per_blk * heads_per_unit
    rows_q = n_rs * BAND
    npc = seq // n_classes
    log2e = math.log2(math.e)
    q_scale = scale * log2e

    chunk = min(256, seq)

    def rope_all(i, _):
        rows = pl.ds(pl.multiple_of(i * chunk, chunk), chunk)
        cos_c, sin_c = cos_ref[rows, :], sin_ref[rows, :]
        krot_ref[rows, :] = _rope_lanes(k_ref[rows, :], cos_c, sin_c, d)
        for u in range(n_units):
            lanes = slice(u * LANES, (u + 1) * LANES)
            qrot_scr[rows, lanes] = _rope_lanes(q_ref[rows, lanes], cos_c, sin_c, d) * q_scale
        return 0

    lax.fori_loop(0, seq // chunk, rope_all, 0)

    lane_q = lax.broadcasted_iota(jnp.int32, (BAND, LANES), 1)

    def band_mask(s, first):
        qlen = BAND // s
        klen = qlen if first else 2 * qlen
        nk = s * klen
        rq = lax.broadcasted_iota(jnp.int32, (BAND, nk), 0)
        rk = lax.broadcasted_iota(jnp.int32, (BAND, nk), 1)
        lq, lk = qlen.bit_length() - 1, klen.bit_length() - 1
        rel = s * ((rq & (qlen - 1)) - (rk & (klen - 1))) + ((rq >> lq) - (rk >> lk)) + (0 if first else BAND)
        return (rel >= 0) & (rel <= BAND)

    def gather(ref, runs, lanes=None):
        rd = lambda st, ln: ref[pl.ds(st, ln), :] if lanes is None else ref[pl.ds(st, ln), lanes]
        parts = [rd(st, ln) for st, ln in runs]
        return parts[0] if len(parts) == 1 else jnp.concatenate(parts, axis=0)

    def scatter(store, runs, value):
        off = 0
        for st, ln in runs:
            store(pl.ds(st, ln), value[off:off + ln])
            off += ln

    def dup(a, j):
        if d == LANES:
            return a
        lane = lax.broadcasted_iota(jnp.int32, a.shape, 1)
        rolled = pltpu.roll(a, LANES // 2, axis=1)
        return jnp.where(lane < d, a, rolled) if j == 0 else jnp.where(lane < d, rolled, a)

    def block(p_idx, kvj, ug, q_runs, k_runs, valid):
        first, last = p_idx == 0, p_idx == n_pat - 1
        nk = sum(ln for _, ln in k_runs)
        parts = []
        unit0 = kvj * units_per_kv + ug * units_per_blk
        hs0 = ug * n_rs
        for u in range(units_per_blk):
            lo = (unit0 + u) * LANES
            xr = gather(qrot_scr, q_runs, slice(lo, lo + LANES))
            if heads_per_unit == 1:
                parts.append(xr)
            else:
                parts += [jnp.where(lane_q < d, xr, 0.0), jnp.where(lane_q >= d, xr, 0.0)]
        qst = jnp.concatenate(parts, axis=0).astype(BF16)
        kk = dup(gather(krot_ref, k_runs), kvj).astype(BF16)
        vv = dup(gather(v_ref, k_runs), kvj).astype(BF16)
        vaug = jnp.concatenate([vv, jnp.ones((nk, LANES), BF16)], axis=1)
        s = lax.dot_general(qst, kk, (((1,), (1,)), ((), ())), preferred_element_type=F32)
        s = jnp.where(valid[None], s.reshape(n_rs, BAND, nk), NEG).reshape(rows_q, nk)
        m_blk = jnp.max(s, axis=-1, keepdims=True)
        if first:
            if has_sink:
                head0 = (pl.program_id(1) * kv_per_block + kvj) * heads_per_kv + hs0
                m_old = jnp.concatenate(
                    [jnp.full((BAND, LANES), sink_ref[head0 + h] * log2e, F32) for h in range(n_rs)], axis=0)
                l_old = 1.0
            else:
                m_old, l_old = jnp.full((rows_q, LANES), NEG, F32), 0.0
        else:
            m_old = jnp.concatenate([gather(m_scr.at[hs0 + h], q_runs) for h in range(n_rs)], axis=0)
        m_new = jnp.maximum(m_old, m_blk)
        p = jnp.exp2(s - jnp.concatenate([m_new] * (nk // LANES), axis=1))
        pv = jnp.dot(p.astype(BF16), vaug, preferred_element_type=F32)
        alpha = jnp.exp2(m_old - m_new)
        if first:
            acc = pv[:, :LANES]
            l_new = pv[:, LANES:] + alpha * l_old
        else:
            acc = alpha * jnp.concatenate([gather(acc_scr.at[hs0 + h], q_runs) for h in range(n_rs)], axis=0)
            acc = acc + pv[:, :LANES]
            l_new = alpha * jnp.concatenate([gather(l_scr.at[hs0 + h], q_runs) for h in range(n_rs)], axis=0)
            l_new = l_new + pv[:, LANES:]
        if last:
            out = acc / l_new
            for u in range(units_per_blk):
                lo = (unit0 + u) * LANES
                if heads_per_unit == 1:
                    piece = out[u * BAND:(u + 1) * BAND]
                else:
                    piece = jnp.where(lane_q < d, out[2 * u * BAND:(2 * u + 1) * BAND],
                                      out[(2 * u + 1) * BAND:(2 * u + 2) * BAND])

                def store_o(rows, val, lo=lo):
                    o_ref[rows, lo:lo + LANES] = val

                scatter(store_o, q_runs, piece.astype(o_ref.dtype))
        else:
            for h in range(n_rs):
                for scr, val in ((acc_scr, acc), (l_scr, l_new), (m_scr, m_new)):
                    def store_s(rows, v, scr=scr, h=hs0 + h):
                        scr[h, rows, :] = v

                    scatter(store_s, q_runs, val[h * BAND:(h + 1) * BAND])

    for p_idx, (window, r) in enumerate(patterns):
        assert window // r == BAND and n_classes % r == 0
        s = n_classes // r
        qlen = BAND // s
        nblk = npc // qlen
        valid_first, valid_rest = band_mask(s, True), band_mask(s, False)
        streams = [(kvj, ug, cr) for kvj in range(kv_per_block) for ug in range(head_split) for cr in range(r)]

        def one_block(kvj, ug, cr, m, first_blk, p_idx=p_idx, r=r, s=s, qlen=qlen,
                      valid_first=valid_first, valid_rest=valid_rest):
            base = [npc * (cr + r * a) for a in range(s)]
            if first_blk:
                q_runs = [(b0, qlen) for b0 in base]
                block(p_idx, kvj, ug, q_runs, q_runs, valid_first)
            else:
                off = pl.multiple_of(qlen * m, qlen)
                q_runs = [(b0 + off, qlen) for b0 in base]
                k_runs = [(b0 + off - qlen, 2 * qlen) for b0 in base]
                block(p_idx, kvj, ug, q_runs, k_runs, valid_rest)

        for kvj, ug, cr in streams:
            one_block(kvj, ug, cr, 0, True)
        if nblk > 1:
            unroll = 1
            if len(streams) < 3:
                unroll = next((u for u in (2, 3, 4, 5) if (nblk - 1) % u == 0 and len(streams) * u >= 3), 1)

            def body(it, _, streams=streams, unroll=unroll, one_block=one_block):
                for u in range(unroll):
                    for kvj, ug, cr in streams:
                        one_block(kvj, ug, cr, 1 + it * unroll + u, False)
                return 0

            lax.fori_loop(0, (nblk - 1) // unroll, body, 0)


def _class_major_positions(seq, n_classes):
    return jnp.arange(seq).reshape(seq // n_classes, n_classes).T.reshape(seq)


def _band_attention(y3, sinks, *, patterns, n_classes, d, n_q_heads, n_kv_heads, q_col, k_col, v_col, name,
                    head_split=1):
    nb, seq, _ = y3.shape
    kv_per_block = LANES // d
    n_kvb = n_kv_heads // kv_per_block
    qw = n_q_heads * d // n_kvb
    n_rs = n_q_heads // n_kv_heads
    cos, sin = _rope_tables(_class_major_positions(seq, n_classes), d)
    kern = functools.partial(_band_attn_kernel, patterns=patterns, d=d, kv_per_block=kv_per_block,
                             head_split=head_split, n_classes=n_classes, seq=seq, scale=d ** -0.5,
                             has_sink=sinks is not None)
    in_specs = [pl.BlockSpec((None, seq, qw), lambda b, j, *_: (b, 0, q_col // qw + j)),
                pl.BlockSpec((None, seq, LANES), lambda b, j, *_: (b, 0, k_col // LANES + j)),
                pl.BlockSpec((None, seq, LANES), lambda b, j, *_: (b, 0, v_col // LANES + j)),
                pl.BlockSpec((seq, LANES), lambda b, j, *_: (0, 0)),
                pl.BlockSpec((seq, LANES), lambda b, j, *_: (0, 0))]
    out_specs = [pl.BlockSpec((None, seq, qw), lambda b, j, *_: (b, 0, j)),
                 pl.BlockSpec((None, seq, LANES), lambda b, j, *_: (b, 0, j))]
    out_shape = [jax.ShapeDtypeStruct((nb, seq, n_q_heads * d), BF16),
                 jax.ShapeDtypeStruct((nb, seq, n_kv_heads * d), F32)]
    scratch = [pltpu.VMEM((seq, qw), F32)]
    if len(patterns) > 1:
        scratch += [pltpu.VMEM((n_rs, seq, LANES), F32) for _ in range(3)]
    assert q_col % qw == 0 and k_col % LANES == 0 and v_col % LANES == 0
    if sinks is None:
        gs = pl.GridSpec(grid=(nb, n_kvb), in_specs=in_specs, out_specs=out_specs, scratch_shapes=scratch)
        args = (y3, y3, y3, cos, sin)
    else:
        gs = pltpu.PrefetchScalarGridSpec(num_scalar_prefetch=1, grid=(nb, n_kvb), in_specs=in_specs,
                                          out_specs=out_specs, scratch_shapes=scratch)
        args = (sinks.astype(F32), y3, y3, y3, cos, sin)
    return pl.pallas_call(kern, grid_spec=gs, out_shape=out_shape,
                          compiler_params=_cparams(("parallel", "arbitrary")), name=name)(*args)


def _mlstm_kernel(q_ref, k_ref, v_ref, og_ref, igc_ref, lfc_ref, igr_ref, lfr_ref, hn_ref,
                  o_ref, c_ref, n_ref, m_ref, *, rows_per_step, chunk, heads):
    L = chunk
    ri = lax.broadcasted_iota(jnp.int32, (L, L), 0)
    ci = lax.broadcasted_iota(jnp.int32, (L, L), 1)
    tril = ri >= ci
    tril_f = tril.astype(F32)
    triu_f = (ri <= ci).astype(F32)
    hp = lax.Precision.HIGHEST

    @pl.when(pl.program_id(2) == 0)
    def _():
        c_ref[...] = jnp.zeros_like(c_ref)
        n_ref[...] = jnp.zeros_like(n_ref)
        m_ref[...] = jnp.zeros_like(m_ref)

    def head_step(h, c):
        rows = pl.ds(pl.multiple_of(c * L, L), L)
        kq, kv = slice(h * A_DK, (h + 1) * A_DK), slice(h * A_DV, (h + 1) * A_DV)
        m = m_ref[h]
        qc = q_ref[rows, kq] * (A_DK ** -0.5)
        kc = k_ref[rows, kq]
        vc = v_ref[rows, kv].astype(BF16)
        icol, fcol = igc_ref[h, rows, :], lfc_ref[h, rows, :]
        irow, frow = igr_ref[h, c], lfr_ref[h, c]
        bcol = jnp.dot(tril_f, jnp.broadcast_to(fcol, (L, LANES)), precision=hp,
                       preferred_element_type=F32)[:, :1]
        brow = jnp.dot(jnp.broadcast_to(frow, (8, L)), triu_f, precision=hp, preferred_element_type=F32)[:1, :]
        acol, arow = icol - bcol, irow - brow
        log_d = jnp.where(tril, bcol + arow, NEG)
        inter = bcol + m
        mt = jnp.maximum(inter, jnp.max(log_d, axis=-1, keepdims=True))
        dmat = jnp.exp(log_d - mt)
        wi = jnp.exp(inter - mt)
        qb = qc.astype(BF16)
        sc = lax.dot_general(qb, kc.astype(BF16), (((1,), (1,)), ((), ())), preferred_element_type=F32) * dmat
        cm = c_ref[h]
        nv = n_ref[h]
        num = (jnp.dot(sc.astype(BF16), vc, preferred_element_type=F32)
               + wi * jnp.dot(qb, cm.astype(BF16), preferred_element_type=F32))
        den = jnp.sum(sc, axis=-1, keepdims=True) + wi * jnp.sum(qc * nv, axis=-1, keepdims=True)
        hc = num / jnp.maximum(jnp.abs(den), jnp.exp(-mt))
        hn = hc * lax.rsqrt(jnp.mean(hc * hc, axis=-1, keepdims=True) + NORM_EPS) * hn_ref[h]
        o_ref[rows, kv] = (hn * jax.nn.sigmoid(og_ref[rows, kv])).astype(o_ref.dtype)
        blast = bcol[L - 1:L, :]
        m_new = jnp.maximum(blast + m, jnp.max(blast + arow, axis=-1, keepdims=True))
        w_c = jnp.exp(blast + m - m_new)
        kw = kc * jnp.exp(blast + acol - m_new)
        c_ref[h] = w_c * cm + lax.dot_general(kw.astype(BF16), vc, (((0,), (0,)), ((), ())),
                                              preferred_element_type=F32)
        n_ref[h] = w_c * nv + jnp.sum(kw, axis=0, keepdims=True)
        m_ref[h] = m_new

    def step(c, _):
        for h in range(heads):
            head_step(h, c)
        return 0

    lax.fori_loop(0, rows_per_step // L, step, 0)


def _mlstm_prompt(y3, ig, lf, head_norm, *, q_col, k_col, v_col, o_col, heads=A_HEADS, rows_per_step=512):
    nb, seq, _ = y3.shape
    L = A_CHUNK
    nc = seq // L
    rows_per_step = min(rows_per_step, seq)
    cps = rows_per_step // L
    igc = jnp.transpose(ig, (0, 2, 1))[..., None]
    lfc = jnp.transpose(lf, (0, 2, 1))[..., None]
    igr = igc.reshape(nb, A_HEADS, nc, 1, L)
    lfr = lfc.reshape(nb, A_HEADS, nc, 1, L)
    hn = head_norm.reshape(A_HEADS, 1, A_DV).astype(F32)
    col = lambda c0, w: pl.BlockSpec((None, rows_per_step, heads * w),
                                     lambda b, g, s: (b, s, c0 // (heads * w) + g))
    gcol = pl.BlockSpec((None, heads, rows_per_step, 1), lambda b, g, s: (b, g, s, 0))
    grow = pl.BlockSpec((None, heads, cps, 1, L), lambda b, g, s: (b, g, s, 0, 0))
    state = lambda *tail: pl.BlockSpec((None, heads) + tail, lambda b, g, s: (b, g) + (0,) * len(tail))
    assert seq % rows_per_step == 0 and A_HEADS % heads == 0 and all(
        c0 % (heads * w) == 0 for c0, w in ((q_col, A_DK), (k_col, A_DK), (v_col, A_DV), (o_col, A_DV)))
    return pl.pallas_call(
        functools.partial(_mlstm_kernel, rows_per_step=rows_per_step, chunk=L, heads=heads),
        grid=(nb, A_HEADS // heads, seq // rows_per_step),
        in_specs=[col(q_col, A_DK), col(k_col, A_DK), col(v_col, A_DV), col(o_col, A_DV),
                  gcol, gcol, grow, grow,
                  pl.BlockSpec((heads, 1, A_DV), lambda b, g, s: (g, 0, 0))],
        out_specs=[pl.BlockSpec((None, rows_per_step, heads * A_DV), lambda b, g, s: (b, s, g)),
                   state(A_DK, A_DV), state(1, A_DK), state(1, 1)],
        out_shape=[jax.ShapeDtypeStruct((nb, seq, A_HEADS * A_DV), BF16),
                   jax.ShapeDtypeStruct((nb, A_HEADS, A_DK, A_DV), F32),
                   jax.ShapeDtypeStruct((nb, A_HEADS, 1, A_DK), F32),
                   jax.ShapeDtypeStruct((nb, A_HEADS, 1, 1), F32)],
        compiler_params=_cparams(("parallel", "parallel", "arbitrary")),
        name="mlstm_prompt",
    )(y3, y3, y3, y3, igc, lfc, igr, lfr, hn)


def _mlstm_step_kernel(q_ref, k_ref, v_ref, og_ref, ig_ref, lf_ref, c0_ref, n0_ref, m0_ref, hn_ref,
                       o_ref, c1_ref, n1_ref, m1_ref):
    for h in range(A_HEADS):
        q = q_ref[h] * (A_DK ** -0.5)
        k = k_ref[h]
        v = v_ref[h]
        c0 = c0_ref[h]
        n0 = n0_ref[h]
        ig, lf, m0 = ig_ref[h], lf_ref[h], m0_ref[h]
        inter = lf + m0
        mt = jnp.maximum(inter, ig)
        dm = jnp.exp(ig - mt)
        wi = jnp.exp(inter - mt)
        sc = jnp.sum(q * k, axis=0, keepdims=True) * dm
        qc = jnp.sum(c0 * q, axis=0, keepdims=True)
        num = sc * v + wi * qc
        den = sc + wi * jnp.sum(q * n0, axis=0, keepdims=True)
        hc = num / jnp.maximum(jnp.abs(den), jnp.exp(-mt))
        hn = hc * lax.rsqrt(jnp.mean(hc * hc, axis=-1, keepdims=True) + NORM_EPS) * hn_ref[h]
        o_ref[h] = (hn * jax.nn.sigmoid(og_ref[h])).astype(o_ref.dtype)
        c1_ref[h] = wi * c0 + (k * dm) * v
        n1_ref[h] = wi * n0 + dm * k
        m1_ref[h] = mt


def _mlstm_sample(y, ig, lf, head_norm, c0, n0, m0):
    nb = y.shape[0]
    col = lambda a: a.reshape(nb, A_HEADS, A_DK, 1)
    row = lambda a: a.reshape(nb, A_HEADS, 1, A_DV)
    sca = lambda a: a.astype(F32).reshape(nb, A_HEADS, 1, 1)
    cspec = pl.BlockSpec((None, A_HEADS, A_DK, 1), lambda b: (b, 0, 0, 0))
    rspec = pl.BlockSpec((None, A_HEADS, 1, A_DV), lambda b: (b, 0, 0, 0))
    sspec = pl.BlockSpec((None, A_HEADS, 1, 1), lambda b: (b, 0, 0, 0))
    mspec = pl.BlockSpec((None, A_HEADS, A_DK, A_DV), lambda b: (b, 0, 0, 0))
    o, c1, n1, m1 = pl.pallas_call(
        _mlstm_step_kernel,
        grid=(nb,),
        in_specs=[cspec, cspec, rspec, rspec, sspec, sspec, mspec, cspec, sspec,
                  pl.BlockSpec((A_HEADS, 1, A_DV), lambda b: (0, 0, 0))],
        out_specs=[rspec, mspec, cspec, sspec],
        out_shape=[jax.ShapeDtypeStruct((nb, A_HEADS, 1, A_DV), F32),
                   jax.ShapeDtypeStruct((nb, A_HEADS, A_DK, A_DV), F32),
                   jax.ShapeDtypeStruct((nb, A_HEADS, A_DK, 1), F32),
                   jax.ShapeDtypeStruct((nb, A_HEADS, 1, 1), F32)],
        compiler_params=_cparams(("parallel",)),
        name="mlstm_step",
    )(col(y[:, A_Q:A_Q + A_HEADS * A_DK]), col(y[:, A_K:A_K + A_HEADS * A_DK]),
      row(y[:, A_V:A_V + A_HEADS * A_DV]), row(y[:, A_O:A_O + A_HEADS * A_DV]),
      sca(ig), sca(lf), c0.astype(F32), col(n0.astype(F32)), sca(m0),
      head_norm.astype(F32).reshape(A_HEADS, 1, A_DV))
    return (o.reshape(nb, A_HEADS * A_DV), c1, n1.reshape(nb, A_HEADS, A_DK), m1.reshape(nb, A_HEADS))


def _decode_attn_kernel(*refs, n_pat, d, group, scale, has_sink):
    q_ref, kn_ref, vn_ref, cos_ref, sin_ref = refs[:5]
    refs = refs[5:]
    if has_sink:
        sink_ref, refs = refs[0], refs[1:]
    k_refs, v_refs = refs[:n_pat], refs[n_pat:2 * n_pat]
    o_ref, knew_ref = refs[2 * n_pat:]
    n_seg = LANES // d
    cos, sin = cos_ref[...], sin_ref[...]
    kn = _rope_lanes(kn_ref[...], cos, sin, d)
    knew_ref[...] = kn
    vn = vn_ref[...]
    lane = lax.broadcasted_iota(jnp.int32, kn.shape, 1)

    def seg_sum(p):
        if n_seg == 1:
            return jnp.sum(p, axis=-1, keepdims=True)
        lane_p = lax.broadcasted_iota(jnp.int32, p.shape, p.ndim - 1)
        lo = jnp.sum(jnp.where(lane_p < d, p, 0.0), axis=-1, keepdims=True)
        hi = jnp.sum(jnp.where(lane_p >= d, p, 0.0), axis=-1, keepdims=True)
        return jnp.where(lane_p < d, lo, hi)

    for g in range(group):
        qg = _rope_lanes(q_ref[g], cos, sin, d)
        s_new = seg_sum(kn * qg) * scale
        s = [seg_sum(k_refs[p][...] * qg[None]) * scale for p in range(n_pat)]
        m = s_new
        for p in range(n_pat):
            m = jnp.maximum(m, jnp.max(s[p], axis=0))
        if has_sink:
            m = jnp.maximum(m, sink_ref[g])
        e_new = jnp.exp(s_new - m) * float(n_pat)
        l = e_new
        acc = e_new * vn
        for p in range(n_pat):
            e = jnp.exp(s[p] - m[None])
            l = l + jnp.sum(e, axis=0)
            acc = acc + jnp.sum(e * v_refs[p][...], axis=0)
        if has_sink:
            l = l + jnp.exp(sink_ref[g] - m)
        o_ref[g] = (acc / l).astype(o_ref.dtype)


def _decode_attention(q, k_new, v_new, k_cache, v_cache, sinks, pos, *, patterns, d, name):
    nb, n_heads, _ = q.shape
    n_kv, cache_len = k_cache.shape[2], k_cache.shape[1]
    group = n_heads // n_kv
    n_seg = LANES // d
    r_rows = n_kv // n_seg
    to_tiles = lambda a, inner: a.reshape(nb, r_rows, n_seg, inner, d).transpose(0, 3, 1, 2, 4).reshape(
        nb, inner, r_rows, LANES)
    qt = to_tiles(q, group)
    knt = k_new.reshape(nb, r_rows, LANES)
    vnt = v_new.reshape(nb, r_rows, LANES)
    cos, sin = _rope_tables(pos, d)
    kc = k_cache.reshape(nb, cache_len, r_rows, LANES)
    vc = v_cache.reshape(nb, cache_len, r_rows, LANES)
    tile = pl.BlockSpec((None, r_rows, LANES), lambda b: (b, 0, 0))
    one = pl.BlockSpec((1, LANES), lambda b: (0, 0))
    in_specs = [pl.BlockSpec((None, group, r_rows, LANES), lambda b: (b, 0, 0, 0)), tile, tile, one, one]
    args = [qt, knt, vnt, cos, sin]
    if sinks is not None:
        sk = jnp.broadcast_to(sinks.astype(F32).reshape(r_rows, n_seg, group, 1), (r_rows, n_seg, group, d))
        args.append(sk.transpose(2, 0, 1, 3).reshape(group, r_rows, LANES))
        in_specs.append(pl.BlockSpec((group, r_rows, LANES), lambda b: (0, 0, 0)))
    views = []
    for window, r in patterns:
        assert window // r == BAND and cache_len % (BAND * r) == 0
        blk = cache_len // r // BAND - 1
        views.append(pl.BlockSpec((None, BAND, None, r_rows, LANES), lambda b, blk=blk: (b, blk, 0, 0, 0)))
    in_specs += views + views
    args += [kc.reshape(nb, cache_len // r, r, r_rows, LANES) for _, r in patterns]
    args += [vc.reshape(nb, cache_len // r, r, r_rows, LANES) for _, r in patterns]
    o, kn = pl.pallas_call(
        functools.partial(_decode_attn_kernel, n_pat=len(patterns), d=d, group=group, scale=d ** -0.5,
                          has_sink=sinks is not None),
        grid=(nb,),
        in_specs=in_specs,
        out_specs=[pl.BlockSpec((None, group, r_rows, LANES), lambda b: (b, 0, 0, 0)), tile],
        out_shape=[jax.ShapeDtypeStruct((nb, group, r_rows, LANES), F32),
                   jax.ShapeDtypeStruct((nb, r_rows, LANES), F32)],
        compiler_params=_cparams(("parallel",)),
        name=name,
    )(*args)
    o = o.reshape(nb, group, r_rows, n_seg, d).transpose(0, 2, 3, 1, 4).reshape(nb, n_heads * d)
    return o, kn.reshape(nb, n_kv, d)


def _shift_kernel(c_ref, new_ref, o_ref, *, length, chunk):
    n_full = (length - 1) // chunk

    def body(i, _):
        o_ref[pl.ds(i * chunk, chunk)] = c_ref[pl.ds(i * chunk + 1, chunk)]
        return 0

    lax.fori_loop(0, n_full, body, 0)
    done = n_full * chunk
    if done < length - 1:
        o_ref[done:length - 1] = c_ref[done + 1:length]
    o_ref[length - 1] = new_ref[...]


def _shift_cache(cache, new, name):
    nb, length, n_kv, d = cache.shape
    return pl.pallas_call(
        functools.partial(_shift_kernel, length=length, chunk=min(64, length - 1)),
        grid=(nb,),
        in_specs=[pl.BlockSpec((None, length, n_kv, d), lambda b: (b, 0, 0, 0)),
                  pl.BlockSpec((None, n_kv, d), lambda b: (b, 0, 0))],
        out_specs=pl.BlockSpec((None, length, n_kv, d), lambda b: (b, 0, 0, 0)),
        out_shape=jax.ShapeDtypeStruct(cache.shape, cache.dtype),
        compiler_params=_cparams(("parallel",)),
        name=name,
    )(cache, new.astype(cache.dtype))


N_IN_EVEN = 2 * A_HEADS * A_DK + 2 * A_HEADS * A_DV + 2 * A_HEADS + (B_HEADS + 2 * B_KV_HEADS) * B_HEAD_DIM
A_Q, A_K, A_V, A_O = 0, 1024, 2048, 4096
GATE_COL = 6144
B_COL = GATE_COL + 2 * A_HEADS
B_Q, B_K, B_V = 0, 2048, 2304
C_Q, C_K, C_V = 0, 4096, 5120
SWA_PATTERN = ((B_WINDOW, 1),)


def _even_projection(hp, hs, w_in, j):
    w_t = jnp.swapaxes(w_in, 1, 2)
    ya = _matmul2([(hp, hs, w_t, j, 0)], GATE_COL, F32, 1024, 512, "even_in_a", w_rows_are_outputs=True)
    yb = _matmul2([(hp, hs, w_t, j, 0)], N_IN_EVEN - B_COL, F32, 1024, 512, "even_in_b",
                  w_rows_are_outputs=True, w_row0=B_COL)
    w_gate = jnp.pad(w_in[j, :, GATE_COL:B_COL], ((0, 0), (0, LANES - 2 * A_HEADS)))[None]
    yg = _matmul2([(hp, hs, w_gate, 0, 0)], LANES, F32, 1024, LANES, "even_in_gates")
    return [(ya[g], yg[g][:, :2 * A_HEADS], yb[g]) for g in range(2)]


def _gates(yg, gate_b):
    gb = gate_b.astype(F32)
    ig = yg[..., :A_HEADS] + gb[:A_HEADS]
    lf = jax.nn.log_sigmoid(yg[..., A_HEADS:] + gb[A_HEADS:])
    return ig, lf


def _even_mixer_prompt(proj, nb, seq, gate_b, head_norm, sinks):
    y, yg, yb = proj
    y3, yb3 = y.reshape(nb, seq, -1), yb.reshape(nb, seq, -1)
    ig, lf = _gates(yg.reshape(nb, seq, -1), gate_b)
    ha, c1, n1, m1 = _mlstm_prompt(y3, ig, lf, head_norm, q_col=A_Q, k_col=A_K, v_col=A_V, o_col=A_O)
    ob, kb = _band_attention(yb3, sinks, patterns=SWA_PATTERN, n_classes=1, d=B_HEAD_DIM, n_q_heads=B_HEADS,
                             n_kv_heads=B_KV_HEADS, q_col=B_Q, k_col=B_K, v_col=B_V, name="swa_prompt",
                             head_split=2)
    kbuf = kb[:, -B_WINDOW:].reshape(nb, -1, B_KV_HEADS, B_HEAD_DIM)
    vbuf = yb3[:, -B_WINDOW:, B_V:].reshape(nb, -1, B_KV_HEADS, B_HEAD_DIM)
    state = (c1, n1.reshape(nb, A_HEADS, A_DK), m1.reshape(nb, A_HEADS), kbuf, vbuf)
    return ha.reshape(nb * seq, -1), ob.reshape(nb * seq, -1), state


def _even_mixer_sample(proj, gate_b, head_norm, sinks, state):
    y, yg, yb = proj
    nb = y.shape[0]
    ig, lf = _gates(yg, gate_b)
    c0, n0, m0, k_old, v_old = state
    ha, c1, n1, m1 = _mlstm_sample(y, ig, lf, head_norm, c0, n0, m0)
    q = yb[:, B_Q:B_K].reshape(nb, B_HEADS, B_HEAD_DIM)
    k_new = yb[:, B_K:B_V].reshape(nb, B_KV_HEADS, B_HEAD_DIM)
    v_new = yb[:, B_V:].reshape(nb, B_KV_HEADS, B_HEAD_DIM)
    ob, k_rot = _decode_attention(q, k_new, v_new, k_old, v_old, sinks, PAST_LEN + jnp.arange(1),
                                  patterns=SWA_PATTERN, d=B_HEAD_DIM, name="swa_step")
    state = (c1, n1, m1, _shift_cache(k_old, k_rot, "swa_k_shift"), _shift_cache(v_old, v_new, "swa_v_shift"))
    return ha.astype(BF16), ob.astype(BF16), state


def _even_mixer(hp, hs, nb, seq, j, w_in, gate_b, head_norm, sinks, w_out, state):
    proj_p, proj_s = _even_projection(hp, hs, w_in, j)
    ha_p, ob_p, st_p = _even_mixer_prompt(proj_p, nb, seq, gate_b, head_norm, sinks)
    ha_s, ob_s, st_s = _even_mixer_sample(proj_s, gate_b, head_norm, sinks, state)
    outs = _matmul2([(ha_p, ha_s, w_out, j, 0), (ob_p, ob_s, w_out, j, 1)], D_MODEL, F32, 1024, 512, "even_out")
    return outs, (st_p, st_s)


def _odd_mixer(hp, hs, nb, seq, j, w_in, w_out, state):
    out_p, st_p = _odd_mixer_prompt(hp, nb, seq, j, w_in, w_out)
    out_s, st_s = _odd_mixer_sample(hs, j, w_in, w_out, state)
    return (out_p, out_s), (st_p, st_s)


C_CLASSES = max(r for _, r in C_PATTERNS)


def _odd_mixer_prompt(h2, nb, seq, j, w_in, w_out):
    y3 = _matmul_regroup(h2, w_in, j, w_in.shape[2], nb, C_CLASSES, True, 1024, 512, "odd_in").reshape(nb, seq, -1)
    o, kr = _band_attention(y3, None, patterns=C_PATTERNS, n_classes=C_CLASSES, d=C_HEAD_DIM, n_q_heads=C_HEADS,
                            n_kv_heads=C_KV_HEADS, q_col=C_Q, k_col=C_K, v_col=C_V, name="dil_prompt")
    out = _matmul_regroup(o.reshape(nb, C_CLASSES, seq // C_CLASSES, -1), w_out, j, D_MODEL, nb, C_CLASSES, False,
                          1024, 512, "odd_out")
    natural = lambda a: a.reshape(nb, C_CLASSES, seq // C_CLASSES, C_KV_HEADS, C_HEAD_DIM).transpose(
        0, 2, 1, 3, 4).reshape(nb, seq, C_KV_HEADS, C_HEAD_DIM)
    return out, (natural(kr), natural(y3[:, :, C_V:]))


def _odd_mixer_sample(h2, j, w_in, w_out, state):
    nb = h2.shape[0]
    y = _matmul([(h2, w_in, j, 0)], w_in.shape[2], F32, nb, 512, "odd_in_s")
    q = y[:, C_Q:C_K].reshape(nb, C_HEADS, C_HEAD_DIM)
    k_new = y[:, C_K:C_V].reshape(nb, C_KV_HEADS, C_HEAD_DIM)
    v_new = y[:, C_V:].reshape(nb, C_KV_HEADS, C_HEAD_DIM)
    k_old, v_old = state
    o, k_rot = _decode_attention(q, k_new, v_new, k_old, v_old, None, PAST_LEN + jnp.arange(1),
                                 patterns=C_PATTERNS, d=C_HEAD_DIM, name="dil_step")
    out = _matmul([(o.astype(BF16), w_out, j, 0)], D_MODEL, F32, nb, 512, "odd_out_s")
    return out, (_shift_cache(k_old, k_rot, "dil_k_shift"), _shift_cache(v_old, v_new, "dil_v_shift"))


def _trunk(xs, rows, mixers, norms, ffn):
    g_mix_pre, g_mix_post, g_ffn_pre, g_ffn_post = norms
    wg, wu, wd = ffn
    xs = list(xs)
    hs = [_prenorm(r, x, g_mix_pre[0], 0, MOD_SC1, MOD_SH1) for r, x in zip(rows, xs)]
    states = []
    for li in range(DEPTH):
        ys, state = mixers[li](*hs)
        states.append(state)
        for g in range(2):
            xs[g], hs[g] = _postnorm(rows[g], xs[g], ys[g], g_mix_post[li], li, MOD_GT1,
                                     (g_ffn_pre[li], li, MOD_SC2, MOD_SH2))
        acts = _matmul2([(hs[0], hs[1], (wg, wu), li, 0)], wg.shape[2], BF16, 1024, 256, "swiglu", swiglu=True)
        fs = [_matmul([(a, wd, li, 0)], D_MODEL, F32, 512, 512, "ffn_down") for a in acts]
        nxt = (g_mix_pre[li + 1], li + 1, MOD_SC1, MOD_SH1) if li + 1 < DEPTH else None
        for g in range(2):
            xs[g], hs[g] = _postnorm(rows[g], xs[g], fs[g], g_ffn_post[li], li, MOD_GT2, nxt)
    return xs, states


def kernel(x_prompt, x_sample, state_l0_mlstm_c, state_l0_mlstm_n, state_l0_mlstm_m, cache_l0_swa_k, cache_l0_swa_v, cache_l1_dil_k, cache_l1_dil_v, state_l2_mlstm_c, state_l2_mlstm_n, state_l2_mlstm_m, cache_l2_swa_k, cache_l2_swa_v, cache_l3_dil_k, cache_l3_dil_v, c_prompt, c_sample, ada_w, ada_b, norm_mix_pre, norm_mix_post, norm_ffn_pre, norm_ffn_post, even_w_in, even_gate_b, even_head_norm, even_sinks, even_w_out, odd_w_in, odd_w_out, ffn_w_gate, ffn_w_up, ffn_w_down):
    past = {0: (state_l0_mlstm_c, state_l0_mlstm_n, state_l0_mlstm_m, cache_l0_swa_k, cache_l0_swa_v),
            1: (cache_l1_dil_k, cache_l1_dil_v),
            2: (state_l2_mlstm_c, state_l2_mlstm_n, state_l2_mlstm_m, cache_l2_swa_k, cache_l2_swa_v),
            3: (cache_l3_dil_k, cache_l3_dil_v)}
    nbp, seq, dm = x_prompt.shape
    nbs = x_sample.shape[0]
    c_all = jnp.concatenate([c_sample, c_prompt], axis=0)
    pad = (-c_all.shape[0]) % 16
    sc_all = jnp.pad(jax.nn.silu(c_all), ((0, pad), (0, 0))).astype(BF16)
    mod_all = _ada(sc_all, ada_w, ada_b)
    mixers = []
    for li in range(DEPTH):
        j = li // 2
        if li % 2 == 0:
            mw = (j, even_w_in, even_gate_b[j], even_head_norm[j], even_sinks[j], even_w_out, past[li])
            mixers.append(lambda hp, hs, mw=mw: _even_mixer(hp, hs, nbp, seq, *mw))
        else:
            mw = (j, odd_w_in, odd_w_out, past[li])
            mixers.append(lambda hp, hs, mw=mw: _odd_mixer(hp, hs, nbp, seq, *mw))
    norms = (norm_mix_pre, norm_mix_post, norm_ffn_pre, norm_ffn_post)
    ffn = (ffn_w_gate, ffn_w_up, ffn_w_down.astype(BF16))
    rows = (_Rows(mod_all, nbp * seq, seq, nbs, 256), _Rows(mod_all, nbs, 1, 0, nbs))
    (y_p, y_s), states = _trunk((x_prompt.reshape(nbp * seq, dm), x_sample.reshape(nbs, dm)), rows, mixers,
                                norms, ffn)
    y_p, y_s = y_p.reshape(x_prompt.shape), y_s.reshape(x_sample.shape)
    new_p, new_s = [s[0] for s in states], [s[1] for s in states]
    (p0c, p0n, p0m, p0k, p0v), (p1k, p1v), (p2c, p2n, p2m, p2k, p2v), (p3k, p3v) = new_p
    (s0c, s0n, s0m, s0k, s0v), (s1k, s1v), (s2c, s2n, s2m, s2k, s2v), (s3k, s3v) = new_s
    return (y_p, y_s,
            p0c, p0n, p0m, p0k, p0v, p1k, p1v, p2c, p2n, p2m, p2k, p2v, p3k, p3v,
            s0c, s0n, s0m, s0k, s0v, s1k, s1v, s2c, s2n, s2m, s2k, s2v, s3k, s3v)
```

```python
import functools
import math

import jax
import jax.numpy as jnp
import numpy as np
from jax import lax
from jax.experimental import pallas as pl
from jax.experimental.pallas import tpu as pltpu

F32 = jnp.float32
BF16 = jnp.bfloat16

D_MODEL = 4096
DEPTH = 4
PAST_LEN = 8192
A_HEADS, A_DK, A_DV, A_CHUNK = 4, 256, 512, 64
B_HEADS, B_KV_HEADS, B_HEAD_DIM, B_WINDOW = 32, 4, 64, 128
C_HEADS, C_KV_HEADS, C_HEAD_DIM = 32, 8, 128
C_PATTERNS = ((128, 1), (512, 4), (2048, 16))
BAND = 128
ROPE_THETA = 10000.0
NORM_EPS = 1e-6
LANES = 128
NEG = -1e30
VMEM_LIMIT = 56 * 1024 * 1024


def _cparams(sem):
    return pltpu.CompilerParams(dimension_semantics=sem, vmem_limit_bytes=VMEM_LIMIT)


def _mm_kernel(*refs, n_in):
    o_ref = refs[-1]
    acc = None
    for i in range(n_in):
        x = refs[2 * i][...].astype(BF16)
        w = refs[2 * i + 1][...].astype(BF16)
        d = jnp.dot(x, w, preferred_element_type=F32)
        acc = d if acc is None else acc + d
    o_ref[...] = acc.astype(o_ref.dtype)


def _w_spec(w, layer, kblk, k, tn):
    if w.ndim == 3:
        return pl.BlockSpec((None, k, tn), lambda i, j: (layer, kblk, j))
    return pl.BlockSpec((k, tn), lambda i, j: (kblk, j))


def _matmul(pairs, n, out_dtype, tm, tn, name):
    m = pairs[0][0].shape[0]
    tm = min(tm, m)
    tn = min(tn, n)
    assert m % tm == 0 and n % tn == 0, (m, n, tm, tn)
    in_specs, args = [], []
    for x, w, layer, kblk in pairs:
        k = x.shape[1]
        in_specs += [pl.BlockSpec((tm, k), lambda i, j: (i, 0)), _w_spec(w, layer, kblk, k, tn)]
        args += [x, w]
    return pl.pallas_call(
        functools.partial(_mm_kernel, n_in=len(pairs)),
        grid=(m // tm, n // tn),
        in_specs=in_specs,
        out_specs=pl.BlockSpec((tm, tn), lambda i, j: (i, j)),
        out_shape=jax.ShapeDtypeStruct((m, n), out_dtype),
        compiler_params=_cparams(("parallel", "arbitrary")),
        name=name,
    )(*args)


def _mm2_kernel(*refs, n_in, swiglu, w_rows_are_outputs):
    n_w = 2 if swiglu else 1
    per = 2 + n_w
    ins = refs[:n_in * per]
    op_ref, os_ref = refs[n_in * per:]
    w_refs = [ins[(t // n_w) * per + 2 + t % n_w] for t in range(n_in * n_w)]
    ws = [(r[0] if len(r.shape) == 3 else r[...]).astype(BF16) for r in w_refs]

    def compute(xsel, o_ref):
        accs = [None] * n_w
        for p in range(n_in):
            x = ins[p * per + xsel][...]
            for t in range(n_w):
                contract = (((1,), (1 if w_rows_are_outputs else 0,)), ((), ()))
                dt = lax.dot_general(x, ws[p * n_w + t], contract, preferred_element_type=F32)
                accs[t] = dt if accs[t] is None else accs[t] + dt
        r = accs[0] * jax.nn.sigmoid(accs[0]) * accs[1] if swiglu else accs[0]
        o_ref[...] = r.astype(o_ref.dtype)

    compute(0, op_ref)

    @pl.when(pl.program_id(0) == 0)
    def _():
        compute(1, os_ref)


def _matmul2(pairs, n, out_dtype, tm, tn, name, swiglu=False, w_rows_are_outputs=False, w_row0=0):
    m, ms = pairs[0][0].shape[0], pairs[0][1].shape[0]
    assert m % tm == 0 and n % tn == 0, (m, n, tm, tn)
    n_j = n // tn
    in_specs, args = [], []
    for xp, xs, ws, layer, kblk in pairs:
        k = xp.shape[1]
        in_specs += [pl.BlockSpec((tm, k), lambda i, j: (i, 0)), pl.BlockSpec((ms, k), lambda i, j: (0, 0))]
        args += [xp, xs]
        for w in (ws if swiglu else (ws,)):
            if w_rows_are_outputs:
                in_specs.append(pl.BlockSpec(
                    (pl.Element(1), pl.Element(tn), pl.Element(k)),
                    lambda i, j, layer=layer, kblk=kblk: (layer, pl.multiple_of(w_row0 + j * tn, 8), kblk * k)))
            else:
                in_specs.append(pl.BlockSpec((None, k, tn), lambda i, j, layer=layer, kblk=kblk: (layer, kblk, j)))
            args.append(w)
    return pl.pallas_call(
        functools.partial(_mm2_kernel, n_in=len(pairs), swiglu=swiglu, w_rows_are_outputs=w_rows_are_outputs),
        grid=(m // tm, n_j),
        in_specs=in_specs,
        out_specs=[pl.BlockSpec((tm, tn), lambda i, j: (i, j)),
                   pl.BlockSpec((ms, tn), lambda i, j: (0, jnp.where(i == 0, j, n_j - 1)))],
        out_shape=[jax.ShapeDtypeStruct((m, n), out_dtype), jax.ShapeDtypeStruct((ms, n), out_dtype)],
        compiler_params=_cparams(("arbitrary", "arbitrary")),
        name=name,
    )(*args)


def _digit_swap(n):
    p = lax.broadcasted_iota(jnp.int32, (n * n, n * n), 0)
    q = lax.broadcasted_iota(jnp.int32, (n * n, n * n), 1)
    shift = n.bit_length() - 1
    return (q == ((p & (n - 1)) << shift) + (p >> shift)).astype(BF16)


def _mm_regroup_kernel(x_ref, w_ref, o_ref, xp_scr, *, n_cls, to_classes):
    blk = n_cls * n_cls
    tm = xp_scr.shape[0]

    @pl.when(pl.program_id(1) == 0)
    def _():
        perm = _digit_swap(n_cls)
        for k in range(tm // blk):
            if to_classes:
                xs = x_ref[k * blk:(k + 1) * blk, :]
            else:
                xs = jnp.concatenate([x_ref[c, k * n_cls:(k + 1) * n_cls, :] for c in range(n_cls)], axis=0)
            xp_scr[k * blk:(k + 1) * blk, :] = jnp.dot(perm, xs, preferred_element_type=F32).astype(BF16)

    r = jnp.dot(xp_scr[...], w_ref[...].astype(BF16), preferred_element_type=F32)
    if to_classes:
        for k in range(tm // blk):
            for c in range(n_cls):
                o_ref[c, k * n_cls:(k + 1) * n_cls, :] = r[k * blk + c * n_cls:k * blk + (c + 1) * n_cls]
    else:
        o_ref[...] = r


def _matmul_regroup(x, w, layer, n, nb, n_cls, to_classes, tm, tn, name):
    k = x.shape[-1]
    seq = x.shape[0] // nb if to_classes else n_cls * x.shape[2]
    tiles = seq // tm
    assert seq % tm == 0 and tm % (n_cls * n_cls) == 0 and n % tn == 0
    cls_block = lambda cols: (None, n_cls, tm // n_cls, cols)
    if to_classes:
        x_spec = pl.BlockSpec((tm, k), lambda i, j: (i, 0))
        o_spec = pl.BlockSpec(cls_block(tn), lambda i, j: (i // tiles, 0, i % tiles, j))
        o_shape = (nb, n_cls, seq // n_cls, n)
    else:
        x_spec = pl.BlockSpec(cls_block(k), lambda i, j: (i // tiles, 0, i % tiles, 0))
        o_spec = pl.BlockSpec((tm, tn), lambda i, j: (i, j))
        o_shape = (nb * seq, n)
    return pl.pallas_call(
        functools.partial(_mm_regroup_kernel, n_cls=n_cls, to_classes=to_classes),
        grid=(nb * tiles, n // tn),
        in_specs=[x_spec, _w_spec(w, layer, 0, k, tn)],
        out_specs=o_spec,
        out_shape=jax.ShapeDtypeStruct(o_shape, F32),
        scratch_shapes=[pltpu.VMEM((tm, k), BF16)],
        compiler_params=_cparams(("parallel", "arbitrary")),
        name=name,
    )(x, w)


MOD_SH1, MOD_SC1, MOD_GT1, MOD_SH2, MOD_SC2, MOD_GT2 = range(6)


def _rms_rows(x, g):
    return x * lax.rsqrt(jnp.mean(x * x, axis=-1, keepdims=True) + NORM_EPS) * g


def _prenorm_kernel(x_ref, g_ref, sc_ref, sh_ref, h_ref):
    h_ref[...] = (_rms_rows(x_ref[...], g_ref[...]) * (1.0 + sc_ref[...]) + sh_ref[...]).astype(h_ref.dtype)


def _postnorm_kernel(*refs, has_next):
    x_ref, y_ref, gt_ref, gpost_ref = refs[:4]
    xn = x_ref[...] + gt_ref[...] * _rms_rows(y_ref[...], gpost_ref[...])
    if has_next:
        gpre_ref, sc_ref, sh_ref, xo_ref, h_ref = refs[4:]
        h_ref[...] = (_rms_rows(xn, gpre_ref[...]) * (1.0 + sc_ref[...]) + sh_ref[...]).astype(h_ref.dtype)
    else:
        xo_ref = refs[4]
    xo_ref[...] = xn


class _Rows:
    def __init__(self, mod_all, n_rows, rows_per_batch, mod_row0, tm):
        self.dm = mod_all.shape[-1] // 6
        self.tm = min(tm, n_rows)
        self.n_rows = n_rows
        self.per_token = rows_per_batch == 1
        if self.per_token:
            self.mod = mod_all
            assert mod_row0 % self.tm == 0 and n_rows == self.tm
        else:
            self.mod = mod_all.reshape(mod_all.shape[0], mod_all.shape[1], 1, mod_all.shape[2])
            assert rows_per_batch % self.tm == 0
        self.tiles_per_batch = max(rows_per_batch // self.tm, 1)
        self.mod_row0 = mod_row0

    def rows(self):
        return pl.BlockSpec((self.tm, self.dm), lambda i: (i, 0))

    def vec(self, layer, chunk):
        if self.per_token:
            return pl.BlockSpec((None, self.tm, self.dm), lambda i: (layer, self.mod_row0 // self.tm, chunk))
        return pl.BlockSpec((None, None, 1, self.dm),
                            lambda i: (layer, self.mod_row0 + i // self.tiles_per_batch, 0, chunk))


def _gain_spec(dm):
    return pl.BlockSpec((None, 1, dm), lambda i: (0, 0, 0))


def _prenorm(rows, x2, gain, layer, sc_chunk, sh_chunk):
    dm = rows.dm
    return pl.pallas_call(
        _prenorm_kernel,
        grid=(rows.n_rows // rows.tm,),
        in_specs=[rows.rows(), _gain_spec(dm), rows.vec(layer, sc_chunk), rows.vec(layer, sh_chunk)],
        out_specs=rows.rows(),
        out_shape=jax.ShapeDtypeStruct((rows.n_rows, dm), BF16),
        compiler_params=_cparams(("parallel",)),
        name="prenorm",
    )(x2, gain.astype(F32).reshape(1, 1, dm), rows.mod, rows.mod)


def _postnorm(rows, x2, y2, gain_post, layer, gt_chunk, nxt):
    dm = rows.dm
    in_specs = [rows.rows(), rows.rows(), rows.vec(layer, gt_chunk), _gain_spec(dm)]
    args = [x2, y2, rows.mod, gain_post.astype(F32).reshape(1, 1, dm)]
    out_specs = [rows.rows()]
    out_shape = [jax.ShapeDtypeStruct((rows.n_rows, dm), F32)]
    if nxt is not None:
        gain_pre, nl, sc_chunk, sh_chunk = nxt
        in_specs += [_gain_spec(dm), rows.vec(nl, sc_chunk), rows.vec(nl, sh_chunk)]
        args += [gain_pre.astype(F32).reshape(1, 1, dm), rows.mod, rows.mod]
        out_specs.append(rows.rows())
        out_shape.append(jax.ShapeDtypeStruct((rows.n_rows, dm), BF16))
    res = pl.pallas_call(
        functools.partial(_postnorm_kernel, has_next=nxt is not None),
        grid=(rows.n_rows // rows.tm,),
        in_specs=in_specs, out_specs=out_specs, out_shape=out_shape,
        compiler_params=_cparams(("parallel",)),
        name="postnorm",
    )(*args)
    return (res[0], res[1]) if nxt is not None else (res[0], None)


def _ada_kernel(c_ref, w_ref, b_ref, o_ref):
    o_ref[...] = jnp.dot(c_ref[...], w_ref[...].astype(BF16), preferred_element_type=F32) + b_ref[...]


def _ada(sc_all, ada_w, ada_b, tn=512):
    nl, k, n = ada_w.shape
    r = sc_all.shape[0]
    return pl.pallas_call(
        _ada_kernel,
        grid=(nl, n // tn),
        in_specs=[pl.BlockSpec((r, k), lambda l, j: (0, 0)),
                  pl.BlockSpec((None, k, tn), lambda l, j: (l, 0, j)),
                  pl.BlockSpec((None, 1, tn), lambda l, j: (l, 0, j))],
        out_specs=pl.BlockSpec((None, r, tn), lambda l, j: (l, 0, j)),
        out_shape=jax.ShapeDtypeStruct((nl, r, n), F32),
        compiler_params=_cparams(("parallel", "arbitrary")),
        name="ada",
    )(sc_all, ada_w, ada_b.astype(F32).reshape(nl, 1, n))


def _rope_tables(pos, d):
    half = d // 2
    inv = jnp.exp(jnp.arange(half, dtype=F32) * (-2.0 * math.log(ROPE_THETA) / d))
    ang = pos.astype(F32)[:, None] * inv[None, :]
    cos, sin = jnp.cos(ang), jnp.sin(ang)
    reps = LANES // d
    return (jnp.tile(jnp.concatenate([cos, cos], axis=-1), (1, reps)),
            jnp.tile(jnp.concatenate([-sin, sin], axis=-1), (1, reps)))


def _rope_lanes(x, cos, sin, d):
    if d == LANES:
        partner = pltpu.roll(x, LANES // 2, axis=1)
    else:
        lane = lax.broadcasted_iota(jnp.int32, x.shape, 1)
        half = d // 2
        partner = jnp.where((lane % d) < half, pltpu.roll(x, LANES - half, axis=1), pltpu.roll(x, half, axis=1))
    return x * cos + partner * sin


def _band_attn_kernel(*refs, patterns, d, kv_per_block, head_split, n_classes, seq, scale, has_sink):
    if has_sink:
        sink_ref, refs = refs[0], refs[1:]
    q_ref, k_ref, v_ref, cos_ref, sin_ref, o_ref, krot_ref = refs[:7]
    scratch = refs[7:]
    n_pat = len(patterns)
    qrot_scr = scratch[0]
    if n_pat > 1:
        acc_scr, m_scr, l_scr = scratch[1:]
    heads_per_unit = LANES // d
    n_units = q_ref.shape[1] // LANES
    units_per_kv = n_units // kv_per_block
    heads_per_kv = units_per_kv * heads_per_unit
    units_per_blk = units_per_kv // head_split
    n_rs = units_per_blk * heads_per_unit
    rows_q = n_rs * BAND
    npc = seq // n_classes
    log2e = math.log2(math.e)
    q_scale = scale * log2e

    chunk = min(256, seq)

    def rope_all(i, _):
        rows = pl.ds(pl.multiple_of(i * chunk, chunk), chunk)
        cos_c, sin_c = cos_ref[rows, :], sin_ref[rows, :]
        krot_ref[rows, :] = _rope_lanes(k_ref[rows, :], cos_c, sin_c, d)
        for u in range(n_units):
            lanes = slice(u * LANES, (u + 1) * LANES)
            qrot_scr[rows, lanes] = _rope_lanes(q_ref[rows, lanes], cos_c, sin_c, d) * q_scale
        return 0

    lax.fori_loop(0, seq // chunk, rope_all, 0)

    lane_q = lax.broadcasted_iota(jnp.int32, (BAND, LANES), 1)

    def band_mask(s, first):
        qlen = BAND // s
        klen = qlen if first else 2 * qlen
        nk = s * klen
        rq = lax.broadcasted_iota(jnp.int32, (BAND, nk), 0)
        rk = lax.broadcasted_iota(jnp.int32, (BAND, nk), 1)
        lq, lk = qlen.bit_length() - 1, klen.bit_length() - 1
        rel = s * ((rq & (qlen - 1)) - (rk & (klen - 1))) + ((rq >> lq) - (rk >> lk)) + (0 if first else BAND)
        return (rel >= 0) & (rel <= BAND)

    def gather(ref, runs, lanes=None):
        rd = lambda st, ln: ref[pl.ds(st, ln), :] if lanes is None else ref[pl.ds(st, ln), lanes]
        parts = [rd(st, ln) for st, ln in runs]
        return parts[0] if len(parts) == 1 else jnp.concatenate(parts, axis=0)

    def scatter(store, runs, value):
        off = 0
        for st, ln in runs:
            store(pl.ds(st, ln), value[off:off + ln])
            off += ln

    def dup(a, j):
        if d == LANES:
            return a
        lane = lax.broadcasted_iota(jnp.int32, a.shape, 1)
        rolled = pltpu.roll(a, LANES // 2, axis=1)
        return jnp.where(lane < d, a, rolled) if j == 0 else jnp.where(lane < d, rolled, a)

    def block(p_idx, kvj, ug, q_runs, k_runs, valid):
        first, last = p_idx == 0, p_idx == n_pat - 1
        nk = sum(ln for _, ln in k_runs)
        parts = []
        unit0 = kvj * units_per_kv + ug * units_per_blk
        hs0 = ug * n_rs
        for u in range(units_per_blk):
            lo = (unit0 + u) * LANES
            xr = gather(qrot_scr, q_runs, slice(lo, lo + LANES))
            if heads_per_unit == 1:
                parts.append(xr)
            else:
                parts += [jnp.where(lane_q < d, xr, 0.0), jnp.where(lane_q >= d, xr, 0.0)]
        qst = jnp.concatenate(parts, axis=0).astype(BF16)
        kk = dup(gather(krot_ref, k_runs), kvj).astype(BF16)
        vv = dup(gather(v_ref, k_runs), kvj).astype(BF16)
        vaug = jnp.concatenate([vv, jnp.ones((nk, LANES), BF16)], axis=1)
        s = lax.dot_general(qst, kk, (((1,), (1,)), ((), ())), preferred_element_type=F32)
        s = jnp.where(valid[None], s.reshape(n_rs, BAND, nk), NEG).reshape(rows_q, nk)
        m_blk = jnp.max(s, axis=-1, keepdims=True)
        if first:
            if has_sink:
                head0 = (pl.program_id(1) * kv_per_block + kvj) * heads_per_kv + hs0
                m_old = jnp.concatenate(
                    [jnp.full((BAND, LANES), sink_ref[head0 + h] * log2e, F32) for h in range(n_rs)], axis=0)
                l_old = 1.0
            else:
                m_old, l_old = jnp.full((rows_q, LANES), NEG, F32), 0.0
        else:
            m_old = jnp.concatenate([gather(m_scr.at[hs0 + h], q_runs) for h in range(n_rs)], axis=0)
        m_new = jnp.maximum(m_old, m_blk)
        p = jnp.exp2(s - jnp.concatenate([m_new] * (nk // LANES), axis=1))
        pv = jnp.dot(p.astype(BF16), vaug, preferred_element_type=F32)
        alpha = jnp.exp2(m_old - m_new)
        if first:
            acc = pv[:, :LANES]
            l_new = pv[:, LANES:] + alpha * l_old
        else:
            acc = alpha * jnp.concatenate([gather(acc_scr.at[hs0 + h], q_runs) for h in range(n_rs)], axis=0)
            acc = acc + pv[:, :LANES]
            l_new = alpha * jnp.concatenate([gather(l_scr.at[hs0 + h], q_runs) for h in range(n_rs)], axis=0)
            l_new = l_new + pv[:, LANES:]
        if last:
            out = acc / l_new
            for u in range(units_per_blk):
                lo = (unit0 + u) * LANES
                if heads_per_unit == 1:
                    piece = out[u * BAND:(u + 1) * BAND]
                else:
                    piece = jnp.where(lane_q < d, out[2 * u * BAND:(2 * u + 1) * BAND],
                                      out[(2 * u + 1) * BAND:(2 * u + 2) * BAND])

                def store_o(rows, val, lo=lo):
                    o_ref[rows, lo:lo + LANES] = val

                scatter(store_o, q_runs, piece.astype(o_ref.dtype))
        else:
            for h in range(n_rs):
                for scr, val in ((acc_scr, acc), (l_scr, l_new), (m_scr, m_new)):
                    def store_s(rows, v, scr=scr, h=hs0 + h):
                        scr[h, rows, :] = v

                    scatter(store_s, q_runs, val[h * BAND:(h + 1) * BAND])

    for p_idx, (window, r) in enumerate(patterns):
        assert window // r == BAND and n_classes % r == 0
        s = n_classes // r
        qlen = BAND // s
        nblk = npc // qlen
        valid_first, valid_rest = band_mask(s, True), band_mask(s, False)
        streams = [(kvj, ug, cr) for kvj in range(kv_per_block) for ug in range(head_split) for cr in range(r)]

        def one_block(kvj, ug, cr, m, first_blk, p_idx=p_idx, r=r, s=s, qlen=qlen,
                      valid_first=valid_first, valid_rest=valid_rest):
            base = [npc * (cr + r * a) for a in range(s)]
            if first_blk:
                q_runs = [(b0, qlen) for b0 in base]
                block(p_idx, kvj, ug, q_runs, q_runs, valid_first)
            else:
                off = pl.multiple_of(qlen * m, qlen)
                q_runs = [(b0 + off, qlen) for b0 in base]
                k_runs = [(b0 + off - qlen, 2 * qlen) for b0 in base]
                block(p_idx, kvj, ug, q_runs, k_runs, valid_rest)

        for kvj, ug, cr in streams:
            one_block(kvj, ug, cr, 0, True)
        if nblk > 1:
            unroll = 1
            if len(streams) < 3:
                unroll = next((u for u in (2, 3, 4, 5) if (nblk - 1) % u == 0 and len(streams) * u >= 3), 1)

            def body(it, _, streams=streams, unroll=unroll, one_block=one_block):
                for u in range(unroll):
                    for kvj, ug, cr in streams:
                        one_block(kvj, ug, cr, 1 + it * unroll + u, False)
                return 0

            lax.fori_loop(0, (nblk - 1) // unroll, body, 0)


def _class_major_positions(seq, n_classes):
    return jnp.arange(seq).reshape(seq // n_classes, n_classes).T.reshape(seq)


def _band_attention(y3, sinks, *, patterns, n_classes, d, n_q_heads, n_kv_heads, q_col, k_col, v_col, name,
                    head_split=1):
    nb, seq, _ = y3.shape
    kv_per_block = LANES // d
    n_kvb = n_kv_heads // kv_per_block
    qw = n_q_heads * d // n_kvb
    n_rs = n_q_heads // n_kv_heads
    cos, sin = _rope_tables(_class_major_positions(seq, n_classes), d)
    kern = functools.partial(_band_attn_kernel, patterns=patterns, d=d, kv_per_block=kv_per_block,
                             head_split=head_split, n_classes=n_classes, seq=seq, scale=d ** -0.5,
                             has_sink=sinks is not None)
    in_specs = [pl.BlockSpec((None, seq, qw), lambda b, j, *_: (b, 0, q_col // qw + j)),
                pl.BlockSpec((None, seq, LANES), lambda b, j, *_: (b, 0, k_col // LANES + j)),
                pl.BlockSpec((None, seq, LANES), lambda b, j, *_: (b, 0, v_col // LANES + j)),
                pl.BlockSpec((seq, LANES), lambda b, j, *_: (0, 0)),
                pl.BlockSpec((seq, LANES), lambda b, j, *_: (0, 0))]
    out_specs = [pl.BlockSpec((None, seq, qw), lambda b, j, *_: (b, 0, j)),
                 pl.BlockSpec((None, seq, LANES), lambda b, j, *_: (b, 0, j))]
    out_shape = [jax.ShapeDtypeStruct((nb, seq, n_q_heads * d), BF16),
                 jax.ShapeDtypeStruct((nb, seq, n_kv_heads * d), F32)]
    scratch = [pltpu.VMEM((seq, qw), F32)]
    if len(patterns) > 1:
        scratch += [pltpu.VMEM((n_rs, seq, LANES), F32) for _ in range(3)]
    assert q_col % qw == 0 and k_col % LANES == 0 and v_col % LANES == 0
    if sinks is None:
        gs = pl.GridSpec(grid=(nb, n_kvb), in_specs=in_specs, out_specs=out_specs, scratch_shapes=scratch)
        args = (y3, y3, y3, cos, sin)
    else:
        gs = pltpu.PrefetchScalarGridSpec(num_scalar_prefetch=1, grid=(nb, n_kvb), in_specs=in_specs,
                                          out_specs=out_specs, scratch_shapes=scratch)
        args = (sinks.astype(F32), y3, y3, y3, cos, sin)
    return pl.pallas_call(kern, grid_spec=gs, out_shape=out_shape,
                          compiler_params=_cparams(("parallel", "arbitrary")), name=name)(*args)


def _mlstm_kernel(q_ref, k_ref, v_ref, og_ref, igc_ref, lfc_ref, igr_ref, lfr_ref, hn_ref,
                  o_ref, c_ref, n_ref, m_ref, *, rows_per_step, chunk, heads):
    L = chunk
    ri = lax.broadcasted_iota(jnp.int32, (L, L), 0)
    ci = lax.broadcasted_iota(jnp.int32, (L, L), 1)
    tril = ri >= ci
    tril_f = tril.astype(F32)
    triu_f = (ri <= ci).astype(F32)
    hp = lax.Precision.HIGHEST

    @pl.when(pl.program_id(2) == 0)
    def _():
        c_ref[...] = jnp.zeros_like(c_ref)
        n_ref[...] = jnp.zeros_like(n_ref)
        m_ref[...] = jnp.zeros_like(m_ref)

    def head_step(h, c):
        rows = pl.ds(pl.multiple_of(c * L, L), L)
        kq, kv = slice(h * A_DK, (h + 1) * A_DK), slice(h * A_DV, (h + 1) * A_DV)
        m = m_ref[h]
        qc = q_ref[rows, kq] * (A_DK ** -0.5)
        kc = k_ref[rows, kq]
        vc = v_ref[rows, kv].astype(BF16)
        icol, fcol = igc_ref[h, rows, :], lfc_ref[h, rows, :]
        irow, frow = igr_ref[h, c], lfr_ref[h, c]
        bcol = jnp.dot(tril_f, jnp.broadcast_to(fcol, (L, LANES)), precision=hp,
                       preferred_element_type=F32)[:, :1]
        brow = jnp.dot(jnp.broadcast_to(frow, (8, L)), triu_f, precision=hp, preferred_element_type=F32)[:1, :]
        acol, arow = icol - bcol, irow - brow
        log_d = jnp.where(tril, bcol + arow, NEG)
        inter = bcol + m
        mt = jnp.maximum(inter, jnp.max(log_d, axis=-1, keepdims=True))
        dmat = jnp.exp(log_d - mt)
        wi = jnp.exp(inter - mt)
        qb = qc.astype(BF16)
        sc = lax.dot_general(qb, kc.astype(BF16), (((1,), (1,)), ((), ())), preferred_element_type=F32) * dmat
        cm = c_ref[h]
        nv = n_ref[h]
        num = (jnp.dot(sc.astype(BF16), vc, preferred_element_type=F32)
               + wi * jnp.dot(qb, cm.astype(BF16), preferred_element_type=F32))
        den = jnp.sum(sc, axis=-1, keepdims=True) + wi * jnp.sum(qc * nv, axis=-1, keepdims=True)
        hc = num / jnp.maximum(jnp.abs(den), jnp.exp(-mt))
        hn = hc * lax.rsqrt(jnp.mean(hc * hc, axis=-1, keepdims=True) + NORM_EPS) * hn_ref[h]
        o_ref[rows, kv] = (hn * jax.nn.sigmoid(og_ref[rows, kv])).astype(o_ref.dtype)
        blast = bcol[L - 1:L, :]
        m_new = jnp.maximum(blast + m, jnp.max(blast + arow, axis=-1, keepdims=True))
        w_c = jnp.exp(blast + m - m_new)
        kw = kc * jnp.exp(blast + acol - m_new)
        c_ref[h] = w_c * cm + lax.dot_general(kw.astype(BF16), vc, (((0,), (0,)), ((), ())),
                                              preferred_element_type=F32)
        n_ref[h] = w_c * nv + jnp.sum(kw, axis=0, keepdims=True)
        m_ref[h] = m_new

    def step(c, _):
        for h in range(heads):
            head_step(h, c)
        return 0

    lax.fori_loop(0, rows_per_step // L, step, 0)


def _mlstm_prompt(y3, ig, lf, head_norm, *, q_col, k_col, v_col, o_col, heads=A_HEADS, rows_per_step=512):
    nb, seq, _ = y3.shape
    L = A_CHUNK
    nc = seq // L
    rows_per_step = min(rows_per_step, seq)
    cps = rows_per_step // L
    igc = jnp.transpose(ig, (0, 2, 1))[..., None]
    lfc = jnp.transpose(lf, (0, 2, 1))[..., None]
    igr = igc.reshape(nb, A_HEADS, nc, 1, L)
    lfr = lfc.reshape(nb, A_HEADS, nc, 1, L)
    hn = head_norm.reshape(A_HEADS, 1, A_DV).astype(F32)
    col = lambda c0, w: pl.BlockSpec((None, rows_per_step, heads * w),
                                     lambda b, g, s: (b, s, c0 // (heads * w) + g))
    gcol = pl.BlockSpec((None, heads, rows_per_step, 1), lambda b, g, s: (b, g, s, 0))
    grow = pl.BlockSpec((None, heads, cps, 1, L), lambda b, g, s: (b, g, s, 0, 0))
    state = lambda *tail: pl.BlockSpec((None, heads) + tail, lambda b, g, s: (b, g) + (0,) * len(tail))
    assert seq % rows_per_step == 0 and A_HEADS % heads == 0 and all(
        c0 % (heads * w) == 0 for c0, w in ((q_col, A_DK), (k_col, A_DK), (v_col, A_DV), (o_col, A_DV)))
    return pl.pallas_call(
        functools.partial(_mlstm_kernel, rows_per_step=rows_per_step, chunk=L, heads=heads),
        grid=(nb, A_HEADS // heads, seq // rows_per_step),
        in_specs=[col(q_col, A_DK), col(k_col, A_DK), col(v_col, A_DV), col(o_col, A_DV),
                  gcol, gcol, grow, grow,
                  pl.BlockSpec((heads, 1, A_DV), lambda b, g, s: (g, 0, 0))],
        out_specs=[pl.BlockSpec((None, rows_per_step, heads * A_DV), lambda b, g, s: (b, s, g)),
                   state(A_DK, A_DV), state(1, A_DK), state(1, 1)],
        out_shape=[jax.ShapeDtypeStruct((nb, seq, A_HEADS * A_DV), BF16),
                   jax.ShapeDtypeStruct((nb, A_HEADS, A_DK, A_DV), F32),
                   jax.ShapeDtypeStruct((nb, A_HEADS, 1, A_DK), F32),
                   jax.ShapeDtypeStruct((nb, A_HEADS, 1, 1), F32)],
        compiler_params=_cparams(("parallel", "parallel", "arbitrary")),
        name="mlstm_prompt",
    )(y3, y3, y3, y3, igc, lfc, igr, lfr, hn)


def _mlstm_step_kernel(q_ref, k_ref, v_ref, og_ref, ig_ref, lf_ref, c0_ref, n0_ref, m0_ref, hn_ref,
                       o_ref, c1_ref, n1_ref, m1_ref):
    for h in range(A_HEADS):
        q = q_ref[h] * (A_DK ** -0.5)
        k = k_ref[h]
        v = v_ref[h]
        c0 = c0_ref[h]
        n0 = n0_ref[h]
        ig, lf, m0 = ig_ref[h], lf_ref[h], m0_ref[h]
        inter = lf + m0
        mt = jnp.maximum(inter, ig)
        dm = jnp.exp(ig - mt)
        wi = jnp.exp(inter - mt)
        sc = jnp.sum(q * k, axis=0, keepdims=True) * dm
        qc = jnp.sum(c0 * q, axis=0, keepdims=True)
        num = sc * v + wi * qc
        den = sc + wi * jnp.sum(q * n0, axis=0, keepdims=True)
        hc = num / jnp.maximum(jnp.abs(den), jnp.exp(-mt))
        hn = hc * lax.rsqrt(jnp.mean(hc * hc, axis=-1, keepdims=True) + NORM_EPS) * hn_ref[h]
        o_ref[h] = (hn * jax.nn.sigmoid(og_ref[h])).astype(o_ref.dtype)
        c1_ref[h] = wi * c0 + (k * dm) * v
        n1_ref[h] = wi * n0 + dm * k
        m1_ref[h] = mt


def _mlstm_sample(y, ig, lf, head_norm, c0, n0, m0):
    nb = y.shape[0]
    col = lambda a: a.reshape(nb, A_HEADS, A_DK, 1)
    row = lambda a: a.reshape(nb, A_HEADS, 1, A_DV)
    sca = lambda a: a.astype(F32).reshape(nb, A_HEADS, 1, 1)
    cspec = pl.BlockSpec((None, A_HEADS, A_DK, 1), lambda b: (b, 0, 0, 0))
    rspec = pl.BlockSpec((None, A_HEADS, 1, A_DV), lambda b: (b, 0, 0, 0))
    sspec = pl.BlockSpec((None, A_HEADS, 1, 1), lambda b: (b, 0, 0, 0))
    mspec = pl.BlockSpec((None, A_HEADS, A_DK, A_DV), lambda b: (b, 0, 0, 0))
    o, c1, n1, m1 = pl.pallas_call(
        _mlstm_step_kernel,
        grid=(nb,),
        in_specs=[cspec, cspec, rspec, rspec, sspec, sspec, mspec, cspec, sspec,
                  pl.BlockSpec((A_HEADS, 1, A_DV), lambda b: (0, 0, 0))],
        out_specs=[rspec, mspec, cspec, sspec],
        out_shape=[jax.ShapeDtypeStruct((nb, A_HEADS, 1, A_DV), F32),
                   jax.ShapeDtypeStruct((nb, A_HEADS, A_DK, A_DV), F32),
                   jax.ShapeDtypeStruct((nb, A_HEADS, A_DK, 1), F32),
                   jax.ShapeDtypeStruct((nb, A_HEADS, 1, 1), F32)],
        compiler_params=_cparams(("parallel",)),
        name="mlstm_step",
    )(col(y[:, A_Q:A_Q + A_HEADS * A_DK]), col(y[:, A_K:A_K + A_HEADS * A_DK]),
      row(y[:, A_V:A_V + A_HEADS * A_DV]), row(y[:, A_O:A_O + A_HEADS * A_DV]),
      sca(ig), sca(lf), c0.astype(F32), col(n0.astype(F32)), sca(m0),
      head_norm.astype(F32).reshape(A_HEADS, 1, A_DV))
    return (o.reshape(nb, A_HEADS * A_DV), c1, n1.reshape(nb, A_HEADS, A_DK), m1.reshape(nb, A_HEADS))


def _decode_attn_kernel(*refs, n_pat, d, group, scale, has_sink):
    q_ref, kn_ref, vn_ref, cos_ref, sin_ref = refs[:5]
    refs = refs[5:]
    if has_sink:
        sink_ref, refs = refs[0], refs[1:]
    k_refs, v_refs = refs[:n_pat], refs[n_pat:2 * n_pat]
    o_ref, knew_ref = refs[2 * n_pat:]
    n_seg = LANES // d
    cos, sin = cos_ref[...], sin_ref[...]
    kn = _rope_lanes(kn_ref[...], cos, sin, d)
    knew_ref[...] = kn
    vn = vn_ref[...]
    lane = lax.broadcasted_iota(jnp.int32, kn.shape, 1)

    def seg_sum(p):
        if n_seg == 1:
            return jnp.sum(p, axis=-1, keepdims=True)
        lane_p = lax.broadcasted_iota(jnp.int32, p.shape, p.ndim - 1)
        lo = jnp.sum(jnp.where(lane_p < d, p, 0.0), axis=-1, keepdims=True)
        hi = jnp.sum(jnp.where(lane_p >= d, p, 0.0), axis=-1, keepdims=True)
        return jnp.where(lane_p < d, lo, hi)

    for g in range(group):
        qg = _rope_lanes(q_ref[g], cos, sin, d)
        s_new = seg_sum(kn * qg) * scale
        s = [seg_sum(k_refs[p][...] * qg[None]) * scale for p in range(n_pat)]
        m = s_new
        for p in range(n_pat):
            m = jnp.maximum(m, jnp.max(s[p], axis=0))
        if has_sink:
            m = jnp.maximum(m, sink_ref[g])
        e_new = jnp.exp(s_new - m) * float(n_pat)
        l = e_new
        acc = e_new * vn
        for p in range(n_pat):
            e = jnp.exp(s[p] - m[None])
            l = l + jnp.sum(e, axis=0)
            acc = acc + jnp.sum(e * v_refs[p][...], axis=0)
        if has_sink:
            l = l + jnp.exp(sink_ref[g] - m)
        o_ref[g] = (acc / l).astype(o_ref.dtype)


def _decode_attention(q, k_new, v_new, k_cache, v_cache, sinks, pos, *, patterns, d, name):
    nb, n_heads, _ = q.shape
    n_kv, cache_len = k_cache.shape[2], k_cache.shape[1]
    group = n_heads // n_kv
    n_seg = LANES // d
    r_rows = n_kv // n_seg
    to_tiles = lambda a, inner: a.reshape(nb, r_rows, n_seg, inner, d).transpose(0, 3, 1, 2, 4).reshape(
        nb, inner, r_rows, LANES)
    qt = to_tiles(q, group)
    knt = k_new.reshape(nb, r_rows, LANES)
    vnt = v_new.reshape(nb, r_rows, LANES)
    cos, sin = _rope_tables(pos, d)
    kc = k_cache.reshape(nb, cache_len, r_rows, LANES)
    vc = v_cache.reshape(nb, cache_len, r_rows, LANES)
    tile = pl.BlockSpec((None, r_rows, LANES), lambda b: (b, 0, 0))
    one = pl.BlockSpec((1, LANES), lambda b: (0, 0))
    in_specs = [pl.BlockSpec((None, group, r_rows, LANES), lambda b: (b, 0, 0, 0)), tile, tile, one, one]
    args = [qt, knt, vnt, cos, sin]
    if sinks is not None:
        sk = jnp.broadcast_to(sinks.astype(F32).reshape(r_rows, n_seg, group, 1), (r_rows, n_seg, group, d))
        args.append(sk.transpose(2, 0, 1, 3).reshape(group, r_rows, LANES))
        in_specs.append(pl.BlockSpec((group, r_rows, LANES), lambda b: (0, 0, 0)))
    views = []
    for window, r in patterns:
        assert window // r == BAND and cache_len % (BAND * r) == 0
        blk = cache_len // r // BAND - 1
        views.append(pl.BlockSpec((None, BAND, None, r_rows, LANES), lambda b, blk=blk: (b, blk, 0, 0, 0)))
    in_specs += views + views
    args += [kc.reshape(nb, cache_len // r, r, r_rows, LANES) for _, r in patterns]
    args += [vc.reshape(nb, cache_len // r, r, r_rows, LANES) for _, r in patterns]
    o, kn = pl.pallas_call(
        functools.partial(_decode_attn_kernel, n_pat=len(patterns), d=d, group=group, scale=d ** -0.5,
                          has_sink=sinks is not None),
        grid=(nb,),
        in_specs=in_specs,
        out_specs=[pl.BlockSpec((None, group, r_rows, LANES), lambda b: (b, 0, 0, 0)), tile],
        out_shape=[jax.ShapeDtypeStruct((nb, group, r_rows, LANES), F32),
                   jax.ShapeDtypeStruct((nb, r_rows, LANES), F32)],
        compiler_params=_cparams(("parallel",)),
        name=name,
    )(*args)
    o = o.reshape(nb, group, r_rows, n_seg, d).transpose(0, 2, 3, 1, 4).reshape(nb, n_heads * d)
    return o, kn.reshape(nb, n_kv, d)


def _shift_kernel(c_ref, new_ref, o_ref, *, length, chunk):
    n_full = (length - 1) // chunk

    def body(i, _):
        o_ref[pl.ds(i * chunk, chunk)] = c_ref[pl.ds(i * chunk + 1, chunk)]
        return 0

    lax.fori_loop(0, n_full, body, 0)
    done = n_full * chunk
    if done < length - 1:
        o_ref[done:length - 1] = c_ref[done + 1:length]
    o_ref[length - 1] = new_ref[...]


def _shift_cache(cache, new, name):
    nb, length, n_kv, d = cache.shape
    return pl.pallas_call(
        functools.partial(_shift_kernel, length=length, chunk=min(64, length - 1)),
        grid=(nb,),
        in_specs=[pl.BlockSpec((None, length, n_kv, d), lambda b: (b, 0, 0, 0)),
                  pl.BlockSpec((None, n_kv, d), lambda b: (b, 0, 0))],
        out_specs=pl.BlockSpec((None, length, n_kv, d), lambda b: (b, 0, 0, 0)),
        out_shape=jax.ShapeDtypeStruct(cache.shape, cache.dtype),
        compiler_params=_cparams(("parallel",)),
        name=name,
    )(cache, new.astype(cache.dtype))


N_IN_EVEN = 2 * A_HEADS * A_DK + 2 * A_HEADS * A_DV + 2 * A_HEADS + (B_HEADS + 2 * B_KV_HEADS) * B_HEAD_DIM
A_Q, A_K, A_V, A_O = 0, 1024, 2048, 4096
GATE_COL = 6144
B_COL = GATE_COL + 2 * A_HEADS
B_Q, B_K, B_V = 0, 2048, 2304
C_Q, C_K, C_V = 0, 4096, 5120
SWA_PATTERN = ((B_WINDOW, 1),)


def _even_projection(hp, hs, w_in, j):
    w_t = jnp.swapaxes(w_in, 1, 2)
    ya = _matmul2([(hp, hs, w_t, j, 0)], GATE_COL, F32, 1024, 512, "even_in_a", w_rows_are_outputs=True)
    yb = _matmul2([(hp, hs, w_t, j, 0)], N_IN_EVEN - B_COL, F32, 1024, 512, "even_in_b",
                  w_rows_are_outputs=True, w_row0=B_COL)
    w_gate = jnp.pad(w_in[j, :, GATE_COL:B_COL], ((0, 0), (0, LANES - 2 * A_HEADS)))[None]
    yg = _matmul2([(hp, hs, w_gate, 0, 0)], LANES, F32, 1024, LANES, "even_in_gates")
    return [(ya[g], yg[g][:, :2 * A_HEADS], yb[g]) for g in range(2)]


def _gates(yg, gate_b):
    gb = gate_b.astype(F32)
    ig = yg[..., :A_HEADS] + gb[:A_HEADS]
    lf = jax.nn.log_sigmoid(yg[..., A_HEADS:] + gb[A_HEADS:])
    return ig, lf


def _even_mixer_prompt(proj, nb, seq, gate_b, head_norm, sinks):
    y, yg, yb = proj
    y3, yb3 = y.reshape(nb, seq, -1), yb.reshape(nb, seq, -1)
    ig, lf = _gates(yg.reshape(nb, seq, -1), gate_b)
    ha, c1, n1, m1 = _mlstm_prompt(y3, ig, lf, head_norm, q_col=A_Q, k_col=A_K, v_col=A_V, o_col=A_O)
    ob, kb = _band_attention(yb3, sinks, patterns=SWA_PATTERN, n_classes=1, d=B_HEAD_DIM, n_q_heads=B_HEADS,
                             n_kv_heads=B_KV_HEADS, q_col=B_Q, k_col=B_K, v_col=B_V, name="swa_prompt",
                             head_split=1)
    kbuf = kb[:, -B_WINDOW:].reshape(nb, -1, B_KV_HEADS, B_HEAD_DIM)
    vbuf = yb3[:, -B_WINDOW:, B_V:].reshape(nb, -1, B_KV_HEADS, B_HEAD_DIM)
    state = (c1, n1.reshape(nb, A_HEADS, A_DK), m1.reshape(nb, A_HEADS), kbuf, vbuf)
    return ha.reshape(nb * seq, -1), ob.reshape(nb * seq, -1), state


def _even_mixer_sample(proj, gate_b, head_norm, sinks, state):
    y, yg, yb = proj
    nb = y.shape[0]
    ig, lf = _gates(yg, gate_b)
    c0, n0, m0, k_old, v_old = state
    ha, c1, n1, m1 = _mlstm_sample(y, ig, lf, head_norm, c0, n0, m0)
    q = yb[:, B_Q:B_K].reshape(nb, B_HEADS, B_HEAD_DIM)
    k_new = yb[:, B_K:B_V].reshape(nb, B_KV_HEADS, B_HEAD_DIM)
    v_new = yb[:, B_V:].reshape(nb, B_KV_HEADS, B_HEAD_DIM)
    ob, k_rot = _decode_attention(q, k_new, v_new, k_old, v_old, sinks, PAST_LEN + jnp.arange(1),
                                  patterns=SWA_PATTERN, d=B_HEAD_DIM, name="swa_step")
    state = (c1, n1, m1, _shift_cache(k_old, k_rot, "swa_k_shift"), _shift_cache(v_old, v_new, "swa_v_shift"))
    return ha.astype(BF16), ob.astype(BF16), state


def _even_mixer(hp, hs, nb, seq, j, w_in, gate_b, head_norm, sinks, w_out, state):
    proj_p, proj_s = _even_projection(hp, hs, w_in, j)
    ha_p, ob_p, st_p = _even_mixer_prompt(proj_p, nb, seq, gate_b, head_norm, sinks)
    ha_s, ob_s, st_s = _even_mixer_sample(proj_s, gate_b, head_norm, sinks, state)
    outs = _matmul2([(ha_p, ha_s, w_out, j, 0), (ob_p, ob_s, w_out, j, 1)], D_MODEL, F32, 1024, 512, "even_out")
    return outs, (st_p, st_s)


def _odd_mixer(hp, hs, nb, seq, j, w_in, w_out, state):
    out_p, st_p = _odd_mixer_prompt(hp, nb, seq, j, w_in, w_out)
    out_s, st_s = _odd_mixer_sample(hs, j, w_in, w_out, state)
    return (out_p, out_s), (st_p, st_s)


C_CLASSES = max(r for _, r in C_PATTERNS)


def _odd_mixer_prompt(h2, nb, seq, j, w_in, w_out):
    y3 = _matmul_regroup(h2, w_in, j, w_in.shape[2], nb, C_CLASSES, True, 1024, 512, "odd_in").reshape(nb, seq, -1)
    o, kr = _band_attention(y3, None, patterns=C_PATTERNS, n_classes=C_CLASSES, d=C_HEAD_DIM, n_q_heads=C_HEADS,
                            n_kv_heads=C_KV_HEADS, q_col=C_Q, k_col=C_K, v_col=C_V, name="dil_prompt")
    out = _matmul_regroup(o.reshape(nb, C_CLASSES, seq // C_CLASSES, -1), w_out, j, D_MODEL, nb, C_CLASSES, False,
                          1024, 512, "odd_out")
    natural = lambda a: a.reshape(nb, C_CLASSES, seq // C_CLASSES, C_KV_HEADS, C_HEAD_DIM).transpose(
        0, 2, 1, 3, 4).reshape(nb, seq, C_KV_HEADS, C_HEAD_DIM)
    return out, (natural(kr), natural(y3[:, :, C_V:]))


def _odd_mixer_sample(h2, j, w_in, w_out, state):
    nb = h2.shape[0]
    y = _matmul([(h2, w_in, j, 0)], w_in.shape[2], F32, nb, 512, "odd_in_s")
    q = y[:, C_Q:C_K].reshape(nb, C_HEADS, C_HEAD_DIM)
    k_new = y[:, C_K:C_V].reshape(nb, C_KV_HEADS, C_HEAD_DIM)
    v_new = y[:, C_V:].reshape(nb, C_KV_HEADS, C_HEAD_DIM)
    k_old, v_old = state
    o, k_rot = _decode_attention(q, k_new, v_new, k_old, v_old, None, PAST_LEN + jnp.arange(1),
                                 patterns=C_PATTERNS, d=C_HEAD_DIM, name="dil_step")
    out = _matmul([(o.astype(BF16), w_out, j, 0)], D_MODEL, F32, nb, 512, "odd_out_s")
    return out, (_shift_cache(k_old, k_rot, "dil_k_shift"), _shift_cache(v_old, v_new, "dil_v_shift"))


def _trunk(xs, rows, mixers, norms, ffn):
    g_mix_pre, g_mix_post, g_ffn_pre, g_ffn_post = norms
    wg, wu, wd = ffn
    xs = list(xs)
    hs = [_prenorm(r, x, g_mix_pre[0], 0, MOD_SC1, MOD_SH1) for r, x in zip(rows, xs)]
    states = []
    for li in range(DEPTH):
        ys, state = mixers[li](*hs)
        states.append(state)
        for g in range(2):
            xs[g], hs[g] = _postnorm(rows[g], xs[g], ys[g], g_mix_post[li], li, MOD_GT1,
                                     (g_ffn_pre[li], li, MOD_SC2, MOD_SH2))
        acts = _matmul2([(hs[0], hs[1], (wg, wu), li, 0)], wg.shape[2], BF16, 1024, 256, "swiglu", swiglu=True)
        fs = [_matmul([(a, wd, li, 0)], D_MODEL, F32, 512, 512, "ffn_down") for a in acts]
        nxt = (g_mix_pre[li + 1], li + 1, MOD_SC1, MOD_SH1) if li + 1 < DEPTH else None
        for g in range(2):
            xs[g], hs[g] = _postnorm(rows[g], xs[g], fs[g], g_ffn_post[li], li, MOD_GT2, nxt)
    return xs, states


def kernel(x_prompt, x_sample, state_l0_mlstm_c, state_l0_mlstm_n, state_l0_mlstm_m, cache_l0_swa_k, cache_l0_swa_v, cache_l1_dil_k, cache_l1_dil_v, state_l2_mlstm_c, state_l2_mlstm_n, state_l2_mlstm_m, cache_l2_swa_k, cache_l2_swa_v, cache_l3_dil_k, cache_l3_dil_v, c_prompt, c_sample, ada_w, ada_b, norm_mix_pre, norm_mix_post, norm_ffn_pre, norm_ffn_post, even_w_in, even_gate_b, even_head_norm, even_sinks, even_w_out, odd_w_in, odd_w_out, ffn_w_gate, ffn_w_up, ffn_w_down):
    past = {0: (state_l0_mlstm_c, state_l0_mlstm_n, state_l0_mlstm_m, cache_l0_swa_k, cache_l0_swa_v),
            1: (cache_l1_dil_k, cache_l1_dil_v),
            2: (state_l2_mlstm_c, state_l2_mlstm_n, state_l2_mlstm_m, cache_l2_swa_k, cache_l2_swa_v),
            3: (cache_l3_dil_k, cache_l3_dil_v)}
    nbp, seq, dm = x_prompt.shape
    nbs = x_sample.shape[0]
    c_all = jnp.concatenate([c_sample, c_prompt], axis=0)
    pad = (-c_all.shape[0]) % 16
    sc_all = jnp.pad(jax.nn.silu(c_all), ((0, pad), (0, 0))).astype(BF16)
    mod_all = _ada(sc_all, ada_w, ada_b)
    mixers = []
    for li in range(DEPTH):
        j = li // 2
        if li % 2 == 0:
            mw = (j, even_w_in, even_gate_b[j], even_head_norm[j], even_sinks[j], even_w_out, past[li])
            mixers.append(lambda hp, hs, mw=mw: _even_mixer(hp, hs, nbp, seq, *mw))
        else:
            mw = (j, odd_w_in, odd_w_out, past[li])
            mixers.append(lambda hp, hs, mw=mw: _odd_mixer(hp, hs, nbp, seq, *mw))
    norms = (norm_mix_pre, norm_mix_post, norm_ffn_pre, norm_ffn_post)
    ffn = (ffn_w_gate, ffn_w_up, ffn_w_down.astype(BF16))
    rows = (_Rows(mod_all, nbp * seq, seq, nbs, 256), _Rows(mod_all, nbs, 1, 0, nbs))
    (y_p, y_s), states = _trunk((x_prompt.reshape(nbp * seq, dm), x_sample.reshape(nbs, dm)), rows, mixers,
                                norms, ffn)
    y_p, y_s = y_p.reshape(x_prompt.shape), y_s.reshape(x_sample.shape)
    new_p, new_s = [s[0] for s in states], [s[1] for s in states]
    (p0c, p0n, p0m, p0k, p0v), (p1k, p1v), (p2c, p2n, p2m, p2k, p2v), (p3k, p3v) = new_p
    (s0c, s0n, s0m, s0k, s0v), (s1k, s1v), (s2c, s2n, s2m, s2k, s2v), (s3k, s3v) = new_s
    return (y_p, y_s,
            p0c, p0n, p0m, p0k, p0v, p1k, p1v, p2c, p2n, p2m, p2k, p2v, p3k, p3v,
            s0c, s0n, s0m, s0k, s0v, s1k, s1v, s2c, s2n, s2m, s2k, s2v, s3k, s3v)
```
